```python
import math
import jax
import jax.numpy as jnp
from jax import lax
import numpy as np

D_MODEL = 1024
BATCH = 8
SEQ = 2048
DEPTH = 2

HEAD_DIM = 64
D_MIX = D_MODEL
GROUP_W = D_MIX // 4
Q_BLOCK = 128
EPS = 1e-6
DIL_HEADS = GROUP_W // HEAD_DIM
DILATED_PATTERNS = ((128, 1), (512, 4), (2048, 16))
DIFF_HEADS = 4
DIFF_HEAD_DIM = GROUP_W // (2 * DIFF_HEADS)
SSM_HEADS = GROUP_W // HEAD_DIM
SSM_HEAD_DIM = HEAD_DIM
SSM_GROUPS = 2
SSM_STATE = 128
SSM_CONV = 5
SSM_CHUNK = 128
SSM_XBC = GROUP_W + 2 * SSM_GROUPS * SSM_STATE
NA_HEADS = GROUP_W // HEAD_DIM
GRID_W = 64
NA_WIN_H = 8
NA_WIN_W = 16
PROJ_WIDTHS = (GROUP_W, GROUP_W, GROUP_W, GROUP_W,
               GROUP_W, GROUP_W, GROUP_W, GROUP_W,
               GROUP_W, SSM_XBC, 2 * SSM_HEADS,
               GROUP_W, GROUP_W, GROUP_W, GROUP_W)
D_IN = 13 * GROUP_W + SSM_XBC + 2 * SSM_HEADS

kernel_name = 'hybrid_parallel_group_encoder'


def rms_norm(x, w):
    x32 = x.astype(jnp.float32)
    y = x32 * lax.rsqrt(jnp.mean(x32 * x32, axis=-1, keepdims=True) + EPS)
    return (y * w.astype(jnp.float32)).astype(x.dtype)


def split_cols(t, widths):
    outs = []
    start = 0
    for w in widths:
        outs.append(t[..., start:start + w])
        start += w
    return outs


def to_heads(t, n):
    b, s, _ = t.shape
    return t.reshape(b, s, n, -1).transpose(0, 2, 1, 3)


def to_pair_heads(t):
    b, s, _ = t.shape
    return t.reshape(b, s, DIFF_HEADS, 2, DIFF_HEAD_DIM).transpose(0, 2, 3, 1, 4)


def from_heads(t):
    b, h, s, d = t.shape
    return t.transpose(0, 2, 1, 3).reshape(b, s, h * d)


def alibi_slopes():
    n = DIL_HEADS + DIFF_HEADS
    m = 2.0 ** (-8.0 * jnp.arange(1, n + 1, dtype=jnp.float32) / n)
    return m[0::2], m[1::2]


def dilated_attention(q, k, v, slopes):
    T = q.shape[2]
    scale = q.shape[-1] ** -0.5
    qf, kf, vf = q.astype(jnp.float32), k.astype(jnp.float32), v.astype(jnp.float32)
    pats = []
    for w, r in DILATED_PATTERNS:
        half = w // (2 * r)
        offs = r * np.arange(-half, half + 1)
        pats.append((offs, jnp.asarray(np.abs(offs), jnp.float32)))

    def block(blk):
        t0 = blk * Q_BLOCK
        pos = t0 + jnp.arange(Q_BLOCK)
        qb = lax.dynamic_slice_in_dim(qf, t0, Q_BLOCK, axis=2)
        outs, lses = [], []
        for offs, dist in pats:
            idx = pos[:, None] + offs[None, :]
            valid = (idx >= 0) & (idx < T)
            idx = jnp.clip(idx, 0, T - 1)
            kg = kf[:, :, idx]
            vg = vf[:, :, idx]
            s = jnp.einsum('bhqd,bhqkd->bhqk', qb, kg) * scale - slopes[:, None, None] * dist[None, None, :]
            s = jnp.where(valid[None, None], s, -jnp.inf)
            lse = jax.nn.logsumexp(s, axis=-1)
            outs.append(jnp.einsum('bhqk,bhqkd->bhqd', jnp.exp(s - lse[..., None]), vg))
            lses.append(lse)
        alpha = jax.nn.softmax(jnp.stack(lses), axis=0)
        return jnp.einsum('pbhq,pbhqd->bhqd', alpha, jnp.stack(outs))

    out = lax.map(block, jnp.arange(T // Q_BLOCK))
    return out.transpose(1, 2, 0, 3, 4).reshape(q.shape).astype(q.dtype)


def diff_attention(q, k, v, lam, slopes):
    T = v.shape[2]
    scale = q.shape[-1] ** -0.5
    qf, kf, vf = q.astype(jnp.float32), k.astype(jnp.float32), v.astype(jnp.float32)
    key_pos = jnp.arange(T)

    def block(blk):
        t0 = blk * Q_BLOCK
        pos = t0 + jnp.arange(Q_BLOCK)
        qb = lax.dynamic_slice_in_dim(qf, t0, Q_BLOCK, axis=3)
        bias = -slopes[:, None, None] * jnp.abs(pos[:, None] - key_pos[None, :]).astype(jnp.float32)
        s = jnp.einsum('bhiqd,bhikd->bhiqk', qb, kf) * scale + bias[None, :, None]
        a = jax.nn.softmax(s, axis=-1)
        return jnp.einsum('bhqk,bhkd->bhqd', a[:, :, 0] - lam * a[:, :, 1], vf)

    out = lax.map(block, jnp.arange(T // Q_BLOCK))
    return out.transpose(1, 2, 0, 3, 4).reshape(v.shape).astype(v.dtype)


def segsum(a):
    L = a.shape[-1]
    a_rep = jnp.broadcast_to(a[..., :, None], a.shape + (L,))
    a_rep = jnp.where(jnp.tril(jnp.ones((L, L), bool), -1), a_rep, 0.0)
    cs = jnp.cumsum(a_rep, axis=-2)
    return jnp.where(jnp.tril(jnp.ones((L, L), bool)), cs, -jnp.inf)


def ssd_scan(X, A, Bm, Cm):
    b, T, h, p = X.shape
    nc, l = T // SSM_CHUNK, SSM_CHUNK
    X = X.reshape(b, nc, l, h, p)
    Bm = Bm.reshape(b, nc, l, h, -1)
    Cm = Cm.reshape(b, nc, l, h, -1)
    A = A.reshape(b, nc, l, h).transpose(0, 3, 1, 2)
    A_cum = jnp.cumsum(A, axis=-1)
    Lmat = jnp.exp(segsum(A))
    CB = jnp.einsum('bclhn,bcshn->bhcls', Cm, Bm)
    y_diag = jnp.einsum('bhcls,bcshp->bclhp', CB * Lmat, X)
    decay_states = jnp.exp(A_cum[..., -1:] - A_cum)
    states = jnp.einsum('bclhn,bhcl,bclhp->bchpn', Bm, decay_states, X)
    states = jnp.concatenate([jnp.zeros_like(states[:, :1]), states], axis=1)
    decay_chunk = jnp.exp(segsum(jnp.pad(A_cum[..., -1], ((0, 0), (0, 0), (1, 0)))))
    states = jnp.einsum('bhzc,bchpn->bzhpn', decay_chunk, states)[:, :-1]
    y_off = jnp.einsum('bclhn,bchpn,bhcl->bclhp', Cm, states, jnp.exp(A_cum))
    return (y_diag + y_off).reshape(b, T, h, p)


def ssd_mixer(z, xbc, dt_raw, conv_w, conv_b, a_log, dt_bias, d_skip, norm_w):
    b, T, _ = xbc.shape
    xbc = lax.conv_general_dilated(xbc, conv_w[:, None, :].astype(xbc.dtype), window_strides=(1,),
                                   padding=[(SSM_CONV // 2, SSM_CONV // 2)],
                                   dimension_numbers=('NWC', 'WIO', 'NWC'),
                                   feature_group_count=SSM_XBC) + conv_b
    xbc = jax.nn.silu(xbc).astype(jnp.float32)
    xs, Bm, Cm = split_cols(xbc, (GROUP_W, SSM_GROUPS * SSM_STATE, SSM_GROUPS * SSM_STATE))
    rep = SSM_HEADS // SSM_GROUPS
    xs = xs.reshape(b, T, SSM_HEADS, SSM_HEAD_DIM)
    Bm = jnp.repeat(Bm.reshape(b, T, SSM_GROUPS, SSM_STATE), rep, axis=2)
    Cm = jnp.repeat(Cm.reshape(b, T, SSM_GROUPS, SSM_STATE), rep, axis=2)
    dt = jax.nn.softplus(dt_raw.reshape(b, T, 2, SSM_HEADS).astype(jnp.float32) + dt_bias.astype(jnp.float32))
    A = -jnp.exp(a_log.astype(jnp.float32))
    y_f = ssd_scan(xs * dt[:, :, 0, :, None], A[0] * dt[:, :, 0], Bm, Cm)
    flip = lambda t: jnp.flip(t, axis=1)
    y_b = flip(ssd_scan(flip(xs * dt[:, :, 1, :, None]), flip(A[1] * dt[:, :, 1]), flip(Bm), flip(Cm)))
    y = y_f + y_b + xs * d_skip.astype(jnp.float32)[:, None]
    y = y.reshape(b, T, GROUP_W) * jax.nn.silu(z.astype(jnp.float32))
    yg = y.reshape(b, T, SSM_GROUPS, GROUP_W // SSM_GROUPS)
    yg = yg * lax.rsqrt(jnp.mean(yg * yg, axis=-1, keepdims=True) + EPS)
    return (yg.reshape(b, T, GROUP_W) * norm_w.astype(jnp.float32)).astype(z.dtype)


def neighborhood_attention(q, k, v, rpb):
    b, h, T, d = q.shape
    rows = T // GRID_W
    kh, kw = min(NA_WIN_H, rows), NA_WIN_W
    scale = d ** -0.5
    qg = q.reshape(b, h, rows, GRID_W, d).astype(jnp.float32)
    kg = k.reshape(b, h, rows, GRID_W, d).astype(jnp.float32)
    vg = v.reshape(b, h, rows, GRID_W, d).astype(jnp.float32)
    cols = np.arange(GRID_W)
    col_idx = np.clip(cols - kw // 2, 0, GRID_W - kw)[:, None] + np.arange(kw)[None, :]
    dc = col_idx - cols[:, None] + NA_WIN_W - 1
    rpb = rpb.astype(jnp.float32)

    def row(r):
        rs = jnp.clip(r - kh // 2, 0, rows - kh)
        qr = lax.dynamic_index_in_dim(qg, r, axis=2, keepdims=False)
        kr = lax.dynamic_slice_in_dim(kg, rs, kh, axis=2)[:, :, :, col_idx]
        vr = lax.dynamic_slice_in_dim(vg, rs, kh, axis=2)[:, :, :, col_idx]
        dr = rs + jnp.arange(kh) - r + NA_WIN_H - 1
        bias = rpb[:, dr[None, :, None], dc[:, None, :]]
        s = jnp.einsum('bhqd,bhrqkd->bhqrk', qr, kr) * scale + bias[None]
        p = jax.nn.softmax(s.reshape(b, h, GRID_W, kh * kw), axis=-1).reshape(b, h, GRID_W, kh, kw)
        return jnp.einsum('bhqrk,bhrqkd->bhqd', p, vr)

    out = lax.map(row, jnp.arange(rows))
    return out.transpose(1, 2, 0, 3, 4).reshape(b, h, T, d).astype(q.dtype)


def setup_inputs(seed: int = 0) -> dict:
    key = jax.random.key(seed)
    ks = jax.random.split(key, 17)
    f32 = jnp.float32

    def nrm(k, shape, s):
        return s * jax.random.normal(k, shape, f32)

    x = nrm(ks[0], (BATCH, SEQ, D_MODEL), 1.0)
    c = nrm(ks[1], (BATCH, D_MODEL), 1.0)
    norm_w = 1.0 + nrm(ks[2], (DEPTH, D_MODEL), 0.02)
    ada_w = nrm(ks[3], (DEPTH, D_MODEL, 3 * D_MODEL), D_MODEL ** -0.5)
    ada_b = nrm(ks[4], (DEPTH, 3 * D_MODEL), 0.02)
    w_in = nrm(ks[5], (DEPTH, D_MODEL, D_IN), D_MODEL ** -0.5)
    diff_lambda = nrm(ks[6], (DEPTH, 4, DIFF_HEAD_DIM), 0.1)
    diff_norm_w = 1.0 + nrm(ks[7], (DEPTH, 2 * DIFF_HEAD_DIM), 0.02)
    conv_w = nrm(ks[8], (DEPTH, SSM_CONV, SSM_XBC), SSM_CONV ** -0.5)
    conv_b = nrm(ks[9], (DEPTH, SSM_XBC), 0.02)
    ssm_a_log = jnp.log(jax.random.uniform(ks[10], (DEPTH, 2, SSM_HEADS), f32, 1.0, 16.0))
    dt0 = jnp.exp(jax.random.uniform(ks[11], (DEPTH, 2, SSM_HEADS), f32, math.log(1e-3), math.log(1e-1)))
    ssm_dt_bias = dt0 + jnp.log(-jnp.expm1(-dt0))
    ssm_d = 1.0 + nrm(ks[12], (DEPTH, SSM_HEADS), 0.1)
    ssm_norm_w = 1.0 + nrm(ks[13], (DEPTH, GROUP_W), 0.02)
    na_rpb = nrm(ks[14], (DEPTH, NA_HEADS, 2 * NA_WIN_H - 1, 2 * NA_WIN_W - 1), 0.02)
    w_out = nrm(ks[15], (DEPTH, D_MIX, D_MODEL), D_MIX ** -0.5)
    final_norm_w = 1.0 + nrm(ks[16], (D_MODEL,), 0.02)
    return {'x': x, 'c': c, 'norm_w': norm_w, 'ada_w': ada_w, 'ada_b': ada_b, 'w_in': w_in,
            'diff_lambda': diff_lambda, 'diff_norm_w': diff_norm_w, 'conv_w': conv_w, 'conv_b': conv_b,
            'ssm_a_log': ssm_a_log, 'ssm_dt_bias': ssm_dt_bias, 'ssm_d': ssm_d, 'ssm_norm_w': ssm_norm_w,
            'na_rpb': na_rpb, 'w_out': w_out, 'final_norm_w': final_norm_w}


def reference(x, c, norm_w, ada_w, ada_b, w_in, diff_lambda, diff_norm_w, conv_w, conv_b,
              ssm_a_log, ssm_dt_bias, ssm_d, ssm_norm_w, na_rpb, w_out, final_norm_w):
    slopes_a, slopes_b = alibi_slopes()
    c_act = jax.nn.silu(c)
    for l in range(DEPTH):
        mod = c_act @ ada_w[l] + ada_b[l]
        shift, scale, gate = jnp.split(mod[:, None, :], 3, axis=-1)
        h = rms_norm(x, norm_w[l]) * (1.0 + scale) + shift
        proj = h @ w_in[l]
        (aq, ak, av, ag, bq, bk, bv, bg, cz, cxbc, cdt, dq, dk, dv, dg) = split_cols(proj, PROJ_WIDTHS)
        o_a = dilated_attention(to_heads(aq, DIL_HEADS), to_heads(ak, DIL_HEADS), to_heads(av, DIL_HEADS), slopes_a)
        y_a = from_heads(o_a) * jax.nn.silu(ag)
        lam_init = 0.8 - 0.6 * math.exp(-0.3 * l)
        lv = diff_lambda[l].astype(jnp.float32)
        lam = jnp.exp(jnp.sum(lv[0] * lv[1])) - jnp.exp(jnp.sum(lv[2] * lv[3])) + lam_init
        o_b = diff_attention(to_pair_heads(bq), to_pair_heads(bk), to_heads(bv, DIFF_HEADS), lam, slopes_b)
        o_b = rms_norm(o_b, diff_norm_w[l]) * (1.0 - lam_init)
        y_b = from_heads(o_b) * jax.nn.silu(bg)
        y_c = ssd_mixer(cz, cxbc, cdt, conv_w[l], conv_b[l], ssm_a_log[l], ssm_dt_bias[l], ssm_d[l], ssm_norm_w[l])
        o_d = neighborhood_attention(to_heads(dq, NA_HEADS), to_heads(dk, NA_HEADS), to_heads(dv, NA_HEADS), na_rpb[l])
        y_d = from_heads(o_d) * jax.nn.silu(dg)
        y = jnp.concatenate([y_a, y_b, y_c, y_d], axis=-1) @ w_out[l]
        x = x + gate * y
    return rms_norm(x, final_norm_w)
```

```python
import functools
import math

import jax
import jax.numpy as jnp
from jax import lax
from jax.experimental import pallas as pl
from jax.experimental.pallas import tpu as pltpu

D_MODEL = 1024
SEQ = 2048
DEPTH = 2
HEAD_DIM = 64
GROUP_W = 256
N_HEADS = 4
EPS = 1e-6
DILATED_PATTERNS = ((128, 1), (512, 4), (2048, 16))
DIFF_HEAD_DIM = 32
SSM_GROUPS = 2
SSM_STATE = 128
SSM_CONV = 5
SSM_CHUNK = 128
SSM_XBC = 768
GRID_W = 64
NA_WIN_H = 8
NA_WIN_W = 16
D_IN = 13 * GROUP_W + SSM_XBC + 2 * N_HEADS

LANES = 128
SUBLANES = 8
VMEM_LIMIT = 56 * 1024 * 1024

NEG = -1e30
F32 = jnp.float32
BF16 = jnp.bfloat16
HIGHEST = lax.Precision.HIGHEST

TQ = 256
TK = 512
ROW_TILE = 512
DT_PAD = LANES
NJ = (2 * SEQ - TQ) // LANES
TOEP_OFF = SEQ - TQ


def _silu(x):
    return x / (1.0 + jnp.exp(-x))


def _dot_nt(a, b):
    return lax.dot_general(a, b, (((1,), (1,)), ((), ())), preferred_element_type=F32)


def _dot(a, b):
    return jnp.dot(a, b, preferred_element_type=F32)


def _params(n_grid):
    return pltpu.CompilerParams(dimension_semantics=("arbitrary",) * n_grid,
                                vmem_limit_bytes=VMEM_LIMIT)


def _head_mask(h, width=HEAD_DIM, total=GROUP_W):
    lane = lax.broadcasted_iota(jnp.int32, (1, total), 1)
    return (lane >= h * width) & (lane < (h + 1) * width)


def _mod_kernel(c_ref, w_ref, b_ref, o_ref):
    c = c_ref[...]
    o_ref[0] = jnp.dot(_silu(c), w_ref[0], precision=HIGHEST,
                       preferred_element_type=F32) + b_ref[0]


def _adaln(c, ada_w, ada_b):
    bsz = c.shape[0]
    tn = 768
    return pl.pallas_call(
        _mod_kernel,
        grid=(DEPTH, 3 * D_MODEL // tn),
        in_specs=[pl.BlockSpec((bsz, D_MODEL), lambda l, j: (0, 0)),
                  pl.BlockSpec((1, D_MODEL, tn), lambda l, j: (l, 0, j)),
                  pl.BlockSpec((1, 1, tn), lambda l, j: (l, 0, j))],
        out_specs=pl.BlockSpec((1, bsz, tn), lambda l, j: (l, 0, j)),
        out_shape=jax.ShapeDtypeStruct((DEPTH, bsz, 3 * D_MODEL), F32),
        compiler_params=_params(2),
        name="adaln_mod",
    )(c, ada_w, ada_b.reshape(DEPTH, 1, 3 * D_MODEL))


def _inproj_kernel(x_ref, mod_ref, nw_ref, w_ref, pa_ref, pb_ref, pc_ref, pd_ref, pdt_ref):
    x = x_ref[0]
    shift = mod_ref[0, :, 0:D_MODEL]
    scale = mod_ref[0, :, D_MODEL:2 * D_MODEL]
    y = x * lax.rsqrt(jnp.mean(x * x, axis=-1, keepdims=True) + EPS) * nw_ref[...]
    h = (y * (1.0 + scale) + shift).astype(BF16)
    for i, ref in enumerate((pa_ref, pb_ref, pc_ref, pd_ref)):
        ref[0] = _dot(h, w_ref[:, i * 1024:(i + 1) * 1024]).astype(BF16)
    pdt_ref[0] = _dot(h, w_ref[:, 4096:4096 + DT_PAD])


def _inproj(x, mod3, norm_w, w_r):
    bsz = x.shape[0]
    nw = w_r.shape[1]
    row = lambda b, t: (b, t, 0)
    big = pl.BlockSpec((1, ROW_TILE, 1024), row)
    return pl.pallas_call(
        _inproj_kernel,
        grid=(bsz, SEQ // ROW_TILE),
        in_specs=[pl.BlockSpec((1, ROW_TILE, D_MODEL), row),
                  pl.BlockSpec((1, 1, 3 * D_MODEL), lambda b, t: (b, 0, 0)),
                  pl.BlockSpec((1, D_MODEL), lambda b, t: (0, 0)),
                  pl.BlockSpec((D_MODEL, nw), lambda b, t: (0, 0))],
        out_specs=[big, big, big, big, pl.BlockSpec((1, ROW_TILE, DT_PAD), row)],
        out_shape=[jax.ShapeDtypeStruct((bsz, SEQ, 1024), BF16)] * 4
        + [jax.ShapeDtypeStruct((bsz, SEQ, DT_PAD), F32)],
        compiler_params=_params(2),
        name="inproj",
    )(x, mod3, norm_w, w_r)


def _outproj_kernel(ya_ref, yb_ref, yc_ref, yd_ref, w_ref, x_ref, mod_ref, fw_ref, o_ref, *, final):
    acc = None
    for i, ref in enumerate((ya_ref, yb_ref, yc_ref, yd_ref)):
        part = _dot(ref[0], w_ref[i * GROUP_W:(i + 1) * GROUP_W, :])
        acc = part if acc is None else acc + part
    gate = mod_ref[0, :, 2 * D_MODEL:3 * D_MODEL]
    xn = x_ref[0] + gate * acc
    if final:
        xn = xn * lax.rsqrt(jnp.mean(xn * xn, axis=-1, keepdims=True) + EPS) * fw_ref[...]
    o_ref[0] = xn


def _outproj(ys, w_out, x, mod3, final_w, final):
    bsz = x.shape[0]
    row = lambda b, t: (b, t, 0)
    yspec = pl.BlockSpec((1, ROW_TILE, GROUP_W), row)
    return pl.pallas_call(
        functools.partial(_outproj_kernel, final=final),
        grid=(bsz, SEQ // ROW_TILE),
        in_specs=[yspec, yspec, yspec, yspec,
                  pl.BlockSpec((D_MODEL, D_MODEL), lambda b, t: (0, 0)),
                  pl.BlockSpec((1, ROW_TILE, D_MODEL), row),
                  pl.BlockSpec((1, 1, 3 * D_MODEL), lambda b, t: (b, 0, 0)),
                  pl.BlockSpec((1, D_MODEL), lambda b, t: (0, 0))],
        out_specs=pl.BlockSpec((1, ROW_TILE, D_MODEL), row),
        out_shape=jax.ShapeDtypeStruct((bsz, SEQ, D_MODEL), F32),
        compiler_params=_params(2),
        name="outproj_final" if final else "outproj",
    )(*ys, w_out, x, mod3, final_w)


def _build_toeplitz(tab_ref, slopes, patterns):
    r_io = lax.broadcasted_iota(jnp.int32, (TQ, LANES), 0)
    c_io = lax.broadcasted_iota(jnp.int32, (TQ, LANES), 1)
    base = c_io - r_io - TOEP_OFF

    def body(j, _):
        d = base + j * LANES
        ad = jnp.abs(d)
        adf = ad.astype(F32)
        if patterns is None:
            logm = jnp.zeros((TQ, LANES), F32)
            valid = None
        else:
            mult = jnp.zeros((TQ, LANES), F32)
            for w, r in patterns:
                reach = r * (w // (2 * r))
                hit = jnp.where(ad <= reach, jnp.where((d & (r - 1)) == 0, 1.0, 0.0), 0.0)
                mult = mult + hit
            valid = mult > 0.5
            logm = jnp.log(jnp.maximum(mult, 1.0))
        for h, slope in enumerate(slopes):
            val = logm - slope * adf
            if valid is not None:
                val = jnp.where(valid, val, NEG)
            tab_ref[h, j] = val
        return 0

    lax.fori_loop(0, NJ, body, 0)


def _bias_tile(tab_ref, h, q0, k0, s):
    base = (k0 - q0 + TOEP_OFF) // LANES
    cols = [s[:, c * LANES:(c + 1) * LANES] + tab_ref[h, base + c] for c in range(TK // LANES)]
    return jnp.concatenate(cols, axis=-1)


def _flash_rows(qm, k_ref, v_ref, tab_ref, h, q0):
    def kstep(j, carry):
        m, l, acc = carry
        k0 = pl.multiple_of(j * TK, TK)
        s = _dot_nt(qm, k_ref[0, pl.ds(k0, TK), :])
        e = _bias_tile(tab_ref, h, q0, k0, s)
        m_new = jnp.maximum(m, jnp.max(e, axis=-1, keepdims=True))
        alpha = jnp.exp(m - m_new)
        p = jnp.exp(e - m_new)
        l = alpha * l + jnp.sum(p, axis=-1, keepdims=True)
        acc = alpha * acc + _dot(p.astype(BF16), v_ref[0, pl.ds(k0, TK), :])
        return m_new, l, acc

    init = (jnp.full((TQ, 1), NEG, F32), jnp.zeros((TQ, 1), F32), jnp.zeros((TQ, GROUP_W), F32))
    _, l, acc = lax.fori_loop(0, SEQ // TK, kstep, init)
    return l, acc


def _mixer_a_kernel(q_ref, k_ref, v_ref, g_ref, o_ref, tab_ref, *, slopes):
    @pl.when(pl.program_id(0) == 0)
    def _():
        _build_toeplitz(tab_ref, slopes, DILATED_PATTERNS)

    def qstep(i, _):
        q0 = pl.multiple_of(i * TQ, TQ)
        q = q_ref[0, pl.ds(q0, TQ), :]
        out = jnp.zeros((TQ, GROUP_W), F32)
        for h in range(N_HEADS):
            hm = _head_mask(h)
            qm = jnp.where(hm, q, jnp.zeros_like(q))
            l, acc = _flash_rows(qm, k_ref, v_ref, tab_ref, h, q0)
            out = jnp.where(hm, acc / l, out)
        g = g_ref[0, pl.ds(q0, TQ), :].astype(F32)
        o_ref[0, pl.ds(q0, TQ), :] = (out * _silu(g)).astype(BF16)
        return 0

    lax.fori_loop(0, SEQ // TQ, qstep, 0)


def _attn_specs():
    def col(j):
        return pl.BlockSpec((1, SEQ, GROUP_W), lambda b: (b, 0, j))
    return [col(0), col(1), col(2), col(3)]


def _mixer_a(pa, slopes):
    bsz = pa.shape[0]
    return pl.pallas_call(
        functools.partial(_mixer_a_kernel, slopes=slopes),
        grid=(bsz,),
        in_specs=_attn_specs(),
        out_specs=pl.BlockSpec((1, SEQ, GROUP_W), lambda b: (b, 0, 0)),
        out_shape=jax.ShapeDtypeStruct((bsz, SEQ, GROUP_W), BF16),
        scratch_shapes=[pltpu.VMEM((N_HEADS, NJ, TQ, LANES), F32)],
        compiler_params=_params(1),
        name="mixer_a_dilated",
    )(pa, pa, pa, pa)


def _mixer_b_kernel(q_ref, k_ref, v_ref, g_ref, lam_ref, nw_ref, o_ref, tab_ref, *, slopes, lam_init):
    @pl.when(pl.program_id(0) == 0)
    def _():
        _build_toeplitz(tab_ref, slopes, None)

    lv = lam_ref[...]
    lam = (jnp.exp(jnp.sum(lv[0:1] * lv[1:2], axis=-1, keepdims=True))
           - jnp.exp(jnp.sum(lv[2:3] * lv[3:4], axis=-1, keepdims=True)) + lam_init)
    nw = nw_ref[...]

    def qstep(i, _):
        q0 = pl.multiple_of(i * TQ, TQ)
        q = q_ref[0, pl.ds(q0, TQ), :]
        out = jnp.zeros((TQ, GROUP_W), F32)
        for h in range(N_HEADS):
            hm = _head_mask(h)
            halves = []
            for i2 in range(2):
                sub = _head_mask(2 * h + i2, DIFF_HEAD_DIM)
                qm = jnp.where(sub, q, jnp.zeros_like(q))
                l, acc = _flash_rows(qm, k_ref, v_ref, tab_ref, h, q0)
                halves.append(acc / l)
            o = halves[0] - lam * halves[1]
            ms = jnp.sum(jnp.where(hm, o * o, 0.0), axis=-1, keepdims=True) * (1.0 / HEAD_DIM)
            on = o * lax.rsqrt(ms + EPS) * nw * (1.0 - lam_init)
            out = jnp.where(hm, on, out)
        g = g_ref[0, pl.ds(q0, TQ), :].astype(F32)
        o_ref[0, pl.ds(q0, TQ), :] = (out * _silu(g)).astype(BF16)
        return 0

    lax.fori_loop(0, SEQ // TQ, qstep, 0)


def _mixer_b(pb, lam_p, nw256, slopes, lam_init):
    bsz = pb.shape[0]
    return pl.pallas_call(
        functools.partial(_mixer_b_kernel, slopes=slopes, lam_init=lam_init),
        grid=(bsz,),
        in_specs=_attn_specs() + [pl.BlockSpec((4, DIFF_HEAD_DIM), lambda b: (0, 0)),
                                  pl.BlockSpec((1, GROUP_W), lambda b: (0, 0))],
        out_specs=pl.BlockSpec((1, SEQ, GROUP_W), lambda b: (b, 0, 0)),
        out_shape=jax.ShapeDtypeStruct((bsz, SEQ, GROUP_W), BF16),
        scratch_shapes=[pltpu.VMEM((N_HEADS, NJ, TQ, LANES), F32)],
        compiler_params=_params(1),
        name="mixer_b_diff",
    )(pb, pb, pb, pb, lam_p, nw256)


CONV_TILE = 256
CONV_HALO = SUBLANES
N_CHUNK = SEQ // SSM_CHUNK


def _cumsum_rows(a):
    row = lax.broadcasted_iota(jnp.int32, a.shape, 0)
    s = 1
    while s < a.shape[0]:
        a = a + jnp.where(row >= s, pltpu.roll(a, s, 0), 0.0)
        s *= 2
    return a


def _mixer_c_kernel(p_ref, dt_ref, cw_ref, cb_ref, alog_ref, dtb_ref, dskip_ref, nw_ref, exp_ref,
                    o_ref, xpad, xc, y_s, sb_s, cs_s, db_s):
    L = SSM_CHUNK
    zero_rows = jnp.zeros((CONV_HALO, SSM_XBC), F32)
    xpad[0:CONV_HALO, :] = zero_rows
    xpad[CONV_HALO + SEQ:CONV_HALO + SEQ + CONV_HALO, :] = zero_rows

    def fill(i, _):
        r0 = pl.multiple_of(i * CONV_TILE, CONV_TILE)
        xpad[pl.ds(CONV_HALO + r0, CONV_TILE), :] = p_ref[0, pl.ds(r0, CONV_TILE), GROUP_W:].astype(F32)
        return 0

    lax.fori_loop(0, SEQ // CONV_TILE, fill, 0)

    def conv(i, _):
        r0 = pl.multiple_of(i * CONV_TILE, CONV_TILE)
        win = xpad[pl.ds(r0, CONV_TILE + 2 * CONV_HALO), :]
        acc = jnp.zeros((CONV_TILE, SSM_XBC), F32) + cb_ref[...]
        for j in range(SSM_CONV):
            off = CONV_HALO - SSM_CONV // 2 + j
            acc = acc + cw_ref[j:j + 1, :] * win[off:off + CONV_TILE, :]
        xc[pl.ds(r0, CONV_TILE), :] = _silu(acc)
        return 0

    lax.fori_loop(0, SEQ // CONV_TILE, conv, 0)

    a_neg = -jnp.exp(alog_ref[...])
    expand = exp_ref[...]
    li = lax.broadcasted_iota(jnp.int32, (L, L), 0)
    si = lax.broadcasted_iota(jnp.int32, (L, L), 1)
    lower = si <= li
    upper = si >= li
    hmasks = [_head_mask(h) for h in range(N_HEADS)]

    def chunk_terms(t0):
        dtr = dt_ref[0, pl.ds(t0, L), :] + dtb_ref[...]
        dt = jnp.maximum(dtr, 0.0) + jnp.log(1.0 + jnp.exp(-jnp.abs(dtr)))
        a = dt * a_neg
        ainc = _cumsum_rows(a)
        aexc = ainc - a
        return dt, ainc, aexc

    def fwd(c, hf):
        t0 = pl.multiple_of(c * L, L)
        dt, ainc, aexc = chunk_terms(t0)
        dt_x = jnp.dot(dt, expand, precision=HIGHEST, preferred_element_type=F32)
        ainc_x = jnp.dot(ainc, expand, precision=HIGHEST, preferred_element_type=F32)
        aexc_x = jnp.dot(aexc, expand, precision=HIGHEST, preferred_element_type=F32)
        ainc_t = ainc.T
        aexc_t = aexc.T
        xs = xc[pl.ds(t0, L), 0:GROUP_W]
        bm = xc[pl.ds(t0, L), GROUP_W:2 * GROUP_W]
        cm = xc[pl.ds(t0, L), 2 * GROUP_W:3 * GROUP_W]
        xf = xs * dt_x[:, 0:GROUP_W]
        xb = xs * dt_x[:, GROUP_W:2 * GROUP_W]
        xcat = jnp.concatenate([xf, xb], axis=0).astype(BF16)
        tot_f = ainc_x[L - 1:L, 0:GROUP_W]
        tot_b = ainc_x[L - 1:L, GROUP_W:2 * GROUP_W]
        y = xs * dskip_ref[...]
        cbs = []
        for g in range(SSM_GROUPS):
            gs = slice(g * SSM_STATE, (g + 1) * SSM_STATE)
            cbs.append(_dot_nt(cm[:, gs].astype(BF16), bm[:, gs].astype(BF16)))
        for h in range(N_HEADS):
            cb = cbs[h // (N_HEADS // SSM_GROUPS)]
            col_f = ainc[:, h:h + 1]
            row_f = ainc_t[h:h + 1, :]
            col_b = aexc[:, N_HEADS + h:N_HEADS + h + 1]
            row_b = aexc_t[N_HEADS + h:N_HEADS + h + 1, :]
            lf = jnp.exp(jnp.where(lower, col_f - row_f, NEG))
            ub = jnp.exp(jnp.where(upper, row_b - col_b, NEG))
            mcat = jnp.concatenate([cb * lf, cb * ub], axis=1).astype(BF16)
            y = y + jnp.where(hmasks[h], _dot(mcat, xcat), 0.0)
        wf = (jnp.exp(tot_f - ainc_x[:, 0:GROUP_W]) * xf).astype(BF16)
        wb = (jnp.exp(aexc_x[:, GROUP_W:2 * GROUP_W]) * xb).astype(BF16)
        ef = jnp.exp(ainc_x[:, 0:GROUP_W])
        cs_s[pl.ds(t0, L), :] = jnp.exp(tot_b - aexc_x[:, GROUP_W:2 * GROUP_W])
        db_s[pl.ds(pl.multiple_of(c * SUBLANES, SUBLANES), SUBLANES), :] = jnp.broadcast_to(
            jnp.exp(tot_b), (SUBLANES, GROUP_W))
        dec_f = jnp.exp(tot_f)
        hf_new = []
        yoff = []
        for g in range(SSM_GROUPS):
            gs = slice(g * SSM_STATE, (g + 1) * SSM_STATE)
            bt = bm[:, gs].T.astype(BF16)
            yoff.append(_dot(cm[:, gs].astype(BF16), hf[g].astype(BF16)))
            hf_new.append(dec_f[:, gs] * hf[g] + _dot(bt, wf[:, gs]))
            sb_s[pl.ds(t0, L), gs] = _dot(bt, wb[:, gs])
        y = y + jnp.concatenate(yoff, axis=1) * ef
        y_s[pl.ds(t0, L), :] = y
        return tuple(hf_new)

    h0 = tuple(jnp.zeros((SSM_STATE, SSM_STATE), F32) for _ in range(SSM_GROUPS))
    lax.fori_loop(0, N_CHUNK, fwd, h0)

    def bwd(i, hb):
        c = N_CHUNK - 1 - i
        t0 = pl.multiple_of(c * L, L)
        cm = xc[pl.ds(t0, L), 2 * GROUP_W:3 * GROUP_W]
        dec_b = db_s[pl.ds(pl.multiple_of(c * SUBLANES, SUBLANES), 1), :]
        yoff = []
        hb_new = []
        for g in range(SSM_GROUPS):
            gs = slice(g * SSM_STATE, (g + 1) * SSM_STATE)
            yoff.append(_dot(cm[:, gs].astype(BF16), hb[g].astype(BF16)))
            hb_new.append(dec_b[:, gs] * hb[g] + sb_s[pl.ds(t0, L), gs])
        y_s[pl.ds(t0, L), :] = y_s[pl.ds(t0, L), :] + jnp.concatenate(yoff, axis=1) * cs_s[pl.ds(t0, L), :]
        return tuple(hb_new)

    lax.fori_loop(0, N_CHUNK, bwd, h0)

    def fin(i, _):
        r0 = pl.multiple_of(i * CONV_TILE, CONV_TILE)
        z = p_ref[0, pl.ds(r0, CONV_TILE), 0:GROUP_W].astype(F32)
        y = y_s[pl.ds(r0, CONV_TILE), :] * _silu(z)
        parts = []
        for g in range(SSM_GROUPS):
            yg = y[:, g * SSM_STATE:(g + 1) * SSM_STATE]
            parts.append(yg * lax.rsqrt(jnp.mean(yg * yg, axis=-1, keepdims=True) + EPS))
        o_ref[0, pl.ds(r0, CONV_TILE), :] = (jnp.concatenate(parts, axis=1) * nw_ref[...]).astype(BF16)
        return 0

    lax.fori_loop(0, SEQ // CONV_TILE, fin, 0)


def _head_expand_matrix():
    j = lax.broadcasted_iota(jnp.int32, (LANES, 2 * GROUP_W), 0)
    c = lax.broadcasted_iota(jnp.int32, (LANES, 2 * GROUP_W), 1)
    return (j == c // HEAD_DIM).astype(F32)


def _mixer_c(pc, pdt, conv_w, conv_b, a_log, dt_bias, d_skip, norm_w):
    bsz = pc.shape[0]
    pad8 = lambda v: jnp.pad(v.reshape(1, 2 * N_HEADS), ((0, 0), (0, LANES - 2 * N_HEADS)))
    small = lambda shape: pl.BlockSpec(shape, lambda b: (0,) * len(shape))
    return pl.pallas_call(
        _mixer_c_kernel,
        grid=(bsz,),
        in_specs=[pl.BlockSpec((1, SEQ, 1024), lambda b: (b, 0, 0)),
                  pl.BlockSpec((1, SEQ, DT_PAD), lambda b: (b, 0, 0)),
                  small((SSM_CONV, SSM_XBC)), small((1, SSM_XBC)),
                  small((1, LANES)), small((1, LANES)),
                  small((1, GROUP_W)), small((1, GROUP_W)),
                  small((LANES, 2 * GROUP_W))],
        out_specs=pl.BlockSpec((1, SEQ, GROUP_W), lambda b: (b, 0, 0)),
        out_shape=jax.ShapeDtypeStruct((bsz, SEQ, GROUP_W), BF16),
        scratch_shapes=[pltpu.VMEM((SEQ + 2 * CONV_HALO, SSM_XBC), F32),
                        pltpu.VMEM((SEQ, SSM_XBC), F32),
                        pltpu.VMEM((SEQ, GROUP_W), F32),
                        pltpu.VMEM((SEQ, GROUP_W), F32),
                        pltpu.VMEM((SEQ, GROUP_W), F32),
                        pltpu.VMEM((N_CHUNK * SUBLANES, GROUP_W), F32)],
        compiler_params=_params(1),
        name="mixer_c_ssd",
    )(pc, pdt, conv_w, conv_b.reshape(1, SSM_XBC), pad8(a_log), pad8(dt_bias),
      jnp.repeat(d_skip, HEAD_DIM).reshape(1, GROUP_W), norm_w.reshape(1, GROUP_W),
      _head_expand_matrix())


NA_ROWS = SEQ // GRID_W
NA_DR = 2 * NA_WIN_H - 1
NA_DC = 2 * NA_WIN_W - 1
NA_PAIRS = NA_DR - 1


def _na_bias_kernel(rpb_ref, o_ref):
    lh = pl.program_id(0)
    cq = lax.broadcasted_iota(jnp.int32, (GRID_W, LANES), 0)
    lane = lax.broadcasted_iota(jnp.int32, (GRID_W, LANES), 1)
    second = lane >= GRID_W
    ck = jnp.where(second, lane - GRID_W, lane)
    cs = jnp.clip(cq - NA_WIN_W // 2, 0, GRID_W - NA_WIN_W)
    inside = (ck >= cs) & (ck < cs + NA_WIN_W)
    dc = ck - cq + NA_WIN_W - 1
    base = lh * (NA_DR * NA_DC)
    for p in range(NA_PAIRS):
        acc = jnp.zeros((GRID_W, LANES), F32)
        for j in range(NA_DC):
            lo = rpb_ref[base + p * NA_DC + j]
            hi = rpb_ref[base + (p + 1) * NA_DC + j]
            acc = acc + jnp.where(dc == j, jnp.where(second, hi, lo), 0.0)
        o_ref[0, p] = jnp.where(inside, acc, NEG)


def _na_bias(na_rpb):
    n = DEPTH * N_HEADS
    return pl.pallas_call(
        _na_bias_kernel,
        grid=(n,),
        in_specs=[pl.BlockSpec(memory_space=pltpu.SMEM)],
        out_specs=pl.BlockSpec((1, NA_PAIRS, GRID_W, LANES), lambda i: (i, 0, 0, 0)),
        out_shape=jax.ShapeDtypeStruct((n, NA_PAIRS, GRID_W, LANES), F32),
        compiler_params=_params(1),
        name="na_bias_table",
    )(na_rpb.reshape(-1))


def _mixer_d_kernel(q_ref, k_ref, v_ref, g_ref, bias_ref, o_ref):
    nkeys = NA_WIN_H * GRID_W

    def rstep(r, _):
        rs = jnp.clip(r - NA_WIN_H // 2, 0, NA_ROWS - NA_WIN_H)
        delta = r - rs
        q0 = pl.multiple_of(r * GRID_W, GRID_W)
        k0 = pl.multiple_of(rs * GRID_W, GRID_W)
        q = q_ref[0, pl.ds(q0, GRID_W), :]
        kw = k_ref[0, pl.ds(k0, nkeys), :]
        vw = v_ref[0, pl.ds(k0, nkeys), :]
        out = jnp.zeros((GRID_W, GROUP_W), F32)
        for h in range(N_HEADS):
            hm = _head_mask(h)
            qm = jnp.where(hm, q, jnp.zeros_like(q))
            s = _dot_nt(qm, kw)
            bias = jnp.concatenate(
                [bias_ref[h, 2 * i - delta + NA_WIN_H - 1] for i in range(NA_WIN_H // 2)], axis=-1)
            e = s + bias
            m = jnp.max(e, axis=-1, keepdims=True)
            p = jnp.exp(e - m)
            l = jnp.sum(p, axis=-1, keepdims=True)
            out = jnp.where(hm, _dot(p.astype(BF16), vw) / l, out)
        g = g_ref[0, pl.ds(q0, GRID_W), :].astype(F32)
        o_ref[0, pl.ds(q0, GRID_W), :] = (out * _silu(g)).astype(BF16)
        return 0

    lax.fori_loop(0, NA_ROWS, rstep, 0)


def _mixer_d(pd, bias_l):
    bsz = pd.shape[0]
    return pl.pallas_call(
        _mixer_d_kernel,
        grid=(bsz,),
        in_specs=_attn_specs() + [pl.BlockSpec((N_HEADS, NA_PAIRS, GRID_W, LANES), lambda b: (0, 0, 0, 0))],
        out_specs=pl.BlockSpec((1, SEQ, GROUP_W), lambda b: (b, 0, 0)),
        out_shape=jax.ShapeDtypeStruct((bsz, SEQ, GROUP_W), BF16),
        compiler_params=_params(1),
        name="mixer_d_neighbourhood",
    )(pd, pd, pd, pd, bias_l)


def _prep_w_in(w):
    scale = jnp.ones((D_IN,), F32)
    scale = scale.at[0:GROUP_W].set(HEAD_DIM ** -0.5)
    scale = scale.at[4 * GROUP_W:5 * GROUP_W].set(DIFF_HEAD_DIM ** -0.5)
    d0 = 9 * GROUP_W + SSM_XBC + 2 * N_HEADS
    scale = scale.at[d0:d0 + GROUP_W].set(HEAD_DIM ** -0.5)
    w = w * scale[None, :]
    dt0 = 9 * GROUP_W + SSM_XBC
    w_r = jnp.concatenate([w[:, :dt0], w[:, d0:], w[:, dt0:d0],
                           jnp.zeros((D_MODEL, DT_PAD - 2 * N_HEADS), F32)], axis=1)
    return w_r.astype(BF16)


def kernel(x, c, norm_w, ada_w, ada_b, w_in, diff_lambda, diff_norm_w, conv_w, conv_b, ssm_a_log,
           ssm_dt_bias, ssm_d, ssm_norm_w, na_rpb, w_out, final_norm_w):
    bsz = x.shape[0]
    n = 2 * N_HEADS
    slopes = [2.0 ** (-8.0 * i / n) for i in range(1, n + 1)]
    slopes_a, slopes_b = tuple(slopes[0::2]), tuple(slopes[1::2])
    mod = _adaln(c, ada_w, ada_b)
    na_bias = _na_bias(na_rpb)
    final_w = final_norm_w.reshape(1, D_MODEL)
    for l in range(DEPTH):
        mod3 = mod[l].reshape(bsz, 1, 3 * D_MODEL)
        pa, pb, pc, pd, pdt = _inproj(x, mod3, norm_w[l].reshape(1, D_MODEL), _prep_w_in(w_in[l]))
        lam_init = 0.8 - 0.6 * math.exp(-0.3 * l)
        ya = _mixer_a(pa, slopes_a)
        yb = _mixer_b(pb, diff_lambda[l], jnp.tile(diff_norm_w[l], N_HEADS).reshape(1, GROUP_W),
                      slopes_b, lam_init)
        yc = _mixer_c(pc, pdt, conv_w[l], conv_b[l], ssm_a_log[l], ssm_dt_bias[l], ssm_d[l], ssm_norm_w[l])
        yd = _mixer_d(pd, na_bias[l * N_HEADS:(l + 1) * N_HEADS])
        x = _outproj((ya, yb, yc, yd), w_out[l].astype(BF16), x, mod3, final_w, final=(l == DEPTH - 1))
    return x
```

```python
import functools
import math

import jax
import jax.numpy as jnp
from jax import lax
from jax.experimental import pallas as pl
from jax.experimental.pallas import tpu as pltpu

D_MODEL = 1024
SEQ = 2048
DEPTH = 2
HEAD_DIM = 64
GROUP_W = 256
N_HEADS = 4
EPS = 1e-6
DILATED_PATTERNS = ((128, 1), (512, 4), (2048, 16))
DIFF_HEAD_DIM = 32
SSM_GROUPS = 2
SSM_STATE = 128
SSM_CONV = 5
SSM_CHUNK = 128
SSM_XBC = 768
GRID_W = 64
NA_WIN_H = 8
NA_WIN_W = 16
D_IN = 13 * GROUP_W + SSM_XBC + 2 * N_HEADS

LANES = 128
SUBLANES = 8
VMEM_LIMIT = 56 * 1024 * 1024

NEG = -1e30
LOG2E = math.log2(math.e)
F32 = jnp.float32
BF16 = jnp.bfloat16
HIGHEST = lax.Precision.HIGHEST

TQ = 128
SLAB = 16
ROW_TILE = 512
DT_PAD = LANES
NJ = (2 * SEQ - TQ) // LANES
TOEP_OFF = SEQ - TQ


def _silu(x):
    return x / (1.0 + jnp.exp(-x))


def _dot_nt(a, b):
    return lax.dot_general(a, b, (((1,), (1,)), ((), ())), preferred_element_type=F32)


def _dot(a, b):
    return jnp.dot(a, b, preferred_element_type=F32)


def _params(n_grid):
    return pltpu.CompilerParams(dimension_semantics=("arbitrary",) * n_grid,
                                vmem_limit_bytes=VMEM_LIMIT)


def _head_mask(h, width=HEAD_DIM, total=GROUP_W):
    lane = lax.broadcasted_iota(jnp.int32, (1, total), 1)
    return (lane >= h * width) & (lane < (h + 1) * width)


def _mod_kernel(c_ref, w_ref, b_ref, o_ref):
    c = c_ref[...]
    o_ref[0] = jnp.dot(_silu(c), w_ref[0], precision=HIGHEST,
                       preferred_element_type=F32) + b_ref[0]


def _adaln(c, ada_w, ada_b):
    bsz = c.shape[0]
    tn = 768
    return pl.pallas_call(
        _mod_kernel,
        grid=(DEPTH, 3 * D_MODEL // tn),
        in_specs=[pl.BlockSpec((bsz, D_MODEL), lambda l, j: (0, 0)),
                  pl.BlockSpec((1, D_MODEL, tn), lambda l, j: (l, 0, j)),
                  pl.BlockSpec((1, 1, tn), lambda l, j: (l, 0, j))],
        out_specs=pl.BlockSpec((1, bsz, tn), lambda l, j: (l, 0, j)),
        out_shape=jax.ShapeDtypeStruct((DEPTH, bsz, 3 * D_MODEL), F32),
        compiler_params=_params(2),
        name="adaln_mod",
    )(c, ada_w, ada_b.reshape(DEPTH, 1, 3 * D_MODEL))


def _inproj_kernel(x_ref, mod_ref, nw_ref, w_ref, pa_ref, pb_ref, pc_ref, pd_ref, pdt_ref):
    x = x_ref[0]
    shift = mod_ref[0, :, 0:D_MODEL]
    scale = mod_ref[0, :, D_MODEL:2 * D_MODEL]
    y = x * lax.rsqrt(jnp.mean(x * x, axis=-1, keepdims=True) + EPS) * nw_ref[...]
    h = (y * (1.0 + scale) + shift).astype(BF16)
    for i, ref in enumerate((pa_ref, pb_ref, pc_ref, pd_ref)):
        ref[0] = _dot(h, w_ref[:, i * 1024:(i + 1) * 1024]).astype(BF16)
    pdt_ref[0] = _dot(h, w_ref[:, 4096:4096 + DT_PAD])


def _inproj(x, mod3, norm_w, w_r):
    bsz = x.shape[0]
    nw = w_r.shape[1]
    row = lambda b, t: (b, t, 0)
    big = pl.BlockSpec((1, ROW_TILE, 1024), row)
    return pl.pallas_call(
        _inproj_kernel,
        grid=(bsz, SEQ // ROW_TILE),
        in_specs=[pl.BlockSpec((1, ROW_TILE, D_MODEL), row),
                  pl.BlockSpec((1, 1, 3 * D_MODEL), lambda b, t: (b, 0, 0)),
                  pl.BlockSpec((1, D_MODEL), lambda b, t: (0, 0)),
                  pl.BlockSpec((D_MODEL, nw), lambda b, t: (0, 0))],
        out_specs=[big, big, big, big, pl.BlockSpec((1, ROW_TILE, DT_PAD), row)],
        out_shape=[jax.ShapeDtypeStruct((bsz, SEQ, 1024), BF16)] * 4
        + [jax.ShapeDtypeStruct((bsz, SEQ, DT_PAD), F32)],
        compiler_params=_params(2),
        name="inproj",
    )(x, mod3, norm_w, w_r)


def _outproj_kernel(ya_ref, yb_ref, yc_ref, yd_ref, w_ref, x_ref, mod_ref, fw_ref, o_ref, *, final):
    acc = None
    for i, ref in enumerate((ya_ref, yb_ref, yc_ref, yd_ref)):
        part = _dot(ref[0], w_ref[i * GROUP_W:(i + 1) * GROUP_W, :])
        acc = part if acc is None else acc + part
    gate = mod_ref[0, :, 2 * D_MODEL:3 * D_MODEL]
    xn = x_ref[0] + gate * acc
    if final:
        xn = xn * lax.rsqrt(jnp.mean(xn * xn, axis=-1, keepdims=True) + EPS) * fw_ref[...]
    o_ref[0] = xn


def _outproj(ys, w_out, x, mod3, final_w, final):
    bsz = x.shape[0]
    row = lambda b, t: (b, t, 0)
    yspec = pl.BlockSpec((1, ROW_TILE, GROUP_W), row)
    return pl.pallas_call(
        functools.partial(_outproj_kernel, final=final),
        grid=(bsz, SEQ // ROW_TILE),
        in_specs=[yspec, yspec, yspec, yspec,
                  pl.BlockSpec((D_MODEL, D_MODEL), lambda b, t: (0, 0)),
                  pl.BlockSpec((1, ROW_TILE, D_MODEL), row),
                  pl.BlockSpec((1, 1, 3 * D_MODEL), lambda b, t: (b, 0, 0)),
                  pl.BlockSpec((1, D_MODEL), lambda b, t: (0, 0))],
        out_specs=pl.BlockSpec((1, ROW_TILE, D_MODEL), row),
        out_shape=jax.ShapeDtypeStruct((bsz, SEQ, D_MODEL), F32),
        compiler_params=_params(2),
        name="outproj_final" if final else "outproj",
    )(*ys, w_out, x, mod3, final_w)


def _build_toeplitz(tab_ref, slopes, patterns):
    r_io = lax.broadcasted_iota(jnp.int32, (TQ, LANES), 0)
    c_io = lax.broadcasted_iota(jnp.int32, (TQ, LANES), 1)
    base = c_io - r_io - TOEP_OFF

    def body(j, _):
        d = base + j * LANES
        ad = jnp.abs(d)
        adf = ad.astype(F32)
        if patterns is None:
            logm = jnp.zeros((TQ, LANES), F32)
            valid = None
        else:
            mult = jnp.zeros((TQ, LANES), F32)
            for w, r in patterns:
                reach = r * (w // (2 * r))
                hit = jnp.where(ad <= reach, jnp.where((d & (r - 1)) == 0, 1.0, 0.0), 0.0)
                mult = mult + hit
            valid = mult > 0.5
            logm = jnp.log(jnp.maximum(mult, 1.0))
        for h, slope in enumerate(slopes):
            val = (logm - slope * adf) * LOG2E
            if valid is not None:
                val = jnp.where(valid, val, NEG)
            tab_ref[h, j] = val
        return 0

    lax.fori_loop(0, NJ, body, 0)


def _mask_values(v_ref, vm_ref):
    v = v_ref[0]
    for h in range(N_HEADS):
        vm_ref[h * SEQ:(h + 1) * SEQ, :] = jnp.where(_head_mask(h), v, jnp.zeros_like(v))


def _row_probs(s, tab_ref, h, tbase):
    ps, invs = [], []
    for r in range(0, TQ, SLAB):
        cols = [s[r:r + SLAB, c * LANES:(c + 1) * LANES] + tab_ref[h, tbase + c, r:r + SLAB, :]
                for c in range(SEQ // LANES)]
        e = jnp.concatenate(cols, axis=-1)
        m = jnp.max(e, axis=-1, keepdims=True)
        p = jnp.exp2(e - m)
        invs.append(1.0 / jnp.sum(p, axis=-1, keepdims=True))
        ps.append(p.astype(BF16))
    return jnp.concatenate(ps, axis=0), jnp.concatenate(invs, axis=0)


def _per_head_lanes(cols):
    out = jnp.broadcast_to(cols[N_HEADS - 1], (TQ, GROUP_W))
    for h in range(N_HEADS - 2, -1, -1):
        out = jnp.where(_head_mask(h), cols[h], out)
    return out


def _mixer_a_kernel(q_ref, k_ref, v_ref, g_ref, o_ref, tab_ref, vm_ref, *, slopes):
    @pl.when(pl.program_id(0) == 0)
    def _():
        _build_toeplitz(tab_ref, slopes, DILATED_PATTERNS)

    _mask_values(v_ref, vm_ref)

    def qstep(i, _):
        q0 = pl.multiple_of(i * TQ, TQ)
        tbase = (TOEP_OFF - q0) // LANES
        q = q_ref[0, pl.ds(q0, TQ), :]
        qs = jnp.concatenate([jnp.where(_head_mask(h), q, jnp.zeros_like(q)) for h in range(N_HEADS)], axis=0)
        s_all = _dot_nt(qs, k_ref[0])
        ps, invs = [], []
        for h in range(N_HEADS):
            p, inv = _row_probs(s_all[h * TQ:(h + 1) * TQ], tab_ref, h, tbase)
            ps.append(p)
            invs.append(inv)
        acc = _dot(jnp.concatenate(ps, axis=-1), vm_ref[...])
        g = g_ref[0, pl.ds(q0, TQ), :].astype(F32)
        o_ref[0, pl.ds(q0, TQ), :] = (acc * _per_head_lanes(invs) * _silu(g)).astype(BF16)
        return 0

    lax.fori_loop(0, SEQ // TQ, qstep, 0)


def _attn_specs():
    def col(j):
        return pl.BlockSpec((1, SEQ, GROUP_W), lambda b: (b, 0, j))
    return [col(0), col(1), col(2), col(3)]


_ATTN_SCRATCH = [pltpu.VMEM((N_HEADS, NJ, TQ, LANES), F32),
                 pltpu.VMEM((N_HEADS * SEQ, GROUP_W), BF16)]


def _mixer_a(pa, slopes):
    bsz = pa.shape[0]
    return pl.pallas_call(
        functools.partial(_mixer_a_kernel, slopes=slopes),
        grid=(bsz,),
        in_specs=_attn_specs(),
        out_specs=pl.BlockSpec((1, SEQ, GROUP_W), lambda b: (b, 0, 0)),
        out_shape=jax.ShapeDtypeStruct((bsz, SEQ, GROUP_W), BF16),
        scratch_shapes=_ATTN_SCRATCH,
        compiler_params=_params(1),
        name="mixer_a_dilated",
    )(pa, pa, pa, pa)


def _mixer_b_kernel(q_ref, k_ref, v_ref, g_ref, lam_ref, nw_ref, o_ref, tab_ref, vm_ref, *, slopes, lam_init):
    @pl.when(pl.program_id(0) == 0)
    def _():
        _build_toeplitz(tab_ref, slopes, None)

    _mask_values(v_ref, vm_ref)
    lv = lam_ref[...]
    lam = (jnp.exp(jnp.sum(lv[0:1] * lv[1:2], axis=-1, keepdims=True))
           - jnp.exp(jnp.sum(lv[2:3] * lv[3:4], axis=-1, keepdims=True)) + lam_init)
    nw = nw_ref[...]

    def qstep(i, _):
        q0 = pl.multiple_of(i * TQ, TQ)
        tbase = (TOEP_OFF - q0) // LANES
        q = q_ref[0, pl.ds(q0, TQ), :]
        halves = []
        for i2 in range(2):
            qs = jnp.concatenate(
                [jnp.where(_head_mask(2 * h + i2, DIFF_HEAD_DIM), q, jnp.zeros_like(q)) for h in range(N_HEADS)],
                axis=0)
            s_all = _dot_nt(qs, k_ref[0])
            ps, invs = [], []
            for h in range(N_HEADS):
                p, inv = _row_probs(s_all[h * TQ:(h + 1) * TQ], tab_ref, h, tbase)
                ps.append(p)
                invs.append(inv)
            halves.append(_dot(jnp.concatenate(ps, axis=-1), vm_ref[...]) * _per_head_lanes(invs))
        o = halves[0] - lam * halves[1]
        o2 = o * o
        ms = [jnp.sum(jnp.where(_head_mask(h), o2, 0.0), axis=-1, keepdims=True) * (1.0 / HEAD_DIM)
              for h in range(N_HEADS)]
        rs = _per_head_lanes([lax.rsqrt(m + EPS) for m in ms])
        on = o * rs * nw * (1.0 - lam_init)
        g = g_ref[0, pl.ds(q0, TQ), :].astype(F32)
        o_ref[0, pl.ds(q0, TQ), :] = (on * _silu(g)).astype(BF16)
        return 0

    lax.fori_loop(0, SEQ // TQ, qstep, 0)


def _mixer_b(pb, lam_p, nw256, slopes, lam_init):
    bsz = pb.shape[0]
    return pl.pallas_call(
        functools.partial(_mixer_b_kernel, slopes=slopes, lam_init=lam_init),
        grid=(bsz,),
        in_specs=_attn_specs() + [pl.BlockSpec((4, DIFF_HEAD_DIM), lambda b: (0, 0)),
                                  pl.BlockSpec((1, GROUP_W), lambda b: (0, 0))],
        out_specs=pl.BlockSpec((1, SEQ, GROUP_W), lambda b: (b, 0, 0)),
        out_shape=jax.ShapeDtypeStruct((bsz, SEQ, GROUP_W), BF16),
        scratch_shapes=_ATTN_SCRATCH,
        compiler_params=_params(1),
        name="mixer_b_diff",
    )(pb, pb, pb, pb, lam_p, nw256)


CONV_TILE = 256
CONV_HALO = SUBLANES
N_CHUNK = SEQ // SSM_CHUNK


def _cumsum_rows(a):
    row = lax.broadcasted_iota(jnp.int32, a.shape, 0)
    s = 1
    while s < a.shape[0]:
        a = a + jnp.where(row >= s, pltpu.roll(a, s, 0), 0.0)
        s *= 2
    return a


def _mixer_c_kernel(p_ref, dt_ref, cw_ref, cb_ref, alog_ref, dtb_ref, dskip_ref, nw_ref, exp_ref,
                    o_ref, xpad, xc, y_s, sb_s, cs_s, db_s):
    L = SSM_CHUNK
    zero_rows = jnp.zeros((CONV_HALO, SSM_XBC), F32)
    xpad[0:CONV_HALO, :] = zero_rows
    xpad[CONV_HALO + SEQ:CONV_HALO + SEQ + CONV_HALO, :] = zero_rows

    def fill(i, _):
        r0 = pl.multiple_of(i * CONV_TILE, CONV_TILE)
        xpad[pl.ds(CONV_HALO + r0, CONV_TILE), :] = p_ref[0, pl.ds(r0, CONV_TILE), GROUP_W:].astype(F32)
        return 0

    lax.fori_loop(0, SEQ // CONV_TILE, fill, 0)

    def conv(i, _):
        r0 = pl.multiple_of(i * CONV_TILE, CONV_TILE)
        win = xpad[pl.ds(r0, CONV_TILE + 2 * CONV_HALO), :]
        acc = jnp.zeros((CONV_TILE, SSM_XBC), F32) + cb_ref[...]
        for j in range(SSM_CONV):
            off = CONV_HALO - SSM_CONV // 2 + j
            acc = acc + cw_ref[j:j + 1, :] * win[off:off + CONV_TILE, :]
        xc[pl.ds(r0, CONV_TILE), :] = _silu(acc)
        return 0

    lax.fori_loop(0, SEQ // CONV_TILE, conv, 0)

    a_neg = -jnp.exp(alog_ref[...])
    expand = exp_ref[...]
    li = lax.broadcasted_iota(jnp.int32, (L, L), 0)
    si = lax.broadcasted_iota(jnp.int32, (L, L), 1)
    lower = si <= li
    upper = si >= li
    hmasks = [_head_mask(h) for h in range(N_HEADS)]

    def chunk_terms(t0):
        dtr = dt_ref[0, pl.ds(t0, L), :] + dtb_ref[...]
        dt = jnp.maximum(dtr, 0.0) + jnp.log(1.0 + jnp.exp(-jnp.abs(dtr)))
        a = dt * a_neg
        ainc = _cumsum_rows(a)
        aexc = ainc - a
        return dt, ainc, aexc

    def fwd(c, hf):
        t0 = pl.multiple_of(c * L, L)
        dt, ainc, aexc = chunk_terms(t0)
        dt_x = jnp.dot(dt, expand, precision=HIGHEST, preferred_element_type=F32)
        ainc_x = jnp.dot(ainc, expand, precision=HIGHEST, preferred_element_type=F32)
        aexc_x = jnp.dot(aexc, expand, precision=HIGHEST, preferred_element_type=F32)
        ainc_t = ainc.T
        aexc_t = aexc.T
        xs = xc[pl.ds(t0, L), 0:GROUP_W]
        bm = xc[pl.ds(t0, L), GROUP_W:2 * GROUP_W]
        cm = xc[pl.ds(t0, L), 2 * GROUP_W:3 * GROUP_W]
        xf = xs * dt_x[:, 0:GROUP_W]
        xb = xs * dt_x[:, GROUP_W:2 * GROUP_W]
        xcat = jnp.concatenate([xf, xb], axis=0).astype(BF16)
        tot_f = ainc_x[L - 1:L, 0:GROUP_W]
        tot_b = ainc_x[L - 1:L, GROUP_W:2 * GROUP_W]
        y = xs * dskip_ref[...]
        cbs = []
        for g in range(SSM_GROUPS):
            gs = slice(g * SSM_STATE, (g + 1) * SSM_STATE)
            cbs.append(_dot_nt(cm[:, gs].astype(BF16), bm[:, gs].astype(BF16)))
        for h in range(N_HEADS):
            cb = cbs[h // (N_HEADS // SSM_GROUPS)]
            col_f = ainc[:, h:h + 1]
            row_f = ainc_t[h:h + 1, :]
            col_b = aexc[:, N_HEADS + h:N_HEADS + h + 1]
            row_b = aexc_t[N_HEADS + h:N_HEADS + h + 1, :]
            lf = jnp.exp(jnp.where(lower, col_f - row_f, NEG))
            ub = jnp.exp(jnp.where(upper, row_b - col_b, NEG))
            mcat = jnp.concatenate([cb * lf, cb * ub], axis=1).astype(BF16)
            y = y + jnp.where(hmasks[h], _dot(mcat, xcat), 0.0)
        wf = (jnp.exp(tot_f - ainc_x[:, 0:GROUP_W]) * xf).astype(BF16)
        wb = (jnp.exp(aexc_x[:, GROUP_W:2 * GROUP_W]) * xb).astype(BF16)
        ef = jnp.exp(ainc_x[:, 0:GROUP_W])
        cs_s[pl.ds(t0, L), :] = jnp.exp(tot_b - aexc_x[:, GROUP_W:2 * GROUP_W])
        db_s[pl.ds(pl.multiple_of(c * SUBLANES, SUBLANES), SUBLANES), :] = jnp.broadcast_to(
            jnp.exp(tot_b), (SUBLANES, GROUP_W))
        dec_f = jnp.exp(tot_f)
        hf_new = []
        yoff = []
        for g in range(SSM_GROUPS):
            gs = slice(g * SSM_STATE, (g + 1) * SSM_STATE)
            bt = bm[:, gs].T.astype(BF16)
            yoff.append(_dot(cm[:, gs].astype(BF16), hf[g].astype(BF16)))
            hf_new.append(dec_f[:, gs] * hf[g] + _dot(bt, wf[:, gs]))
            sb_s[pl.ds(t0, L), gs] = _dot(bt, wb[:, gs])
        y = y + jnp.concatenate(yoff, axis=1) * ef
        y_s[pl.ds(t0, L), :] = y
        return tuple(hf_new)

    h0 = tuple(jnp.zeros((SSM_STATE, SSM_STATE), F32) for _ in range(SSM_GROUPS))
    lax.fori_loop(0, N_CHUNK, fwd, h0)

    def bwd(i, hb):
        c = N_CHUNK - 1 - i
        t0 = pl.multiple_of(c * L, L)
        cm = xc[pl.ds(t0, L), 2 * GROUP_W:3 * GROUP_W]
        dec_b = db_s[pl.ds(pl.multiple_of(c * SUBLANES, SUBLANES), 1), :]
        yoff = []
        hb_new = []
        for g in range(SSM_GROUPS):
            gs = slice(g * SSM_STATE, (g + 1) * SSM_STATE)
            yoff.append(_dot(cm[:, gs].astype(BF16), hb[g].astype(BF16)))
            hb_new.append(dec_b[:, gs] * hb[g] + sb_s[pl.ds(t0, L), gs])
        y_s[pl.ds(t0, L), :] = y_s[pl.ds(t0, L), :] + jnp.concatenate(yoff, axis=1) * cs_s[pl.ds(t0, L), :]
        return tuple(hb_new)

    lax.fori_loop(0, N_CHUNK, bwd, h0)

    def fin(i, _):
        r0 = pl.multiple_of(i * CONV_TILE, CONV_TILE)
        z = p_ref[0, pl.ds(r0, CONV_TILE), 0:GROUP_W].astype(F32)
        y = y_s[pl.ds(r0, CONV_TILE), :] * _silu(z)
        parts = []
        for g in range(SSM_GROUPS):
            yg = y[:, g * SSM_STATE:(g + 1) * SSM_STATE]
            parts.append(yg * lax.rsqrt(jnp.mean(yg * yg, axis=-1, keepdims=True) + EPS))
        o_ref[0, pl.ds(r0, CONV_TILE), :] = (jnp.concatenate(parts, axis=1) * nw_ref[...]).astype(BF16)
        return 0

    lax.fori_loop(0, SEQ // CONV_TILE, fin, 0)


def _head_expand_matrix():
    j = lax.broadcasted_iota(jnp.int32, (LANES, 2 * GROUP_W), 0)
    c = lax.broadcasted_iota(jnp.int32, (LANES, 2 * GROUP_W), 1)
    return (j == c // HEAD_DIM).astype(F32)


def _mixer_c(pc, pdt, conv_w, conv_b, a_log, dt_bias, d_skip, norm_w):
    bsz = pc.shape[0]
    pad8 = lambda v: jnp.pad(v.reshape(1, 2 * N_HEADS), ((0, 0), (0, LANES - 2 * N_HEADS)))
    small = lambda shape: pl.BlockSpec(shape, lambda b: (0,) * len(shape))
    return pl.pallas_call(
        _mixer_c_kernel,
        grid=(bsz,),
        in_specs=[pl.BlockSpec((1, SEQ, 1024), lambda b: (b, 0, 0)),
                  pl.BlockSpec((1, SEQ, DT_PAD), lambda b: (b, 0, 0)),
                  small((SSM_CONV, SSM_XBC)), small((1, SSM_XBC)),
                  small((1, LANES)), small((1, LANES)),
                  small((1, GROUP_W)), small((1, GROUP_W)),
                  small((LANES, 2 * GROUP_W))],
        out_specs=pl.BlockSpec((1, SEQ, GROUP_W), lambda b: (b, 0, 0)),
        out_shape=jax.ShapeDtypeStruct((bsz, SEQ, GROUP_W), BF16),
        scratch_shapes=[pltpu.VMEM((SEQ + 2 * CONV_HALO, SSM_XBC), F32),
                        pltpu.VMEM((SEQ, SSM_XBC), F32),
                        pltpu.VMEM((SEQ, GROUP_W), F32),
                        pltpu.VMEM((SEQ, GROUP_W), F32),
                        pltpu.VMEM((SEQ, GROUP_W), F32),
                        pltpu.VMEM((N_CHUNK * SUBLANES, GROUP_W), F32)],
        compiler_params=_params(1),
        name="mixer_c_ssd",
    )(pc, pdt, conv_w, conv_b.reshape(1, SSM_XBC), pad8(a_log), pad8(dt_bias),
      jnp.repeat(d_skip, HEAD_DIM).reshape(1, GROUP_W), norm_w.reshape(1, GROUP_W),
      _head_expand_matrix())


NA_ROWS = SEQ // GRID_W
NA_DR = 2 * NA_WIN_H - 1
NA_DC = 2 * NA_WIN_W - 1
NA_PAIRS = NA_DR - 1


def _na_bias_kernel(rpb_ref, o_ref):
    lh = pl.program_id(0)
    cq = lax.broadcasted_iota(jnp.int32, (GRID_W, LANES), 0)
    lane = lax.broadcasted_iota(jnp.int32, (GRID_W, LANES), 1)
    second = lane >= GRID_W
    ck = jnp.where(second, lane - GRID_W, lane)
    cs = jnp.clip(cq - NA_WIN_W // 2, 0, GRID_W - NA_WIN_W)
    inside = (ck >= cs) & (ck < cs + NA_WIN_W)
    dc = ck - cq + NA_WIN_W - 1
    base = lh * (NA_DR * NA_DC)
    for p in range(NA_PAIRS):
        acc = jnp.zeros((GRID_W, LANES), F32)
        for j in range(NA_DC):
            lo = rpb_ref[base + p * NA_DC + j]
            hi = rpb_ref[base + (p + 1) * NA_DC + j]
            acc = acc + jnp.where(dc == j, jnp.where(second, hi, lo), 0.0)
        o_ref[0, p] = jnp.where(inside, acc, NEG)


def _na_bias(na_rpb):
    n = DEPTH * N_HEADS
    return pl.pallas_call(
        _na_bias_kernel,
        grid=(n,),
        in_specs=[pl.BlockSpec(memory_space=pltpu.SMEM)],
        out_specs=pl.BlockSpec((1, NA_PAIRS, GRID_W, LANES), lambda i: (i, 0, 0, 0)),
        out_shape=jax.ShapeDtypeStruct((n, NA_PAIRS, GRID_W, LANES), F32),
        compiler_params=_params(1),
        name="na_bias_table",
    )(na_rpb.reshape(-1))


def _mixer_d_kernel(q_ref, k_ref, v_ref, g_ref, bias_ref, o_ref):
    nkeys = NA_WIN_H * GRID_W

    def rstep(r, _):
        rs = jnp.clip(r - NA_WIN_H // 2, 0, NA_ROWS - NA_WIN_H)
        delta = r - rs
        q0 = pl.multiple_of(r * GRID_W, GRID_W)
        k0 = pl.multiple_of(rs * GRID_W, GRID_W)
        q = q_ref[0, pl.ds(q0, GRID_W), :]
        kw = k_ref[0, pl.ds(k0, nkeys), :]
        vw = v_ref[0, pl.ds(k0, nkeys), :]
        out = jnp.zeros((GRID_W, GROUP_W), F32)
        for h in range(N_HEADS):
            hm = _head_mask(h)
            qm = jnp.where(hm, q, jnp.zeros_like(q))
            s = _dot_nt(qm, kw)
            bias = jnp.concatenate(
                [bias_ref[h, 2 * i - delta + NA_WIN_H - 1] for i in range(NA_WIN_H // 2)], axis=-1)
            e = s + bias
            m = jnp.max(e, axis=-1, keepdims=True)
            p = jnp.exp(e - m)
            l = jnp.sum(p, axis=-1, keepdims=True)
            out = jnp.where(hm, _dot(p.astype(BF16), vw) / l, out)
        g = g_ref[0, pl.ds(q0, GRID_W), :].astype(F32)
        o_ref[0, pl.ds(q0, GRID_W), :] = (out * _silu(g)).astype(BF16)
        return 0

    lax.fori_loop(0, NA_ROWS, rstep, 0)


def _mixer_d(pd, bias_l):
    bsz = pd.shape[0]
    return pl.pallas_call(
        _mixer_d_kernel,
        grid=(bsz,),
        in_specs=_attn_specs() + [pl.BlockSpec((N_HEADS, NA_PAIRS, GRID_W, LANES), lambda b: (0, 0, 0, 0))],
        out_specs=pl.BlockSpec((1, SEQ, GROUP_W), lambda b: (b, 0, 0)),
        out_shape=jax.ShapeDtypeStruct((bsz, SEQ, GROUP_W), BF16),
        compiler_params=_params(1),
        name="mixer_d_neighbourhood",
    )(pd, pd, pd, pd, bias_l)


def _prep_w_in(w):
    scale = jnp.ones((D_IN,), F32)
    scale = scale.at[0:GROUP_W].set(HEAD_DIM ** -0.5 * LOG2E)
    scale = scale.at[4 * GROUP_W:5 * GROUP_W].set(DIFF_HEAD_DIM ** -0.5 * LOG2E)
    d0 = 9 * GROUP_W + SSM_XBC + 2 * N_HEADS
    scale = scale.at[d0:d0 + GROUP_W].set(HEAD_DIM ** -0.5)
    w = w * scale[None, :]
    dt0 = 9 * GROUP_W + SSM_XBC
    w_r = jnp.concatenate([w[:, :dt0], w[:, d0:], w[:, dt0:d0],
                           jnp.zeros((D_MODEL, DT_PAD - 2 * N_HEADS), F32)], axis=1)
    return w_r.astype(BF16)


def kernel(x, c, norm_w, ada_w, ada_b, w_in, diff_lambda, diff_norm_w, conv_w, conv_b, ssm_a_log,
           ssm_dt_bias, ssm_d, ssm_norm_w, na_rpb, w_out, final_norm_w):
    bsz = x.shape[0]
    n = 2 * N_HEADS
    slopes = [2.0 ** (-8.0 * i / n) for i in range(1, n + 1)]
    slopes_a, slopes_b = tuple(slopes[0::2]), tuple(slopes[1::2])
    mod = _adaln(c, ada_w, ada_b)
    na_bias = _na_bias(na_rpb)
    final_w = final_norm_w.reshape(1, D_MODEL)
    for l in range(DEPTH):
        mod3 = mod[l].reshape(bsz, 1, 3 * D_MODEL)
        pa, pb, pc, pd, pdt = _inproj(x, mod3, norm_w[l].reshape(1, D_MODEL), _prep_w_in(w_in[l]))
        lam_init = 0.8 - 0.6 * math.exp(-0.3 * l)
        ya = _mixer_a(pa, slopes_a)
        yb = _mixer_b(pb, diff_lambda[l], jnp.tile(diff_norm_w[l], N_HEADS).reshape(1, GROUP_W),
                      slopes_b, lam_init)
        yc = _mixer_c(pc, pdt, conv_w[l], conv_b[l], ssm_a_log[l], ssm_dt_bias[l], ssm_d[l], ssm_norm_w[l])
        yd = _mixer_d(pd, na_bias[l * N_HEADS:(l + 1) * N_HEADS])
        x = _outproj((ya, yb, yc, yd), w_out[l].astype(BF16), x, mod3, final_w, final=(l == DEPTH - 1))
    return x
```

```python
import functools
import math

import jax
import jax.numpy as jnp
from jax import lax
from jax.experimental import pallas as pl
from jax.experimental.pallas import tpu as pltpu

D_MODEL = 1024
SEQ = 2048
DEPTH = 2
HEAD_DIM = 64
GROUP_W = 256
N_HEADS = 4
EPS = 1e-6
DILATED_PATTERNS = ((128, 1), (512, 4), (2048, 16))
DIFF_HEAD_DIM = 32
SSM_GROUPS = 2
SSM_STATE = 128
SSM_CONV = 5
SSM_CHUNK = 128
SSM_XBC = 768
GRID_W = 64
NA_WIN_H = 8
NA_WIN_W = 16
D_IN = 13 * GROUP_W + SSM_XBC + 2 * N_HEADS

LANES = 128
SUBLANES = 8
VMEM_LIMIT = 56 * 1024 * 1024

NEG = -1e30
LOG2E = math.log2(math.e)
F32 = jnp.float32
BF16 = jnp.bfloat16
HIGHEST = lax.Precision.HIGHEST

TQ = 128
SLAB = 16
ROW_TILE = 512
DT_PAD = LANES
NJ = (2 * SEQ - TQ) // LANES
TOEP_OFF = SEQ - TQ


def _silu(x):
    return x / (1.0 + jnp.exp(-x))


def _dot_nt(a, b):
    return lax.dot_general(a, b, (((1,), (1,)), ((), ())), preferred_element_type=F32)


def _dot(a, b):
    return jnp.dot(a, b, preferred_element_type=F32)


def _params(n_grid):
    return pltpu.CompilerParams(dimension_semantics=("arbitrary",) * n_grid,
                                vmem_limit_bytes=VMEM_LIMIT)


def _head_mask(h, width=HEAD_DIM, total=GROUP_W):
    lane = lax.broadcasted_iota(jnp.int32, (1, total), 1)
    return (lane >= h * width) & (lane < (h + 1) * width)


def _mod_kernel(c_ref, w_ref, b_ref, o_ref):
    c = c_ref[...]
    o_ref[0] = jnp.dot(_silu(c), w_ref[0], precision=HIGHEST,
                       preferred_element_type=F32) + b_ref[0]


def _adaln(c, ada_w, ada_b):
    bsz = c.shape[0]
    tn = 768
    return pl.pallas_call(
        _mod_kernel,
        grid=(DEPTH, 3 * D_MODEL // tn),
        in_specs=[pl.BlockSpec((bsz, D_MODEL), lambda l, j: (0, 0)),
                  pl.BlockSpec((1, D_MODEL, tn), lambda l, j: (l, 0, j)),
                  pl.BlockSpec((1, 1, tn), lambda l, j: (l, 0, j))],
        out_specs=pl.BlockSpec((1, bsz, tn), lambda l, j: (l, 0, j)),
        out_shape=jax.ShapeDtypeStruct((DEPTH, bsz, 3 * D_MODEL), F32),
        compiler_params=_params(2),
        name="adaln_mod",
    )(c, ada_w, ada_b.reshape(DEPTH, 1, 3 * D_MODEL))


def _inproj_kernel(x_ref, mod_ref, nw_ref, w_ref, pa_ref, pb_ref, pc_ref, pd_ref, pdt_ref):
    x = x_ref[0]
    shift = mod_ref[0, :, 0:D_MODEL]
    scale = mod_ref[0, :, D_MODEL:2 * D_MODEL]
    y = x * lax.rsqrt(jnp.mean(x * x, axis=-1, keepdims=True) + EPS) * nw_ref[...]
    h = (y * (1.0 + scale) + shift).astype(BF16)
    for i, ref in enumerate((pa_ref, pb_ref, pc_ref, pd_ref)):
        ref[0] = _dot(h, w_ref[:, i * 1024:(i + 1) * 1024]).astype(BF16)
    pdt_ref[0] = _dot(h, w_ref[:, 4096:4096 + DT_PAD])


def _inproj(x, mod3, norm_w, w_r):
    bsz = x.shape[0]
    nw = w_r.shape[1]
    row = lambda b, t: (b, t, 0)
    big = pl.BlockSpec((1, ROW_TILE, 1024), row)
    return pl.pallas_call(
        _inproj_kernel,
        grid=(bsz, SEQ // ROW_TILE),
        in_specs=[pl.BlockSpec((1, ROW_TILE, D_MODEL), row),
                  pl.BlockSpec((1, 1, 3 * D_MODEL), lambda b, t: (b, 0, 0)),
                  pl.BlockSpec((1, D_MODEL), lambda b, t: (0, 0)),
                  pl.BlockSpec((D_MODEL, nw), lambda b, t: (0, 0))],
        out_specs=[big, big, big, big, pl.BlockSpec((1, ROW_TILE, DT_PAD), row)],
        out_shape=[jax.ShapeDtypeStruct((bsz, SEQ, 1024), BF16)] * 4
        + [jax.ShapeDtypeStruct((bsz, SEQ, DT_PAD), F32)],
        compiler_params=_params(2),
        name="inproj",
    )(x, mod3, norm_w, w_r)


def _outproj_kernel(ya_ref, yb_ref, yc_ref, yd_ref, w_ref, x_ref, mod_ref, fw_ref, o_ref, *, final):
    acc = None
    for i, ref in enumerate((ya_ref, yb_ref, yc_ref, yd_ref)):
        part = _dot(ref[0], w_ref[i * GROUP_W:(i + 1) * GROUP_W, :])
        acc = part if acc is None else acc + part
    gate = mod_ref[0, :, 2 * D_MODEL:3 * D_MODEL]
    xn = x_ref[0] + gate * acc
    if final:
        xn = xn * lax.rsqrt(jnp.mean(xn * xn, axis=-1, keepdims=True) + EPS) * fw_ref[...]
    o_ref[0] = xn


def _outproj(ys, w_out, x, mod3, final_w, final):
    bsz = x.shape[0]
    row = lambda b, t: (b, t, 0)
    yspec = pl.BlockSpec((1, ROW_TILE, GROUP_W), row)
    return pl.pallas_call(
        functools.partial(_outproj_kernel, final=final),
        grid=(bsz, SEQ // ROW_TILE),
        in_specs=[yspec, yspec, yspec, yspec,
                  pl.BlockSpec((D_MODEL, D_MODEL), lambda b, t: (0, 0)),
                  pl.BlockSpec((1, ROW_TILE, D_MODEL), row),
                  pl.BlockSpec((1, 1, 3 * D_MODEL), lambda b, t: (b, 0, 0)),
                  pl.BlockSpec((1, D_MODEL), lambda b, t: (0, 0))],
        out_specs=pl.BlockSpec((1, ROW_TILE, D_MODEL), row),
        out_shape=jax.ShapeDtypeStruct((bsz, SEQ, D_MODEL), F32),
        compiler_params=_params(2),
        name="outproj_final" if final else "outproj",
    )(*ys, w_out, x, mod3, final_w)


def _build_toeplitz(tab_ref, slopes, patterns):
    r_io = lax.broadcasted_iota(jnp.int32, (TQ, LANES), 0)
    c_io = lax.broadcasted_iota(jnp.int32, (TQ, LANES), 1)
    base = c_io - r_io - TOEP_OFF

    def body(j, _):
        d = base + j * LANES
        ad = jnp.abs(d)
        adf = ad.astype(F32)
        if patterns is None:
            logm = jnp.zeros((TQ, LANES), F32)
            valid = None
        else:
            mult = jnp.zeros((TQ, LANES), F32)
            for w, r in patterns:
                reach = r * (w // (2 * r))
                hit = jnp.where(ad <= reach, jnp.where((d & (r - 1)) == 0, 1.0, 0.0), 0.0)
                mult = mult + hit
            valid = mult > 0.5
            logm = jnp.log(jnp.maximum(mult, 1.0))
        for h, slope in enumerate(slopes):
            val = (logm - slope * adf) * LOG2E
            if valid is not None:
                val = jnp.where(valid, val, NEG)
            tab_ref[h, j] = val
        return 0

    lax.fori_loop(0, NJ, body, 0)


def _mask_values(v_ref, vm_ref):
    v = v_ref[0]
    for h in range(N_HEADS):
        vm_ref[h * SEQ:(h + 1) * SEQ, :] = jnp.where(_head_mask(h), v, jnp.zeros_like(v))


def _row_probs(s, tab_ref, h, tbase):
    ps, invs = [], []
    for r in range(0, TQ, SLAB):
        cols = [s[r:r + SLAB, c * LANES:(c + 1) * LANES] + tab_ref[h, tbase + c, r:r + SLAB, :]
                for c in range(SEQ // LANES)]
        e = jnp.concatenate(cols, axis=-1)
        m = jnp.max(e, axis=-1, keepdims=True)
        p = jnp.exp2(e - m)
        invs.append(1.0 / jnp.sum(p, axis=-1, keepdims=True))
        ps.append(p.astype(BF16))
    return jnp.concatenate(ps, axis=0), jnp.concatenate(invs, axis=0)


def _per_head_lanes(cols):
    out = jnp.broadcast_to(cols[N_HEADS - 1], (TQ, GROUP_W))
    for h in range(N_HEADS - 2, -1, -1):
        out = jnp.where(_head_mask(h), cols[h], out)
    return out


def _mixer_a_kernel(q_ref, k_ref, v_ref, g_ref, o_ref, tab_ref, vm_ref, *, slopes):
    @pl.when(pl.program_id(0) == 0)
    def _():
        _build_toeplitz(tab_ref, slopes, DILATED_PATTERNS)

    _mask_values(v_ref, vm_ref)

    def qstep(i, _):
        q0 = pl.multiple_of(i * TQ, TQ)
        tbase = (TOEP_OFF - q0) // LANES
        q = q_ref[0, pl.ds(q0, TQ), :]
        qs = jnp.concatenate([jnp.where(_head_mask(h), q, jnp.zeros_like(q)) for h in range(N_HEADS)], axis=0)
        s_all = _dot_nt(qs, k_ref[0])
        ps, invs = [], []
        for h in range(N_HEADS):
            p, inv = _row_probs(s_all[h * TQ:(h + 1) * TQ], tab_ref, h, tbase)
            ps.append(p)
            invs.append(inv)
        acc = _dot(jnp.concatenate(ps, axis=-1), vm_ref[...])
        g = g_ref[0, pl.ds(q0, TQ), :].astype(F32)
        o_ref[0, pl.ds(q0, TQ), :] = (acc * _per_head_lanes(invs) * _silu(g)).astype(BF16)
        return 0

    lax.fori_loop(0, SEQ // TQ, qstep, 0)


def _attn_specs():
    def col(j):
        return pl.BlockSpec((1, SEQ, GROUP_W), lambda b: (b, 0, j))
    return [col(0), col(1), col(2), col(3)]


_ATTN_SCRATCH = [pltpu.VMEM((N_HEADS, NJ, TQ, LANES), F32),
                 pltpu.VMEM((N_HEADS * SEQ, GROUP_W), BF16)]


def _mixer_a(pa, slopes):
    bsz = pa.shape[0]
    return pl.pallas_call(
        functools.partial(_mixer_a_kernel, slopes=slopes),
        grid=(bsz,),
        in_specs=_attn_specs(),
        out_specs=pl.BlockSpec((1, SEQ, GROUP_W), lambda b: (b, 0, 0)),
        out_shape=jax.ShapeDtypeStruct((bsz, SEQ, GROUP_W), BF16),
        scratch_shapes=_ATTN_SCRATCH,
        compiler_params=_params(1),
        name="mixer_a_dilated",
    )(pa, pa, pa, pa)


def _mixer_b_kernel(q_ref, k_ref, v_ref, g_ref, lam_ref, nw_ref, o_ref, tab_ref, vm_ref, *, slopes, lam_init):
    @pl.when(pl.program_id(0) == 0)
    def _():
        _build_toeplitz(tab_ref, slopes, None)

    _mask_values(v_ref, vm_ref)
    lv = lam_ref[...]
    lam = (jnp.exp(jnp.sum(lv[0:1] * lv[1:2], axis=-1, keepdims=True))
           - jnp.exp(jnp.sum(lv[2:3] * lv[3:4], axis=-1, keepdims=True)) + lam_init)
    nw = nw_ref[...]

    def qstep(i, _):
        q0 = pl.multiple_of(i * TQ, TQ)
        tbase = (TOEP_OFF - q0) // LANES
        q = q_ref[0, pl.ds(q0, TQ), :]
        halves = []
        for i2 in range(2):
            qs = jnp.concatenate(
                [jnp.where(_head_mask(2 * h + i2, DIFF_HEAD_DIM), q, jnp.zeros_like(q)) for h in range(N_HEADS)],
                axis=0)
            s_all = _dot_nt(qs, k_ref[0])
            ps, invs = [], []
            for h in range(N_HEADS):
                p, inv = _row_probs(s_all[h * TQ:(h + 1) * TQ], tab_ref, h, tbase)
                ps.append(p)
                invs.append(inv)
            halves.append(_dot(jnp.concatenate(ps, axis=-1), vm_ref[...]) * _per_head_lanes(invs))
        o = halves[0] - lam * halves[1]
        o2 = o * o
        ms = [jnp.sum(jnp.where(_head_mask(h), o2, 0.0), axis=-1, keepdims=True) * (1.0 / HEAD_DIM)
              for h in range(N_HEADS)]
        rs = _per_head_lanes([lax.rsqrt(m + EPS) for m in ms])
        on = o * rs * nw * (1.0 - lam_init)
        g = g_ref[0, pl.ds(q0, TQ), :].astype(F32)
        o_ref[0, pl.ds(q0, TQ), :] = (on * _silu(g)).astype(BF16)
        return 0

    lax.fori_loop(0, SEQ // TQ, qstep, 0)


def _mixer_b(pb, lam_p, nw256, slopes, lam_init):
    bsz = pb.shape[0]
    return pl.pallas_call(
        functools.partial(_mixer_b_kernel, slopes=slopes, lam_init=lam_init),
        grid=(bsz,),
        in_specs=_attn_specs() + [pl.BlockSpec((4, DIFF_HEAD_DIM), lambda b: (0, 0)),
                                  pl.BlockSpec((1, GROUP_W), lambda b: (0, 0))],
        out_specs=pl.BlockSpec((1, SEQ, GROUP_W), lambda b: (b, 0, 0)),
        out_shape=jax.ShapeDtypeStruct((bsz, SEQ, GROUP_W), BF16),
        scratch_shapes=_ATTN_SCRATCH,
        compiler_params=_params(1),
        name="mixer_b_diff",
    )(pb, pb, pb, pb, lam_p, nw256)


CONV_TILE = 256
CONV_HALO = SUBLANES
N_CHUNK = SEQ // SSM_CHUNK


def _expand_heads(x, expand3):
    hi = x.astype(BF16)
    r1 = x - hi.astype(F32)
    mid = r1.astype(BF16)
    lo = (r1 - mid.astype(F32)).astype(BF16)
    return _dot(jnp.concatenate([hi, mid, lo], axis=1), expand3)


def _cumsum_rows(a):
    row = lax.broadcasted_iota(jnp.int32, a.shape, 0)
    s = 1
    while s < a.shape[0]:
        a = a + jnp.where(row >= s, pltpu.roll(a, s, 0), 0.0)
        s *= 2
    return a


def _mixer_c_kernel(p_ref, dt_ref, cw_ref, cb_ref, alog_ref, alogx_ref, dtb_ref, dskip_ref, nw_ref, exp_ref,
                    o_ref, xpad, xc, y_s, sb_s, cs_s, db_s):
    L = SSM_CHUNK
    zero_rows = jnp.zeros((CONV_HALO, SSM_XBC), F32)
    xpad[0:CONV_HALO, :] = zero_rows
    xpad[CONV_HALO + SEQ:CONV_HALO + SEQ + CONV_HALO, :] = zero_rows

    def fill(i, _):
        r0 = pl.multiple_of(i * CONV_TILE, CONV_TILE)
        xpad[pl.ds(CONV_HALO + r0, CONV_TILE), :] = p_ref[0, pl.ds(r0, CONV_TILE), GROUP_W:].astype(F32)
        return 0

    lax.fori_loop(0, SEQ // CONV_TILE, fill, 0)

    def conv(i, _):
        r0 = pl.multiple_of(i * CONV_TILE, CONV_TILE)
        rows = CONV_TILE + 2 * CONV_HALO
        win = xpad[pl.ds(r0, rows), :]
        acc = jnp.zeros((CONV_TILE, SSM_XBC), F32) + cb_ref[...]
        for j in range(SSM_CONV):
            back = (SSM_CONV // 2 - j) % rows
            tap = win if back == 0 else pltpu.roll(win, back, 0)
            acc = acc + cw_ref[j:j + 1, :] * tap[CONV_HALO:CONV_HALO + CONV_TILE, :]
        xc[pl.ds(r0, CONV_TILE), :] = _silu(acc)
        return 0

    lax.fori_loop(0, SEQ // CONV_TILE, conv, 0)

    a_neg = -jnp.exp(alog_ref[...])
    a_neg_x = -jnp.exp(alogx_ref[...])
    expand = exp_ref[...]
    li = lax.broadcasted_iota(jnp.int32, (L, L), 0)
    si = lax.broadcasted_iota(jnp.int32, (L, L), 1)
    lower = si <= li
    upper = si >= li
    hmasks = [_head_mask(h) for h in range(N_HEADS)]

    def chunk_terms(t0):
        dtr = dt_ref[0, pl.ds(t0, L), :] + dtb_ref[...]
        dt = jnp.maximum(dtr, 0.0) + jnp.log(1.0 + jnp.exp(-jnp.abs(dtr)))
        a = dt * a_neg
        ainc = _cumsum_rows(a)
        aexc = ainc - a
        return dt, ainc, aexc

    def fwd(c, hf):
        t0 = pl.multiple_of(c * L, L)
        dt, ainc, aexc = chunk_terms(t0)
        both_x = _expand_heads(jnp.concatenate([dt, ainc], axis=0), expand)
        dt_x = both_x[0:L]
        ainc_x = both_x[L:2 * L]
        aexc_x = ainc_x - dt_x * a_neg_x
        ainc_t = ainc.T
        aexc_t = aexc.T
        xs = xc[pl.ds(t0, L), 0:GROUP_W]
        bm = xc[pl.ds(t0, L), GROUP_W:2 * GROUP_W]
        cm = xc[pl.ds(t0, L), 2 * GROUP_W:3 * GROUP_W]
        xf = xs * dt_x[:, 0:GROUP_W]
        xb = xs * dt_x[:, GROUP_W:2 * GROUP_W]
        xcat = jnp.concatenate([xf, xb], axis=0).astype(BF16)
        tot_f = ainc_x[L - 1:L, 0:GROUP_W]
        tot_b = ainc_x[L - 1:L, GROUP_W:2 * GROUP_W]
        y = xs * dskip_ref[...]
        cbs = []
        for g in range(SSM_GROUPS):
            gs = slice(g * SSM_STATE, (g + 1) * SSM_STATE)
            cbs.append(_dot_nt(cm[:, gs].astype(BF16), bm[:, gs].astype(BF16)))
        for h in range(N_HEADS):
            cb = cbs[h // (N_HEADS // SSM_GROUPS)]
            col_f = ainc[:, h:h + 1]
            row_f = ainc_t[h:h + 1, :]
            col_b = aexc[:, N_HEADS + h:N_HEADS + h + 1]
            row_b = aexc_t[N_HEADS + h:N_HEADS + h + 1, :]
            lf = jnp.exp(jnp.where(lower, col_f - row_f, NEG))
            ub = jnp.exp(jnp.where(upper, row_b - col_b, NEG))
            mcat = jnp.concatenate([cb * lf, cb * ub], axis=1).astype(BF16)
            y = y + jnp.where(hmasks[h], _dot(mcat, xcat), 0.0)
        wf = (jnp.exp(tot_f - ainc_x[:, 0:GROUP_W]) * xf).astype(BF16)
        wb = (jnp.exp(aexc_x[:, GROUP_W:2 * GROUP_W]) * xb).astype(BF16)
        ef = jnp.exp(ainc_x[:, 0:GROUP_W])
        cs_s[pl.ds(t0, L), :] = jnp.exp(tot_b - aexc_x[:, GROUP_W:2 * GROUP_W])
        db_s[pl.ds(pl.multiple_of(c * SUBLANES, SUBLANES), SUBLANES), :] = jnp.broadcast_to(
            jnp.exp(tot_b), (SUBLANES, GROUP_W))
        dec_f = jnp.exp(tot_f)
        hf_new = []
        yoff = []
        for g in range(SSM_GROUPS):
            gs = slice(g * SSM_STATE, (g + 1) * SSM_STATE)
            bt = bm[:, gs].T.astype(BF16)
            yoff.append(_dot(cm[:, gs].astype(BF16), hf[g].astype(BF16)))
            hf_new.append(dec_f[:, gs] * hf[g] + _dot(bt, wf[:, gs]))
            sb_s[pl.ds(t0, L), gs] = _dot(bt, wb[:, gs])
        y = y + jnp.concatenate(yoff, axis=1) * ef
        y_s[pl.ds(t0, L), :] = y
        return tuple(hf_new)

    h0 = tuple(jnp.zeros((SSM_STATE, SSM_STATE), F32) for _ in range(SSM_GROUPS))
    lax.fori_loop(0, N_CHUNK, fwd, h0)

    def bwd(i, hb):
        c = N_CHUNK - 1 - i
        t0 = pl.multiple_of(c * L, L)
        cm = xc[pl.ds(t0, L), 2 * GROUP_W:3 * GROUP_W]
        dec_b = db_s[pl.ds(pl.multiple_of(c * SUBLANES, SUBLANES), 1), :]
        yoff = []
        hb_new = []
        for g in range(SSM_GROUPS):
            gs = slice(g * SSM_STATE, (g + 1) * SSM_STATE)
            yoff.append(_dot(cm[:, gs].astype(BF16), hb[g].astype(BF16)))
            hb_new.append(dec_b[:, gs] * hb[g] + sb_s[pl.ds(t0, L), gs])
        y_s[pl.ds(t0, L), :] = y_s[pl.ds(t0, L), :] + jnp.concatenate(yoff, axis=1) * cs_s[pl.ds(t0, L), :]
        return tuple(hb_new)

    lax.fori_loop(0, N_CHUNK, bwd, h0)

    def fin(i, _):
        r0 = pl.multiple_of(i * CONV_TILE, CONV_TILE)
        z = p_ref[0, pl.ds(r0, CONV_TILE), 0:GROUP_W].astype(F32)
        y = y_s[pl.ds(r0, CONV_TILE), :] * _silu(z)
        parts = []
        for g in range(SSM_GROUPS):
            yg = y[:, g * SSM_STATE:(g + 1) * SSM_STATE]
            parts.append(yg * lax.rsqrt(jnp.mean(yg * yg, axis=-1, keepdims=True) + EPS))
        o_ref[0, pl.ds(r0, CONV_TILE), :] = (jnp.concatenate(parts, axis=1) * nw_ref[...]).astype(BF16)
        return 0

    lax.fori_loop(0, SEQ // CONV_TILE, fin, 0)


def _head_expand_matrix():
    j = lax.broadcasted_iota(jnp.int32, (LANES, 2 * GROUP_W), 0)
    c = lax.broadcasted_iota(jnp.int32, (LANES, 2 * GROUP_W), 1)
    return jnp.tile((j == c // HEAD_DIM).astype(BF16), (3, 1))


def _mixer_c(pc, pdt, conv_w, conv_b, a_log, dt_bias, d_skip, norm_w):
    bsz = pc.shape[0]
    pad8 = lambda v: jnp.pad(v.reshape(1, 2 * N_HEADS), ((0, 0), (0, LANES - 2 * N_HEADS)))
    small = lambda shape: pl.BlockSpec(shape, lambda b: (0,) * len(shape))
    return pl.pallas_call(
        _mixer_c_kernel,
        grid=(bsz,),
        in_specs=[pl.BlockSpec((1, SEQ, 1024), lambda b: (b, 0, 0)),
                  pl.BlockSpec((1, SEQ, DT_PAD), lambda b: (b, 0, 0)),
                  small((SSM_CONV, SSM_XBC)), small((1, SSM_XBC)),
                  small((1, LANES)), small((1, 2 * GROUP_W)), small((1, LANES)),
                  small((1, GROUP_W)), small((1, GROUP_W)),
                  small((3 * LANES, 2 * GROUP_W))],
        out_specs=pl.BlockSpec((1, SEQ, GROUP_W), lambda b: (b, 0, 0)),
        out_shape=jax.ShapeDtypeStruct((bsz, SEQ, GROUP_W), BF16),
        scratch_shapes=[pltpu.VMEM((SEQ + 2 * CONV_HALO, SSM_XBC), F32),
                        pltpu.VMEM((SEQ, SSM_XBC), F32),
                        pltpu.VMEM((SEQ, GROUP_W), F32),
                        pltpu.VMEM((SEQ, GROUP_W), F32),
                        pltpu.VMEM((SEQ, GROUP_W), F32),
                        pltpu.VMEM((N_CHUNK * SUBLANES, GROUP_W), F32)],
        compiler_params=_params(1),
        name="mixer_c_ssd",
    )(pc, pdt, conv_w, conv_b.reshape(1, SSM_XBC), pad8(a_log),
      jnp.repeat(a_log.reshape(-1), HEAD_DIM).reshape(1, 2 * GROUP_W), pad8(dt_bias),
      jnp.repeat(d_skip, HEAD_DIM).reshape(1, GROUP_W), norm_w.reshape(1, GROUP_W),
      _head_expand_matrix())


NA_ROWS = SEQ // GRID_W
NA_DR = 2 * NA_WIN_H - 1
NA_DC = 2 * NA_WIN_W - 1
NA_PAIRS = NA_DR - 1


def _na_bias_kernel(rpb_ref, o_ref):
    lh = pl.program_id(0)
    cq = lax.broadcasted_iota(jnp.int32, (GRID_W, LANES), 0)
    lane = lax.broadcasted_iota(jnp.int32, (GRID_W, LANES), 1)
    second = lane >= GRID_W
    ck = jnp.where(second, lane - GRID_W, lane)
    cs = jnp.clip(cq - NA_WIN_W // 2, 0, GRID_W - NA_WIN_W)
    inside = (ck >= cs) & (ck < cs + NA_WIN_W)
    dc = ck - cq + NA_WIN_W - 1
    base = lh * (NA_DR * NA_DC)
    for p in range(NA_PAIRS):
        acc = jnp.zeros((GRID_W, LANES), F32)
        for j in range(NA_DC):
            lo = rpb_ref[base + p * NA_DC + j]
            hi = rpb_ref[base + (p + 1) * NA_DC + j]
            acc = acc + jnp.where(dc == j, jnp.where(second, hi, lo), 0.0)
        o_ref[0, p] = jnp.where(inside, acc, NEG)


def _na_bias(na_rpb):
    n = DEPTH * N_HEADS
    return pl.pallas_call(
        _na_bias_kernel,
        grid=(n,),
        in_specs=[pl.BlockSpec(memory_space=pltpu.SMEM)],
        out_specs=pl.BlockSpec((1, NA_PAIRS, GRID_W, LANES), lambda i: (i, 0, 0, 0)),
        out_shape=jax.ShapeDtypeStruct((n, NA_PAIRS, GRID_W, LANES), F32),
        compiler_params=_params(1),
        name="na_bias_table",
    )(na_rpb.reshape(-1))


NA_UNROLL = 4


def _mixer_d_kernel(q_ref, k_ref, v_ref, g_ref, bias_ref, o_ref, vm_ref):
    nkeys = NA_WIN_H * GRID_W
    _mask_values(v_ref, vm_ref)

    def one_row(r):
        rs = jnp.clip(r - NA_WIN_H // 2, 0, NA_ROWS - NA_WIN_H)
        delta = r - rs
        q0 = pl.multiple_of(r * GRID_W, GRID_W)
        k0 = pl.multiple_of(rs * GRID_W, GRID_W)
        q = q_ref[0, pl.ds(q0, GRID_W), :]
        qs = jnp.concatenate([jnp.where(_head_mask(h), q, jnp.zeros_like(q)) for h in range(N_HEADS)], axis=0)
        s_all = _dot_nt(qs, k_ref[0, pl.ds(k0, nkeys), :])
        ps, invs = [], []
        for h in range(N_HEADS):
            bias = jnp.concatenate(
                [bias_ref[h, 2 * i - delta + NA_WIN_H - 1] for i in range(NA_WIN_H // 2)], axis=-1)
            e = s_all[h * GRID_W:(h + 1) * GRID_W] + bias
            m = jnp.max(e, axis=-1, keepdims=True)
            p = jnp.exp(e - m)
            invs.append(1.0 / jnp.sum(p, axis=-1, keepdims=True))
            ps.append(p.astype(BF16))
        vcat = jnp.concatenate([vm_ref[pl.ds(pl.multiple_of(h * SEQ + k0, GRID_W), nkeys), :]
                                for h in range(N_HEADS)], axis=0)
        acc = _dot(jnp.concatenate(ps, axis=-1), vcat)
        inv = jnp.broadcast_to(invs[N_HEADS - 1], (GRID_W, GROUP_W))
        for h in range(N_HEADS - 2, -1, -1):
            inv = jnp.where(_head_mask(h), invs[h], inv)
        g = g_ref[0, pl.ds(q0, GRID_W), :].astype(F32)
        o_ref[0, pl.ds(q0, GRID_W), :] = (acc * inv * _silu(g)).astype(BF16)

    def rstep(i, _):
        for u in range(NA_UNROLL):
            one_row(i * NA_UNROLL + u)
        return 0

    lax.fori_loop(0, NA_ROWS // NA_UNROLL, rstep, 0)


def _mixer_d(pd, bias_l):
    bsz = pd.shape[0]
    return pl.pallas_call(
        _mixer_d_kernel,
        grid=(bsz,),
        in_specs=_attn_specs() + [pl.BlockSpec((N_HEADS, NA_PAIRS, GRID_W, LANES), lambda b: (0, 0, 0, 0))],
        out_specs=pl.BlockSpec((1, SEQ, GROUP_W), lambda b: (b, 0, 0)),
        out_shape=jax.ShapeDtypeStruct((bsz, SEQ, GROUP_W), BF16),
        scratch_shapes=[pltpu.VMEM((N_HEADS * SEQ, GROUP_W), BF16)],
        compiler_params=_params(1),
        name="mixer_d_neighbourhood",
    )(pd, pd, pd, pd, bias_l)


def _prep_w_in(w):
    scale = jnp.ones((D_IN,), F32)
    scale = scale.at[0:GROUP_W].set(HEAD_DIM ** -0.5 * LOG2E)
    scale = scale.at[4 * GROUP_W:5 * GROUP_W].set(DIFF_HEAD_DIM ** -0.5 * LOG2E)
    d0 = 9 * GROUP_W + SSM_XBC + 2 * N_HEADS
    scale = scale.at[d0:d0 + GROUP_W].set(HEAD_DIM ** -0.5)
    w = w * scale[None, :]
    dt0 = 9 * GROUP_W + SSM_XBC
    w_r = jnp.concatenate([w[:, :dt0], w[:, d0:], w[:, dt0:d0],
                           jnp.zeros((D_MODEL, DT_PAD - 2 * N_HEADS), F32)], axis=1)
    return w_r.astype(BF16)


def kernel(x, c, norm_w, ada_w, ada_b, w_in, diff_lambda, diff_norm_w, conv_w, conv_b, ssm_a_log,
           ssm_dt_bias, ssm_d, ssm_norm_w, na_rpb, w_out, final_norm_w):
    bsz = x.shape[0]
    n = 2 * N_HEADS
    slopes = [2.0 ** (-8.0 * i / n) for i in range(1, n + 1)]
    slopes_a, slopes_b = tuple(slopes[0::2]), tuple(slopes[1::2])
    mod = _adaln(c, ada_w, ada_b)
    na_bias = _na_bias(na_rpb)
    final_w = final_norm_w.reshape(1, D_MODEL)
    for l in range(DEPTH):
        mod3 = mod[l].reshape(bsz, 1, 3 * D_MODEL)
        pa, pb, pc, pd, pdt = _inproj(x, mod3, norm_w[l].reshape(1, D_MODEL), _prep_w_in(w_in[l]))
        lam_init = 0.8 - 0.6 * math.exp(-0.3 * l)
        ya = _mixer_a(pa, slopes_a)
        yb = _mixer_b(pb, diff_lambda[l], jnp.tile(diff_norm_w[l], N_HEADS).reshape(1, GROUP_W),
                      slopes_b, lam_init)
        yc = _mixer_c(pc, pdt, conv_w[l], conv_b[l], ssm_a_log[l], ssm_dt_bias[l], ssm_d[l], ssm_norm_w[l])
        yd = _mixer_d(pd, na_bias[l * N_HEADS:(l + 1) * N_HEADS])
        x = _outproj((ya, yb, yc, yd), w_out[l].astype(BF16), x, mod3, final_w, final=(l == DEPTH - 1))
    return x
```

```python
import functools
import math

import jax
import jax.numpy as jnp
from jax import lax
from jax.experimental import pallas as pl
from jax.experimental.pallas import tpu as pltpu

D_MODEL = 1024
SEQ = 2048
DEPTH = 2
HEAD_DIM = 64
GROUP_W = 256
N_HEADS = 4
EPS = 1e-6
DILATED_PATTERNS = ((128, 1), (512, 4), (2048, 16))
DIFF_HEAD_DIM = 32
SSM_GROUPS = 2
SSM_STATE = 128
SSM_CONV = 5
SSM_CHUNK = 128
SSM_XBC = 768
GRID_W = 64
NA_WIN_H = 8
NA_WIN_W = 16
D_IN = 13 * GROUP_W + SSM_XBC + 2 * N_HEADS

LANES = 128
SUBLANES = 8
VMEM_LIMIT = 56 * 1024 * 1024

NEG = -1e30
LOG2E = math.log2(math.e)
F32 = jnp.float32
BF16 = jnp.bfloat16
HIGHEST = lax.Precision.HIGHEST

ROW_TILE = 512
DT_PAD = LANES


def _silu(x):
    return x / (1.0 + jnp.exp(-x))


def _dot_nt(a, b):
    return lax.dot_general(a, b, (((1,), (1,)), ((), ())), preferred_element_type=F32)


def _dot(a, b):
    return jnp.dot(a, b, preferred_element_type=F32)


def _params(n_grid):
    return pltpu.CompilerParams(dimension_semantics=("arbitrary",) * n_grid,
                                vmem_limit_bytes=VMEM_LIMIT)


def _head_mask(h, width=HEAD_DIM, total=GROUP_W):
    lane = lax.broadcasted_iota(jnp.int32, (1, total), 1)
    return (lane >= h * width) & (lane < (h + 1) * width)


def _mod_kernel(c_ref, w_ref, b_ref, o_ref):
    c = c_ref[...]
    o_ref[0] = jnp.dot(_silu(c), w_ref[0], precision=HIGHEST,
                       preferred_element_type=F32) + b_ref[0]


def _adaln(c, ada_w, ada_b):
    bsz = c.shape[0]
    tn = 768
    return pl.pallas_call(
        _mod_kernel,
        grid=(DEPTH, 3 * D_MODEL // tn),
        in_specs=[pl.BlockSpec((bsz, D_MODEL), lambda l, j: (0, 0)),
                  pl.BlockSpec((1, D_MODEL, tn), lambda l, j: (l, 0, j)),
                  pl.BlockSpec((1, 1, tn), lambda l, j: (l, 0, j))],
        out_specs=pl.BlockSpec((1, bsz, tn), lambda l, j: (l, 0, j)),
        out_shape=jax.ShapeDtypeStruct((DEPTH, bsz, 3 * D_MODEL), F32),
        compiler_params=_params(2),
        name="adaln_mod",
    )(c, ada_w, ada_b.reshape(DEPTH, 1, 3 * D_MODEL))


W_MAIN = 16 * GROUP_W
DT_COL0 = 9 * GROUP_W + SSM_XBC
PREP_TILE = 512


def _prep_kernel(w_ref, wd_ref, wdt_ref, sc_ref, scdt_ref, om_ref, odt_ref):
    j = pl.program_id(1)

    @pl.when(j < DT_COL0 // PREP_TILE)
    def _():
        om_ref[0] = (w_ref[0] * sc_ref[...]).astype(BF16)

    @pl.when(j >= DT_COL0 // PREP_TILE)
    def _():
        om_ref[0] = (wd_ref[0] * sc_ref[...]).astype(BF16)

    @pl.when(j == 0)
    def _():
        odt_ref[0] = (wdt_ref[0] * scdt_ref[...]).astype(BF16)


def _prep_w_in(w_in):
    scale = [1.0] * W_MAIN
    for c0, s in ((0, HEAD_DIM ** -0.5 * LOG2E), (4 * GROUP_W, DIFF_HEAD_DIM ** -0.5 * LOG2E),
                  (12 * GROUP_W, HEAD_DIM ** -0.5)):
        scale[c0:c0 + GROUP_W] = [s] * GROUP_W
    sc = jnp.asarray(scale, F32).reshape(1, W_MAIN)
    scdt = jnp.asarray([1.0] * (2 * N_HEADS) + [0.0] * (DT_PAD - 2 * N_HEADS), F32).reshape(1, DT_PAD)
    w_d = w_in[:, :, DT_COL0 + 2 * N_HEADS:]
    n_lo = DT_COL0 // PREP_TILE
    return pl.pallas_call(
        _prep_kernel,
        grid=(DEPTH, W_MAIN // PREP_TILE),
        in_specs=[pl.BlockSpec((1, D_MODEL, PREP_TILE), lambda l, j: (l, 0, jnp.minimum(j, n_lo - 1))),
                  pl.BlockSpec((1, D_MODEL, PREP_TILE), lambda l, j: (l, 0, jnp.maximum(j - n_lo, 0))),
                  pl.BlockSpec((1, D_MODEL, DT_PAD), lambda l, j: (l, 0, DT_COL0 // DT_PAD)),
                  pl.BlockSpec((1, PREP_TILE), lambda l, j: (0, j)),
                  pl.BlockSpec((1, DT_PAD), lambda l, j: (0, 0))],
        out_specs=[pl.BlockSpec((1, D_MODEL, PREP_TILE), lambda l, j: (l, 0, j)),
                   pl.BlockSpec((1, D_MODEL, DT_PAD), lambda l, j: (l, 0, 0))],
        out_shape=[jax.ShapeDtypeStruct((DEPTH, D_MODEL, W_MAIN), BF16),
                   jax.ShapeDtypeStruct((DEPTH, D_MODEL, DT_PAD), BF16)],
        compiler_params=_params(2),
        name="prep_w_in",
    )(w_in, w_d, w_in, sc, scdt)


def _inproj_kernel(x_ref, mod_ref, nw_ref, w_ref, wdt_ref, pa_ref, pb_ref, pc_ref, pd_ref, pdt_ref):
    x = x_ref[0]
    shift = mod_ref[:, 0:D_MODEL]
    scale = mod_ref[:, D_MODEL:2 * D_MODEL]
    y = x * lax.rsqrt(jnp.mean(x * x, axis=-1, keepdims=True) + EPS) * nw_ref[...]
    h = (y * (1.0 + scale) + shift).astype(BF16)
    for i, ref in enumerate((pa_ref, pb_ref, pc_ref, pd_ref)):
        ref[0] = _dot(h, w_ref[:, i * 1024:(i + 1) * 1024]).astype(BF16)
    pdt_ref[0] = _dot(h, wdt_ref[...])


def _inproj(x, mod4, norm_w, w_main, w_dt, l):
    bsz = x.shape[0]
    row = lambda b, t: (b, t, 0)
    big = pl.BlockSpec((1, ROW_TILE, 1024), row)
    return pl.pallas_call(
        _inproj_kernel,
        grid=(bsz, SEQ // ROW_TILE),
        in_specs=[pl.BlockSpec((1, ROW_TILE, D_MODEL), row),
                  pl.BlockSpec((None, None, 1, 3 * D_MODEL), lambda b, t: (l, b, 0, 0)),
                  pl.BlockSpec((None, 1, D_MODEL), lambda b, t: (l, 0, 0)),
                  pl.BlockSpec((None, D_MODEL, W_MAIN), lambda b, t: (l, 0, 0)),
                  pl.BlockSpec((None, D_MODEL, DT_PAD), lambda b, t: (l, 0, 0))],
        out_specs=[big, big, big, big, pl.BlockSpec((1, ROW_TILE, DT_PAD), row)],
        out_shape=[jax.ShapeDtypeStruct((bsz, SEQ, 1024), BF16)] * 4
        + [jax.ShapeDtypeStruct((bsz, SEQ, DT_PAD), F32)],
        compiler_params=_params(2),
        name="inproj",
    )(x, mod4, norm_w, w_main, w_dt)


def _outproj_kernel(ya_ref, yb_ref, yc_ref, yd_ref, w_ref, x_ref, mod_ref, fw_ref, o_ref, wb_ref, *, final):
    @pl.when((pl.program_id(0) == 0) & (pl.program_id(1) == 0))
    def _():
        wb_ref[...] = w_ref[...].astype(BF16)

    acc = None
    for i, ref in enumerate((ya_ref, yb_ref, yc_ref, yd_ref)):
        part = _dot(ref[0], wb_ref[i * GROUP_W:(i + 1) * GROUP_W, :])
        acc = part if acc is None else acc + part
    gate = mod_ref[:, 2 * D_MODEL:3 * D_MODEL]
    xn = x_ref[0] + gate * acc
    if final:
        xn = xn * lax.rsqrt(jnp.mean(xn * xn, axis=-1, keepdims=True) + EPS) * fw_ref[...]
    o_ref[0] = xn


def _outproj(ys, w_out, x, mod4, final_w, l):
    bsz = x.shape[0]
    final = l == DEPTH - 1
    row = lambda b, t: (b, t, 0)
    yspec = pl.BlockSpec((1, ROW_TILE, GROUP_W), row)
    return pl.pallas_call(
        functools.partial(_outproj_kernel, final=final),
        grid=(bsz, SEQ // ROW_TILE),
        in_specs=[yspec, yspec, yspec, yspec,
                  pl.BlockSpec((None, D_MODEL, D_MODEL), lambda b, t: (l, 0, 0)),
                  pl.BlockSpec((1, ROW_TILE, D_MODEL), row),
                  pl.BlockSpec((None, None, 1, 3 * D_MODEL), lambda b, t: (l, b, 0, 0)),
                  pl.BlockSpec((1, D_MODEL), lambda b, t: (0, 0))],
        out_specs=pl.BlockSpec((1, ROW_TILE, D_MODEL), row),
        out_shape=jax.ShapeDtypeStruct((bsz, SEQ, D_MODEL), F32),
        scratch_shapes=[pltpu.VMEM((D_MODEL, D_MODEL), BF16)],
        compiler_params=_params(2),
        name="outproj_final" if final else "outproj",
    )(*ys, w_out, x, mod4, final_w)


KEY_CHUNK = 512
WIDTH = 1024
PAIR_LANES = WIDTH // N_HEADS
VT_ROWS = HEAD_DIM + 16


def _build_toeplitz_t(tab_ref, slopes, patterns, off, unit=1):
    _, rows, tq = tab_ref.shape
    step = min(rows, 512)
    i_io = lax.broadcasted_iota(jnp.int32, (step, tq), 0)
    r_io = lax.broadcasted_iota(jnp.int32, (step, tq), 1)
    base = (i_io - r_io - off) * unit

    def write(i0, n):
        d = base[:n] + i0 * unit
        ad = jnp.abs(d)
        adf = ad.astype(F32)
        logm, valid = 0.0, None
        if patterns is not None:
            mult = jnp.zeros(d.shape, F32)
            for w, r in patterns:
                reach = r * (w // (2 * r))
                mult = mult + jnp.where(ad <= reach, jnp.where((d & (r - 1)) == 0, 1.0, 0.0), 0.0)
            valid = mult > 0.5
            logm = jnp.log(jnp.maximum(mult, 1.0))
        for h, slope in enumerate(slopes):
            val = (logm - slope * adf) * LOG2E
            if valid is not None:
                val = jnp.where(valid, val, NEG)
            tab_ref[h, pl.ds(i0, n), :] = val

    def body(j, _):
        write(pl.multiple_of(j * step, step), step)
        return 0

    lax.fori_loop(0, rows // step, body, 0)
    if rows % step:
        write(rows - rows % step, rows % step)


def _sublane_groups(x):
    return x.reshape(x.shape[0] // SUBLANES, SUBLANES, x.shape[1])


def _attention_t(q_ref, k_ref, v_ref, vt_ref, e_refs, qt_ref, acc_ref, *, n_pairs, key_chunk, win_chunks,
                 key_start, bias, finish):
    tq = WIDTH // n_pairs
    fw = GROUP_W // n_pairs
    n_blocks = SEQ // tq
    ones_row = jnp.where(lax.broadcasted_iota(jnp.int32, (VT_ROWS - HEAD_DIM, key_chunk), 0) == 0, 1.0, 0.0)
    for c in range(SEQ // key_chunk):
        rows = slice(c * key_chunk, (c + 1) * key_chunk)
        vt = v_ref[0, rows, :].astype(F32).T.astype(BF16)
        for h in range(N_HEADS):
            vt_ref[c, h * VT_ROWS:h * VT_ROWS + HEAD_DIM, :] = vt[h * HEAD_DIM:(h + 1) * HEAD_DIM]
            vt_ref[c, h * VT_ROWS + HEAD_DIM:(h + 1) * VT_ROWS, :] = ones_row.astype(BF16)

    feat = lax.broadcasted_iota(jnp.int32, (GROUP_W, tq), 0)
    mx0 = jnp.full((SUBLANES, WIDTH), NEG, F32)

    def load_queries(i):
        qt = q_ref[0, pl.ds(pl.multiple_of(i * tq, tq), tq), :].astype(F32).T
        for p in range(n_pairs):
            keep = (feat >= p * fw) & (feat < (p + 1) * fw)
            qt_ref[:, p * tq:(p + 1) * tq] = jnp.where(keep, qt, 0.0).astype(BF16)

    def scores(c, i, e_ref, mx):
        k0 = pl.multiple_of((key_start(i) + c) * key_chunk, key_chunk)
        s = _dot(k_ref[0, pl.ds(k0, key_chunk), :], qt_ref[...])
        cols = []
        for h in range(N_HEADS):
            b = bias(h, i, c)
            for p in range(h * n_pairs // N_HEADS, (h + 1) * n_pairs // N_HEADS):
                cols.append(s[:, p * tq:(p + 1) * tq] + b)
        e = jnp.concatenate(cols, axis=1)
        e_ref[pl.ds(pl.multiple_of(c * key_chunk, key_chunk), key_chunk), :] = e
        return jnp.maximum(mx, jnp.max(_sublane_groups(e), axis=0))

    def probs(c, i, e_ref, m):
        r0 = pl.multiple_of(c * key_chunk, key_chunk)
        pb = jnp.exp2(e_ref[pl.ds(r0, key_chunk), :] - m).astype(BF16)
        kc = key_start(i) + c
        for h in range(N_HEADS):
            acc_ref[h] += _dot(vt_ref[kc, h * VT_ROWS:(h + 1) * VT_ROWS, :],
                               pb[:, h * PAIR_LANES:(h + 1) * PAIR_LANES])

    def finish_block(i, m):
        sums = jnp.concatenate([acc_ref[h, HEAD_DIM:HEAD_DIM + 1, :] for h in range(N_HEADS)], axis=1)
        finish(i, m, sums, acc_ref)

    load_queries(0)
    mx = lax.fori_loop(0, win_chunks, lambda c, mx: scores(c, 0, e_refs[0], mx), mx0)

    def step(i, slot, m):
        load_queries(i + 1)
        acc_ref[...] = jnp.zeros_like(acc_ref)

        def both(c, mx):
            mx = scores(c, i + 1, e_refs[1 - slot], mx)
            probs(c, i, e_refs[slot], m)
            return mx

        mx = lax.fori_loop(0, win_chunks, both, mx0, unroll=True)
        finish_block(i, m)
        return jnp.max(mx, axis=0, keepdims=True)

    def two_steps(j, m):
        return step(2 * j + 1, 1, step(2 * j, 0, m))

    m = lax.fori_loop(0, n_blocks // 2 - 1, two_steps, jnp.max(mx, axis=0, keepdims=True))
    m = step(n_blocks - 2, 0, m)
    acc_ref[...] = jnp.zeros_like(acc_ref)

    def last(c, _):
        probs(c, n_blocks - 1, e_refs[1], m)
        return 0

    lax.fori_loop(0, win_chunks, last, 0)
    finish_block(n_blocks - 1, m)


def _attn_specs():
    def col(j):
        return pl.BlockSpec((1, SEQ, GROUP_W), lambda b: (b, 0, j))
    return [col(0), col(1), col(2), col(3)]


def _attn_scratch(n_pairs, key_chunk, win_chunks, tab_rows):
    tq = WIDTH // n_pairs
    return [pltpu.VMEM((N_HEADS, tab_rows, tq), F32),
            pltpu.VMEM((SEQ // key_chunk, N_HEADS * VT_ROWS, key_chunk), BF16),
            pltpu.VMEM((win_chunks * key_chunk, WIDTH), F32),
            pltpu.VMEM((win_chunks * key_chunk, WIDTH), F32),
            pltpu.VMEM((GROUP_W, WIDTH), BF16),
            pltpu.VMEM((N_HEADS, VT_ROWS, PAIR_LANES), F32)]


def _mask_values(v_ref, vm_ref):
    v = v_ref[0]
    for h in range(N_HEADS):
        vm_ref[h * SEQ:(h + 1) * SEQ, :] = jnp.where(_head_mask(h), v, jnp.zeros_like(v))


A_PAIRS = N_HEADS
A_TQ = WIDTH // A_PAIRS
A_CHUNK = 256
A_WIN = 3
A_FAR = DILATED_PATTERNS[-1][1]
A_NEAR_PATTERNS = DILATED_PATTERNS[:-1]
A_NEAR_REACH = max(r * (w // (2 * r)) for w, r in A_NEAR_PATTERNS)
assert A_NEAR_REACH <= A_CHUNK and A_TQ == A_CHUNK
A_TAB_OFF = 2 * A_CHUNK
A_TAB_ROWS = A_TAB_OFF + A_WIN * A_CHUNK
A_CLASS = SEQ // A_FAR
A_PART = GROUP_W + LANES


def _a_key_start(i):
    return jnp.clip(i - 1, 0, SEQ // A_CHUNK - A_WIN)


def _mixer_a_near_kernel(q_ref, k_ref, v_ref, part_ref, tab_ref, vt_ref, e0_ref, e1_ref, qt_ref, acc_ref, *, slopes):
    @pl.when(pl.program_id(0) == 0)
    def _():
        _build_toeplitz_t(tab_ref, slopes, A_NEAR_PATTERNS, A_TAB_OFF)

    def bias(h, i, c):
        t0 = pl.multiple_of((_a_key_start(i) + c - i) * A_CHUNK + A_TAB_OFF, A_CHUNK)
        return tab_ref[h, pl.ds(t0, A_CHUNK), :]

    def finish(i, m, sums, acc_ref):
        q0 = pl.multiple_of(i * A_TQ, A_TQ)
        ot = jnp.concatenate([acc_ref[h, 0:HEAD_DIM, :] for h in range(N_HEADS)], axis=0)
        stats = jnp.concatenate([m[:, h * A_TQ:(h + 1) * A_TQ] for h in range(N_HEADS)]
                                + [sums[:, h * A_TQ:(h + 1) * A_TQ] for h in range(N_HEADS)]
                                + [jnp.zeros((LANES - 2 * N_HEADS, A_TQ), F32)], axis=0)
        rec = jnp.concatenate([ot.T, stats.T], axis=1)
        for j in range(A_PART // LANES):
            part_ref[0, j, pl.ds(q0, A_TQ), :] = rec[:, j * LANES:(j + 1) * LANES]

    _attention_t(q_ref, k_ref, v_ref, vt_ref, (e0_ref, e1_ref), qt_ref, acc_ref, n_pairs=A_PAIRS,
                 key_chunk=A_CHUNK, win_chunks=A_WIN, key_start=_a_key_start, bias=bias, finish=finish)


def _mixer_a_far_kernel(p_ref, part_ref, o_ref, tab_ref, x_ref, y_ref, *, slopes):
    @pl.when(pl.program_id(0) == 0)
    def _():
        _build_toeplitz_t(tab_ref, slopes, DILATED_PATTERNS[-1:], 0, unit=A_FAR)

    def widen(i, _):
        r0 = pl.multiple_of(i * CONV_TILE, CONV_TILE)
        x = p_ref[0, pl.ds(r0, CONV_TILE), :].astype(F32)
        for j in range(4 * GROUP_W // LANES):
            x_ref[j, pl.ds(r0, CONV_TILE), :] = x[:, j * LANES:(j + 1) * LANES]
        return 0

    lax.fori_loop(0, SEQ // CONV_TILE, widen, 0)

    n = A_CLASS
    feat = lax.broadcasted_iota(jnp.int32, (GROUP_W, n), 0)
    ones_rows = jnp.where(lax.broadcasted_iota(jnp.int32, (VT_ROWS - HEAD_DIM, n), 0) == 0, 1.0, 0.0).astype(BF16)
    for rho in range(A_FAR):
        cls = pl.ds(rho, n, stride=A_FAR)
        tiles = GROUP_W // LANES
        rows_of = lambda ref, j0, nt: jnp.concatenate([ref[j, cls, :] for j in range(j0, j0 + nt)], axis=1)
        cols = lambda j: rows_of(x_ref, j * tiles, tiles)
        qt = cols(0).T
        qt_all = jnp.concatenate(
            [jnp.where((feat >= h * HEAD_DIM) & (feat < (h + 1) * HEAD_DIM), qt, 0.0) for h in range(N_HEADS)],
            axis=1).astype(BF16)
        s = _dot(cols(1).astype(BF16), qt_all)
        e = jnp.concatenate([s[:, h * n:(h + 1) * n] + tab_ref[h] for h in range(N_HEADS)], axis=1)
        m_far = jnp.max(e, axis=0, keepdims=True)
        pb = jnp.exp2(e - m_far).astype(BF16)
        vt = cols(2).T.astype(BF16)
        near = rows_of(part_ref.at[0], 0, tiles).T
        stats = part_ref[0, tiles, cls, :].T
        outs = []
        for h in range(N_HEADS):
            far = _dot(jnp.concatenate([vt[h * HEAD_DIM:(h + 1) * HEAD_DIM], ones_rows], axis=0),
                       pb[:, h * n:(h + 1) * n])
            m_near = stats[h:h + 1]
            m_h = m_far[:, h * n:(h + 1) * n]
            top = jnp.maximum(m_near, m_h)
            w_near = jnp.exp2(m_near - top)
            w_far = jnp.exp2(m_h - top)
            denom = stats[N_HEADS + h:N_HEADS + h + 1] * w_near + far[HEAD_DIM:HEAD_DIM + 1] * w_far
            outs.append((near[h * HEAD_DIM:(h + 1) * HEAD_DIM] * w_near + far[0:HEAD_DIM] * w_far) / denom)
        y = jnp.concatenate(outs, axis=0).T * _silu(cols(3))
        for j in range(tiles):
            y_ref[j, cls, :] = y[:, j * LANES:(j + 1) * LANES]

    def narrow(i, _):
        r0 = pl.multiple_of(i * CONV_TILE, CONV_TILE)
        o_ref[0, pl.ds(r0, CONV_TILE), :] = jnp.concatenate(
            [y_ref[j, pl.ds(r0, CONV_TILE), :] for j in range(GROUP_W // LANES)], axis=1).astype(BF16)
        return 0

    lax.fori_loop(0, SEQ // CONV_TILE, narrow, 0)


def _mixer_a(pa, slopes):
    bsz = pa.shape[0]
    col = lambda j: pl.BlockSpec((1, SEQ, GROUP_W), lambda b: (b, 0, j))
    part = pl.pallas_call(
        functools.partial(_mixer_a_near_kernel, slopes=slopes),
        grid=(bsz,),
        in_specs=[col(0), col(1), col(2)],
        out_specs=pl.BlockSpec((1, A_PART // LANES, SEQ, LANES), lambda b: (b, 0, 0, 0)),
        out_shape=jax.ShapeDtypeStruct((bsz, A_PART // LANES, SEQ, LANES), F32),
        scratch_shapes=_attn_scratch(A_PAIRS, A_CHUNK, A_WIN, A_TAB_ROWS),
        compiler_params=_params(1),
        name="mixer_a_near",
    )(pa, pa, pa)
    whole = lambda width: pl.BlockSpec((1, SEQ, width), lambda b: (b, 0, 0))
    return pl.pallas_call(
        functools.partial(_mixer_a_far_kernel, slopes=slopes),
        grid=(bsz,),
        in_specs=[whole(4 * GROUP_W), pl.BlockSpec((1, A_PART // LANES, SEQ, LANES), lambda b: (b, 0, 0, 0))],
        out_specs=whole(GROUP_W),
        out_shape=jax.ShapeDtypeStruct((bsz, SEQ, GROUP_W), BF16),
        scratch_shapes=[pltpu.VMEM((N_HEADS, A_CLASS, A_CLASS), F32),
                        pltpu.VMEM((4 * GROUP_W // LANES, SEQ, LANES), F32),
                        pltpu.VMEM((GROUP_W // LANES, SEQ, LANES), F32)],
        compiler_params=_params(1),
        name="mixer_a_far",
    )(pa, part)


B_PAIRS = 2 * N_HEADS
B_TQ = WIDTH // B_PAIRS


def _mixer_b_kernel(q_ref, k_ref, v_ref, g_ref, lam_ref, nw_ref, o_ref, tab_ref, vt_ref, e0_ref, e1_ref, qt_ref,
                    acc_ref, *, slopes, lam_init):
    tq = B_TQ

    @pl.when(pl.program_id(0) == 0)
    def _():
        _build_toeplitz_t(tab_ref, slopes, None, SEQ - tq)

    def bias(h, i, c):
        return tab_ref[h, pl.ds(pl.multiple_of(SEQ - tq - i * tq + c * KEY_CHUNK, LANES), KEY_CHUNK), :]

    lv = lam_ref[...]
    lam = (jnp.exp(jnp.sum(lv[0:1] * lv[1:2], axis=-1, keepdims=True))
           - jnp.exp(jnp.sum(lv[2:3] * lv[3:4], axis=-1, keepdims=True)) + lam_init)
    nw = nw_ref[...]

    def finish(i, m, sums, acc_ref):
        q0 = pl.multiple_of(i * tq, tq)
        inv = 1.0 / sums
        outs = []
        for h in range(N_HEADS):
            acc = acc_ref[h, 0:HEAD_DIM, :]
            o = (acc[:, 0:tq] * inv[:, 2 * h * tq:(2 * h + 1) * tq]
                 - lam * (acc[:, tq:2 * tq] * inv[:, (2 * h + 1) * tq:(2 * h + 2) * tq]))
            ms = jnp.mean(o * o, axis=0, keepdims=True)
            outs.append(o * lax.rsqrt(ms + EPS))
        on = jnp.concatenate(outs, axis=0).T * nw * (1.0 - lam_init)
        g = g_ref[0, pl.ds(q0, tq), :].astype(F32)
        o_ref[0, pl.ds(q0, tq), :] = (on * _silu(g)).astype(BF16)

    _attention_t(q_ref, k_ref, v_ref, vt_ref, (e0_ref, e1_ref), qt_ref, acc_ref, n_pairs=B_PAIRS,
                 key_chunk=KEY_CHUNK, win_chunks=SEQ // KEY_CHUNK, key_start=lambda i: 0, bias=bias, finish=finish)


def _mixer_b(pb, lam_p, nw256, slopes, lam_init):
    bsz = pb.shape[0]
    return pl.pallas_call(
        functools.partial(_mixer_b_kernel, slopes=slopes, lam_init=lam_init),
        grid=(bsz,),
        in_specs=_attn_specs() + [pl.BlockSpec((4, DIFF_HEAD_DIM), lambda b: (0, 0)),
                                  pl.BlockSpec((1, GROUP_W), lambda b: (0, 0))],
        out_specs=pl.BlockSpec((1, SEQ, GROUP_W), lambda b: (b, 0, 0)),
        out_shape=jax.ShapeDtypeStruct((bsz, SEQ, GROUP_W), BF16),
        scratch_shapes=_attn_scratch(B_PAIRS, KEY_CHUNK, SEQ // KEY_CHUNK, 2 * SEQ - B_TQ),
        compiler_params=_params(1),
        name="mixer_b_diff",
    )(pb, pb, pb, pb, lam_p, nw256)


CONV_TILE = 256
CONV_HALO = SUBLANES
N_CHUNK = SEQ // SSM_CHUNK


def _expand_heads(x, expand3):
    hi = x.astype(BF16)
    r1 = x - hi.astype(F32)
    mid = r1.astype(BF16)
    lo = (r1 - mid.astype(F32)).astype(BF16)
    return _dot(jnp.concatenate([hi, mid, lo], axis=1), expand3)


def _cumsum_rows(a):
    row = lax.broadcasted_iota(jnp.int32, a.shape, 0)
    s = 1
    while s < a.shape[0]:
        a = a + jnp.where(row >= s, pltpu.roll(a, s, 0), 0.0)
        s *= 2
    return a


def _mixer_c_kernel(p_ref, dt_ref, cw_ref, cb_ref, alog_ref, alogx_ref, dtb_ref, dskip_ref, nw_ref, exp_ref,
                    o_ref, xpad, xc, y_s, sb_s, cs_s, db_s):
    L = SSM_CHUNK
    zero_rows = jnp.zeros((CONV_HALO, SSM_XBC), F32)
    xpad[0:CONV_HALO, :] = zero_rows
    xpad[CONV_HALO + SEQ:CONV_HALO + SEQ + CONV_HALO, :] = zero_rows

    def fill(i, _):
        r0 = pl.multiple_of(i * CONV_TILE, CONV_TILE)
        xpad[pl.ds(CONV_HALO + r0, CONV_TILE), :] = p_ref[0, pl.ds(r0, CONV_TILE), GROUP_W:].astype(F32)
        return 0

    lax.fori_loop(0, SEQ // CONV_TILE, fill, 0)

    def conv(i, _):
        r0 = pl.multiple_of(i * CONV_TILE, CONV_TILE)
        rows = CONV_TILE + 2 * CONV_HALO
        win = xpad[pl.ds(r0, rows), :]
        acc = jnp.zeros((CONV_TILE, SSM_XBC), F32) + cb_ref[...]
        for j in range(SSM_CONV):
            back = (SSM_CONV // 2 - j) % rows
            tap = win if back == 0 else pltpu.roll(win, back, 0)
            acc = acc + cw_ref[j:j + 1, :] * tap[CONV_HALO:CONV_HALO + CONV_TILE, :]
        xc[pl.ds(r0, CONV_TILE), :] = _silu(acc)
        return 0

    lax.fori_loop(0, SEQ // CONV_TILE, conv, 0)

    a_neg = -jnp.exp(alog_ref[...])
    a_neg_x = -jnp.exp(alogx_ref[...])
    expand = exp_ref[...]
    li = lax.broadcasted_iota(jnp.int32, (L, L), 0)
    si = lax.broadcasted_iota(jnp.int32, (L, L), 1)
    lower = si <= li
    upper = si >= li
    hmasks = [_head_mask(h) for h in range(N_HEADS)]

    def chunk_terms(t0):
        dtr = dt_ref[0, pl.ds(t0, L), :] + dtb_ref[...]
        dt = jnp.maximum(dtr, 0.0) + jnp.log(1.0 + jnp.exp(-jnp.abs(dtr)))
        a = dt * a_neg
        ainc = _cumsum_rows(a)
        aexc = ainc - a
        return dt, ainc, aexc

    def fwd(c, hf):
        t0 = pl.multiple_of(c * L, L)
        dt, ainc, aexc = chunk_terms(t0)
        both_x = _expand_heads(jnp.concatenate([dt, ainc], axis=0), expand)
        dt_x = both_x[0:L]
        ainc_x = both_x[L:2 * L]
        aexc_x = ainc_x - dt_x * a_neg_x
        ainc_t = ainc.T
        aexc_t = aexc.T
        xs = xc[pl.ds(t0, L), 0:GROUP_W]
        bm = xc[pl.ds(t0, L), GROUP_W:2 * GROUP_W]
        cm = xc[pl.ds(t0, L), 2 * GROUP_W:3 * GROUP_W]
        xf = xs * dt_x[:, 0:GROUP_W]
        xb = xs * dt_x[:, GROUP_W:2 * GROUP_W]
        xcat = jnp.concatenate([xf, xb], axis=0).astype(BF16)
        tot_f = ainc_x[L - 1:L, 0:GROUP_W]
        tot_b = ainc_x[L - 1:L, GROUP_W:2 * GROUP_W]
        y = xs * dskip_ref[...]
        cbs = []
        for g in range(SSM_GROUPS):
            gs = slice(g * SSM_STATE, (g + 1) * SSM_STATE)
            cbs.append(_dot_nt(cm[:, gs].astype(BF16), bm[:, gs].astype(BF16)))
        for h in range(N_HEADS):
            cb = cbs[h // (N_HEADS // SSM_GROUPS)]
            col_f = ainc[:, h:h + 1]
            row_f = ainc_t[h:h + 1, :]
            col_b = aexc[:, N_HEADS + h:N_HEADS + h + 1]
            row_b = aexc_t[N_HEADS + h:N_HEADS + h + 1, :]
            lf = jnp.exp(jnp.where(lower, col_f - row_f, NEG))
            ub = jnp.exp(jnp.where(upper, row_b - col_b, NEG))
            mcat = jnp.concatenate([cb * lf, cb * ub], axis=1).astype(BF16)
            y = y + jnp.where(hmasks[h], _dot(mcat, xcat), 0.0)
        wf = (jnp.exp(tot_f - ainc_x[:, 0:GROUP_W]) * xf).astype(BF16)
        wb = (jnp.exp(aexc_x[:, GROUP_W:2 * GROUP_W]) * xb).astype(BF16)
        ef = jnp.exp(ainc_x[:, 0:GROUP_W])
        cs_s[pl.ds(t0, L), :] = jnp.exp(tot_b - aexc_x[:, GROUP_W:2 * GROUP_W])
        db_s[pl.ds(pl.multiple_of(c * SUBLANES, SUBLANES), SUBLANES), :] = jnp.broadcast_to(
            jnp.exp(tot_b), (SUBLANES, GROUP_W))
        dec_f = jnp.exp(tot_f)
        hf_new = []
        yoff = []
        for g in range(SSM_GROUPS):
            gs = slice(g * SSM_STATE, (g + 1) * SSM_STATE)
            bt = bm[:, gs].T.astype(BF16)
            yoff.append(_dot(cm[:, gs].astype(BF16), hf[g].astype(BF16)))
            hf_new.append(dec_f[:, gs] * hf[g] + _dot(bt, wf[:, gs]))
            sb_s[pl.ds(t0, L), gs] = _dot(bt, wb[:, gs])
        y = y + jnp.concatenate(yoff, axis=1) * ef
        y_s[pl.ds(t0, L), :] = y
        return tuple(hf_new)

    h0 = tuple(jnp.zeros((SSM_STATE, SSM_STATE), F32) for _ in range(SSM_GROUPS))
    lax.fori_loop(0, N_CHUNK, fwd, h0)

    def bwd(i, hb):
        c = N_CHUNK - 1 - i
        t0 = pl.multiple_of(c * L, L)
        cm = xc[pl.ds(t0, L), 2 * GROUP_W:3 * GROUP_W]
        dec_b = db_s[pl.ds(pl.multiple_of(c * SUBLANES, SUBLANES), 1), :]
        yoff = []
        hb_new = []
        for g in range(SSM_GROUPS):
            gs = slice(g * SSM_STATE, (g + 1) * SSM_STATE)
            yoff.append(_dot(cm[:, gs].astype(BF16), hb[g].astype(BF16)))
            hb_new.append(dec_b[:, gs] * hb[g] + sb_s[pl.ds(t0, L), gs])
        y_s[pl.ds(t0, L), :] = y_s[pl.ds(t0, L), :] + jnp.concatenate(yoff, axis=1) * cs_s[pl.ds(t0, L), :]
        return tuple(hb_new)

    lax.fori_loop(0, N_CHUNK, bwd, h0)

    def fin(i, _):
        r0 = pl.multiple_of(i * CONV_TILE, CONV_TILE)
        z = p_ref[0, pl.ds(r0, CONV_TILE), 0:GROUP_W].astype(F32)
        y = y_s[pl.ds(r0, CONV_TILE), :] * _silu(z)
        parts = []
        for g in range(SSM_GROUPS):
            yg = y[:, g * SSM_STATE:(g + 1) * SSM_STATE]
            parts.append(yg * lax.rsqrt(jnp.mean(yg * yg, axis=-1, keepdims=True) + EPS))
        o_ref[0, pl.ds(r0, CONV_TILE), :] = (jnp.concatenate(parts, axis=1) * nw_ref[...]).astype(BF16)
        return 0

    lax.fori_loop(0, SEQ // CONV_TILE, fin, 0)


def _head_expand_matrix():
    j = lax.broadcasted_iota(jnp.int32, (LANES, 2 * GROUP_W), 0)
    c = lax.broadcasted_iota(jnp.int32, (LANES, 2 * GROUP_W), 1)
    return jnp.tile((j == c // HEAD_DIM).astype(BF16), (3, 1))


def _mixer_c(pc, pdt, conv_w, conv_b, a_log, dt_bias, d_skip, norm_w):
    bsz = pc.shape[0]
    pad8 = lambda v: jnp.pad(v.reshape(1, 2 * N_HEADS), ((0, 0), (0, LANES - 2 * N_HEADS)))
    small = lambda shape: pl.BlockSpec(shape, lambda b: (0,) * len(shape))
    return pl.pallas_call(
        _mixer_c_kernel,
        grid=(bsz,),
        in_specs=[pl.BlockSpec((1, SEQ, 1024), lambda b: (b, 0, 0)),
                  pl.BlockSpec((1, SEQ, DT_PAD), lambda b: (b, 0, 0)),
                  small((SSM_CONV, SSM_XBC)), small((1, SSM_XBC)),
                  small((1, LANES)), small((1, 2 * GROUP_W)), small((1, LANES)),
                  small((1, GROUP_W)), small((1, GROUP_W)),
                  small((3 * LANES, 2 * GROUP_W))],
        out_specs=pl.BlockSpec((1, SEQ, GROUP_W), lambda b: (b, 0, 0)),
        out_shape=jax.ShapeDtypeStruct((bsz, SEQ, GROUP_W), BF16),
        scratch_shapes=[pltpu.VMEM((SEQ + 2 * CONV_HALO, SSM_XBC), F32),
                        pltpu.VMEM((SEQ, SSM_XBC), F32),
                        pltpu.VMEM((SEQ, GROUP_W), F32),
                        pltpu.VMEM((SEQ, GROUP_W), F32),
                        pltpu.VMEM((SEQ, GROUP_W), F32),
                        pltpu.VMEM((N_CHUNK * SUBLANES, GROUP_W), F32)],
        compiler_params=_params(1),
        name="mixer_c_ssd",
    )(pc, pdt, conv_w, conv_b.reshape(1, SSM_XBC), pad8(a_log),
      jnp.repeat(a_log.reshape(-1), HEAD_DIM).reshape(1, 2 * GROUP_W), pad8(dt_bias),
      jnp.repeat(d_skip, HEAD_DIM).reshape(1, GROUP_W), norm_w.reshape(1, GROUP_W),
      _head_expand_matrix())


NA_ROWS = SEQ // GRID_W
NA_DR = 2 * NA_WIN_H - 1
NA_DC = 2 * NA_WIN_W - 1
NA_PAIRS = NA_DR - 1


def _na_bias_kernel(rpb_ref, o_ref):
    lh = pl.program_id(0)
    cq = lax.broadcasted_iota(jnp.int32, (GRID_W, LANES), 0)
    lane = lax.broadcasted_iota(jnp.int32, (GRID_W, LANES), 1)
    second = lane >= GRID_W
    ck = jnp.where(second, lane - GRID_W, lane)
    cs = jnp.clip(cq - NA_WIN_W // 2, 0, GRID_W - NA_WIN_W)
    inside = (ck >= cs) & (ck < cs + NA_WIN_W)
    dc = ck - cq + NA_WIN_W - 1
    base = lh * (NA_DR * NA_DC)
    for p in range(NA_PAIRS):
        acc = jnp.zeros((GRID_W, LANES), F32)
        for j in range(NA_DC):
            lo = rpb_ref[base + p * NA_DC + j]
            hi = rpb_ref[base + (p + 1) * NA_DC + j]
            acc = acc + jnp.where(dc == j, jnp.where(second, hi, lo), 0.0)
        o_ref[0, p] = jnp.where(inside, acc, NEG)


def _na_bias(na_rpb):
    n = DEPTH * N_HEADS
    return pl.pallas_call(
        _na_bias_kernel,
        grid=(n,),
        in_specs=[pl.BlockSpec(memory_space=pltpu.SMEM)],
        out_specs=pl.BlockSpec((1, NA_PAIRS, GRID_W, LANES), lambda i: (i, 0, 0, 0)),
        out_shape=jax.ShapeDtypeStruct((n, NA_PAIRS, GRID_W, LANES), F32),
        compiler_params=_params(1),
        name="na_bias_table",
    )(na_rpb.reshape(-1))


NA_UNROLL = 4


def _mixer_d_kernel(q_ref, k_ref, v_ref, g_ref, bias_ref, o_ref, vm_ref):
    nkeys = NA_WIN_H * GRID_W
    _mask_values(v_ref, vm_ref)

    def one_row(r):
        rs = jnp.clip(r - NA_WIN_H // 2, 0, NA_ROWS - NA_WIN_H)
        delta = r - rs
        q0 = pl.multiple_of(r * GRID_W, GRID_W)
        k0 = pl.multiple_of(rs * GRID_W, GRID_W)
        q = q_ref[0, pl.ds(q0, GRID_W), :]
        qs = jnp.concatenate([jnp.where(_head_mask(h), q, jnp.zeros_like(q)) for h in range(N_HEADS)], axis=0)
        s_all = _dot_nt(qs, k_ref[0, pl.ds(k0, nkeys), :])
        ps, invs = [], []
        for h in range(N_HEADS):
            bias = jnp.concatenate(
                [bias_ref[h, 2 * i - delta + NA_WIN_H - 1] for i in range(NA_WIN_H // 2)], axis=-1)
            e = s_all[h * GRID_W:(h + 1) * GRID_W] + bias
            m = jnp.max(e, axis=-1, keepdims=True)
            p = jnp.exp(e - m)
            invs.append(1.0 / jnp.sum(p, axis=-1, keepdims=True))
            ps.append(p.astype(BF16))
        vcat = jnp.concatenate([vm_ref[pl.ds(pl.multiple_of(h * SEQ + k0, GRID_W), nkeys), :]
                                for h in range(N_HEADS)], axis=0)
        acc = _dot(jnp.concatenate(ps, axis=-1), vcat)
        inv = jnp.broadcast_to(invs[N_HEADS - 1], (GRID_W, GROUP_W))
        for h in range(N_HEADS - 2, -1, -1):
            inv = jnp.where(_head_mask(h), invs[h], inv)
        g = g_ref[0, pl.ds(q0, GRID_W), :].astype(F32)
        o_ref[0, pl.ds(q0, GRID_W), :] = (acc * inv * _silu(g)).astype(BF16)

    def rstep(i, _):
        for u in range(NA_UNROLL):
            one_row(i * NA_UNROLL + u)
        return 0

    lax.fori_loop(0, NA_ROWS // NA_UNROLL, rstep, 0)


def _mixer_d(pd, bias, l):
    bsz = pd.shape[0]
    return pl.pallas_call(
        _mixer_d_kernel,
        grid=(bsz,),
        in_specs=_attn_specs() + [pl.BlockSpec((N_HEADS, NA_PAIRS, GRID_W, LANES), lambda b: (l, 0, 0, 0))],
        out_specs=pl.BlockSpec((1, SEQ, GROUP_W), lambda b: (b, 0, 0)),
        out_shape=jax.ShapeDtypeStruct((bsz, SEQ, GROUP_W), BF16),
        scratch_shapes=[pltpu.VMEM((N_HEADS * SEQ, GROUP_W), BF16)],
        compiler_params=_params(1),
        name="mixer_d_neighbourhood",
    )(pd, pd, pd, pd, bias)


def kernel(x, c, norm_w, ada_w, ada_b, w_in, diff_lambda, diff_norm_w, conv_w, conv_b, ssm_a_log,
           ssm_dt_bias, ssm_d, ssm_norm_w, na_rpb, w_out, final_norm_w):
    bsz = x.shape[0]
    n = 2 * N_HEADS
    slopes = [2.0 ** (-8.0 * i / n) for i in range(1, n + 1)]
    slopes_a, slopes_b = tuple(slopes[0::2]), tuple(slopes[1::2])
    mod4 = _adaln(c, ada_w, ada_b).reshape(DEPTH, bsz, 1, 3 * D_MODEL)
    na_bias = _na_bias(na_rpb)
    w_main, w_dt = _prep_w_in(w_in)
    norm_w3 = norm_w.reshape(DEPTH, 1, D_MODEL)
    final_w = final_norm_w.reshape(1, D_MODEL)
    for l in range(DEPTH):
        pa, pb, pc, pd, pdt = _inproj(x, mod4, norm_w3, w_main, w_dt, l)
        lam_init = 0.8 - 0.6 * math.exp(-0.3 * l)
        ya = _mixer_a(pa, slopes_a)
        yb = _mixer_b(pb, diff_lambda[l], jnp.tile(diff_norm_w[l], N_HEADS).reshape(1, GROUP_W),
                      slopes_b, lam_init)
        yc = _mixer_c(pc, pdt, conv_w[l], conv_b[l], ssm_a_log[l], ssm_dt_bias[l], ssm_d[l], ssm_norm_w[l])
        yd = _mixer_d(pd, na_bias, l)
        x = _outproj((ya, yb, yc, yd), w_out, x, mod4, final_w, l)
    return x
```

```python
import functools
import math

import jax
import jax.numpy as jnp
from jax import lax
from jax.experimental import pallas as pl
from jax.experimental.pallas import tpu as pltpu

D_MODEL = 1024
SEQ = 2048
DEPTH = 2
HEAD_DIM = 64
GROUP_W = 256
N_HEADS = 4
EPS = 1e-6
DILATED_PATTERNS = ((128, 1), (512, 4), (2048, 16))
DIFF_HEAD_DIM = 32
SSM_GROUPS = 2
SSM_STATE = 128
SSM_CONV = 5
SSM_CHUNK = 128
SSM_XBC = 768
GRID_W = 64
NA_WIN_H = 8
NA_WIN_W = 16
D_IN = 13 * GROUP_W + SSM_XBC + 2 * N_HEADS

LANES = 128
SUBLANES = 8
VMEM_LIMIT = 56 * 1024 * 1024

NEG = -1e30
LOG2E = math.log2(math.e)
F32 = jnp.float32
BF16 = jnp.bfloat16
HIGHEST = lax.Precision.HIGHEST

ROW_TILE = 512
DT_PAD = LANES


def _silu(x):
    return x / (1.0 + jnp.exp(-x))


def _dot_nt(a, b):
    return lax.dot_general(a, b, (((1,), (1,)), ((), ())), preferred_element_type=F32)


def _dot(a, b):
    return jnp.dot(a, b, preferred_element_type=F32)


def _params(n_grid):
    return pltpu.CompilerParams(dimension_semantics=("arbitrary",) * n_grid,
                                vmem_limit_bytes=VMEM_LIMIT)


def _head_mask(h, width=HEAD_DIM, total=GROUP_W):
    lane = lax.broadcasted_iota(jnp.int32, (1, total), 1)
    return (lane >= h * width) & (lane < (h + 1) * width)


def _mod_kernel(c_ref, w_ref, b_ref, o_ref):
    c = c_ref[...]
    o_ref[0] = jnp.dot(_silu(c), w_ref[0], precision=HIGHEST,
                       preferred_element_type=F32) + b_ref[0]


def _adaln(c, ada_w, ada_b):
    bsz = c.shape[0]
    tn = 768
    return pl.pallas_call(
        _mod_kernel,
        grid=(DEPTH, 3 * D_MODEL // tn),
        in_specs=[pl.BlockSpec((bsz, D_MODEL), lambda l, j: (0, 0)),
                  pl.BlockSpec((1, D_MODEL, tn), lambda l, j: (l, 0, j)),
                  pl.BlockSpec((1, 1, tn), lambda l, j: (l, 0, j))],
        out_specs=pl.BlockSpec((1, bsz, tn), lambda l, j: (l, 0, j)),
        out_shape=jax.ShapeDtypeStruct((DEPTH, bsz, 3 * D_MODEL), F32),
        compiler_params=_params(2),
        name="adaln_mod",
    )(c, ada_w, ada_b.reshape(DEPTH, 1, 3 * D_MODEL))


W_MAIN = 16 * GROUP_W
DT_COL0 = 9 * GROUP_W + SSM_XBC
PREP_TILE = 512


def _prep_kernel(w_ref, wd_ref, wdt_ref, sc_ref, scdt_ref, om_ref, odt_ref):
    j = pl.program_id(1)

    @pl.when(j < DT_COL0 // PREP_TILE)
    def _():
        om_ref[0] = (w_ref[0] * sc_ref[...]).astype(BF16)

    @pl.when(j >= DT_COL0 // PREP_TILE)
    def _():
        om_ref[0] = (wd_ref[0] * sc_ref[...]).astype(BF16)

    @pl.when(j == 0)
    def _():
        odt_ref[0] = (wdt_ref[0] * scdt_ref[...]).astype(BF16)


def _prep_w_in(w_in):
    scale = [1.0] * W_MAIN
    for c0, s in ((0, HEAD_DIM ** -0.5 * LOG2E), (4 * GROUP_W, DIFF_HEAD_DIM ** -0.5 * LOG2E),
                  (12 * GROUP_W, HEAD_DIM ** -0.5)):
        scale[c0:c0 + GROUP_W] = [s] * GROUP_W
    sc = jnp.asarray(scale, F32).reshape(1, W_MAIN)
    scdt = jnp.asarray([1.0] * (2 * N_HEADS) + [0.0] * (DT_PAD - 2 * N_HEADS), F32).reshape(1, DT_PAD)
    w_d = w_in[:, :, DT_COL0 + 2 * N_HEADS:]
    n_lo = DT_COL0 // PREP_TILE
    return pl.pallas_call(
        _prep_kernel,
        grid=(DEPTH, W_MAIN // PREP_TILE),
        in_specs=[pl.BlockSpec((1, D_MODEL, PREP_TILE), lambda l, j: (l, 0, jnp.minimum(j, n_lo - 1))),
                  pl.BlockSpec((1, D_MODEL, PREP_TILE), lambda l, j: (l, 0, jnp.maximum(j - n_lo, 0))),
                  pl.BlockSpec((1, D_MODEL, DT_PAD), lambda l, j: (l, 0, DT_COL0 // DT_PAD)),
                  pl.BlockSpec((1, PREP_TILE), lambda l, j: (0, j)),
                  pl.BlockSpec((1, DT_PAD), lambda l, j: (0, 0))],
        out_specs=[pl.BlockSpec((1, D_MODEL, PREP_TILE), lambda l, j: (l, 0, j)),
                   pl.BlockSpec((1, D_MODEL, DT_PAD), lambda l, j: (l, 0, 0))],
        out_shape=[jax.ShapeDtypeStruct((DEPTH, D_MODEL, W_MAIN), BF16),
                   jax.ShapeDtypeStruct((DEPTH, D_MODEL, DT_PAD), BF16)],
        compiler_params=_params(2),
        name="prep_w_in",
    )(w_in, w_d, w_in, sc, scdt)


def _inproj_kernel(x_ref, mod_ref, nw_ref, w_ref, wdt_ref, pa_ref, pb_ref, pc_ref, pd_ref, pdt_ref):
    x = x_ref[0]
    shift = mod_ref[:, 0:D_MODEL]
    scale = mod_ref[:, D_MODEL:2 * D_MODEL]
    y = x * lax.rsqrt(jnp.mean(x * x, axis=-1, keepdims=True) + EPS) * nw_ref[...]
    h = (y * (1.0 + scale) + shift).astype(BF16)
    for i, ref in enumerate((pa_ref, pb_ref, pc_ref, pd_ref)):
        ref[0] = _dot(h, w_ref[:, i * 1024:(i + 1) * 1024]).astype(BF16)
    pdt_ref[0] = _dot(h, wdt_ref[...])


def _inproj(x, mod4, norm_w, w_main, w_dt, l):
    bsz = x.shape[0]
    row = lambda b, t: (b, t, 0)
    big = pl.BlockSpec((1, ROW_TILE, 1024), row)
    return pl.pallas_call(
        _inproj_kernel,
        grid=(bsz, SEQ // ROW_TILE),
        in_specs=[pl.BlockSpec((1, ROW_TILE, D_MODEL), row),
                  pl.BlockSpec((None, None, 1, 3 * D_MODEL), lambda b, t: (l, b, 0, 0)),
                  pl.BlockSpec((None, 1, D_MODEL), lambda b, t: (l, 0, 0)),
                  pl.BlockSpec((None, D_MODEL, W_MAIN), lambda b, t: (l, 0, 0)),
                  pl.BlockSpec((None, D_MODEL, DT_PAD), lambda b, t: (l, 0, 0))],
        out_specs=[big, big, big, big, pl.BlockSpec((1, ROW_TILE, DT_PAD), row)],
        out_shape=[jax.ShapeDtypeStruct((bsz, SEQ, 1024), BF16)] * 4
        + [jax.ShapeDtypeStruct((bsz, SEQ, DT_PAD), F32)],
        compiler_params=_params(2),
        name="inproj",
    )(x, mod4, norm_w, w_main, w_dt)


def _outproj_kernel(ya_ref, yb_ref, yc_ref, yd_ref, w_ref, x_ref, mod_ref, fw_ref, o_ref, wb_ref, *, final):
    @pl.when((pl.program_id(0) == 0) & (pl.program_id(1) == 0))
    def _():
        wb_ref[...] = w_ref[...].astype(BF16)

    acc = None
    for i, ref in enumerate((ya_ref, yb_ref, yc_ref, yd_ref)):
        part = _dot(ref[0], wb_ref[i * GROUP_W:(i + 1) * GROUP_W, :])
        acc = part if acc is None else acc + part
    gate = mod_ref[:, 2 * D_MODEL:3 * D_MODEL]
    xn = x_ref[0] + gate * acc
    if final:
        xn = xn * lax.rsqrt(jnp.mean(xn * xn, axis=-1, keepdims=True) + EPS) * fw_ref[...]
    o_ref[0] = xn


def _outproj(ys, w_out, x, mod4, final_w, l):
    bsz = x.shape[0]
    final = l == DEPTH - 1
    row = lambda b, t: (b, t, 0)
    yspec = pl.BlockSpec((1, ROW_TILE, GROUP_W), row)
    return pl.pallas_call(
        functools.partial(_outproj_kernel, final=final),
        grid=(bsz, SEQ // ROW_TILE),
        in_specs=[yspec, yspec, yspec, yspec,
                  pl.BlockSpec((None, D_MODEL, D_MODEL), lambda b, t: (l, 0, 0)),
                  pl.BlockSpec((1, ROW_TILE, D_MODEL), row),
                  pl.BlockSpec((None, None, 1, 3 * D_MODEL), lambda b, t: (l, b, 0, 0)),
                  pl.BlockSpec((1, D_MODEL), lambda b, t: (0, 0))],
        out_specs=pl.BlockSpec((1, ROW_TILE, D_MODEL), row),
        out_shape=jax.ShapeDtypeStruct((bsz, SEQ, D_MODEL), F32),
        scratch_shapes=[pltpu.VMEM((D_MODEL, D_MODEL), BF16)],
        compiler_params=_params(2),
        name="outproj_final" if final else "outproj",
    )(*ys, w_out, x, mod4, final_w)


KEY_CHUNK = 256
WIDTH = 1024
PAIR_LANES = WIDTH // N_HEADS
VT_ROWS = HEAD_DIM + 16


def _build_toeplitz_t(tab_ref, slopes, patterns, off, unit=1):
    _, rows, tq = tab_ref.shape
    step = min(rows, 512)
    i_io = lax.broadcasted_iota(jnp.int32, (step, tq), 0)
    r_io = lax.broadcasted_iota(jnp.int32, (step, tq), 1)
    base = (i_io - r_io - off) * unit

    def write(i0, n):
        d = base[:n] + i0 * unit
        ad = jnp.abs(d)
        adf = ad.astype(F32)
        logm, valid = 0.0, None
        if patterns is not None:
            mult = jnp.zeros(d.shape, F32)
            for w, r in patterns:
                reach = r * (w // (2 * r))
                mult = mult + jnp.where(ad <= reach, jnp.where((d & (r - 1)) == 0, 1.0, 0.0), 0.0)
            valid = mult > 0.5
            logm = jnp.log(jnp.maximum(mult, 1.0))
        for h, slope in enumerate(slopes):
            val = (logm - slope * adf) * LOG2E
            if valid is not None:
                val = jnp.where(valid, val, NEG)
            tab_ref[h, pl.ds(i0, n), :] = val

    def body(j, _):
        write(pl.multiple_of(j * step, step), step)
        return 0

    lax.fori_loop(0, rows // step, body, 0)
    if rows % step:
        write(rows - rows % step, rows % step)


def _sublane_groups(x):
    return x.reshape(x.shape[0] // SUBLANES, SUBLANES, x.shape[1])


def _attention_t(q_ref, k_ref, v_ref, vt_ref, e_refs, qt_ref, acc_ref, *, n_pairs, key_chunk, win_chunks,
                 key_start, bias, finish):
    tq = WIDTH // n_pairs
    fw = GROUP_W // n_pairs
    n_blocks = SEQ // tq
    ones_row = jnp.where(lax.broadcasted_iota(jnp.int32, (VT_ROWS - HEAD_DIM, key_chunk), 0) == 0, 1.0, 0.0)
    for c in range(SEQ // key_chunk):
        rows = slice(c * key_chunk, (c + 1) * key_chunk)
        vt = v_ref[0, rows, :].astype(F32).T.astype(BF16)
        for h in range(N_HEADS):
            vt_ref[c, h * VT_ROWS:h * VT_ROWS + HEAD_DIM, :] = vt[h * HEAD_DIM:(h + 1) * HEAD_DIM]
            vt_ref[c, h * VT_ROWS + HEAD_DIM:(h + 1) * VT_ROWS, :] = ones_row.astype(BF16)

    feat = lax.broadcasted_iota(jnp.int32, (GROUP_W, tq), 0)
    mx0 = jnp.full((SUBLANES, WIDTH), NEG, F32)

    def load_queries(i):
        qt = q_ref[0, pl.ds(pl.multiple_of(i * tq, tq), tq), :].astype(F32).T
        for p in range(n_pairs):
            keep = (feat >= p * fw) & (feat < (p + 1) * fw)
            qt_ref[:, p * tq:(p + 1) * tq] = jnp.where(keep, qt, 0.0).astype(BF16)

    pph = n_pairs // N_HEADS

    def scores_head(c, i, h, e_ref, mx):
        lanes = slice(h * PAIR_LANES, (h + 1) * PAIR_LANES)
        k0 = pl.multiple_of((key_start(i) + c) * key_chunk, key_chunk)
        s = _dot(k_ref[0, pl.ds(k0, key_chunk), :], qt_ref[:, lanes])
        b = bias(h, i, c)
        e = jnp.concatenate([s[:, p * tq:(p + 1) * tq] + b for p in range(pph)], axis=1)
        e_ref[pl.ds(pl.multiple_of(c * key_chunk, key_chunk), key_chunk), lanes] = e
        return jnp.maximum(mx, jnp.max(_sublane_groups(e), axis=0))

    def scores(c, i, e_ref, mx):
        parts = [scores_head(c, i, h, e_ref, mx[:, h * PAIR_LANES:(h + 1) * PAIR_LANES]) for h in range(N_HEADS)]
        return jnp.concatenate(parts, axis=1)

    def probs_head(c, i, h, e_ref, m):
        lanes = slice(h * PAIR_LANES, (h + 1) * PAIR_LANES)
        r0 = pl.multiple_of(c * key_chunk, key_chunk)
        pb = jnp.exp2(e_ref[pl.ds(r0, key_chunk), lanes] - m[:, lanes]).astype(BF16)
        acc_ref[h] += _dot(vt_ref[key_start(i) + c, h * VT_ROWS:(h + 1) * VT_ROWS, :], pb)

    def probs(c, i, e_ref, m):
        for h in range(N_HEADS):
            probs_head(c, i, h, e_ref, m)

    def finish_block(i, m):
        sums = jnp.concatenate([acc_ref[h, HEAD_DIM:HEAD_DIM + 1, :] for h in range(N_HEADS)], axis=1)
        finish(i, m, sums, acc_ref)

    load_queries(0)
    mx = lax.fori_loop(0, win_chunks, lambda c, mx: scores(c, 0, e_refs[0], mx), mx0)

    def step(i, slot, m):
        load_queries(i + 1)
        acc_ref[...] = jnp.zeros_like(acc_ref)

        def both(c, mx):
            parts = []
            for h in range(N_HEADS):
                parts.append(scores_head(c, i + 1, h, e_refs[1 - slot], mx[:, h * PAIR_LANES:(h + 1) * PAIR_LANES]))
                probs_head(c, i, h, e_refs[slot], m)
            return jnp.concatenate(parts, axis=1)

        mx = lax.fori_loop(0, win_chunks, both, mx0, unroll=True)
        finish_block(i, m)
        return jnp.max(mx, axis=0, keepdims=True)

    def two_steps(j, m):
        return step(2 * j + 1, 1, step(2 * j, 0, m))

    m = lax.fori_loop(0, n_blocks // 2 - 1, two_steps, jnp.max(mx, axis=0, keepdims=True))
    m = step(n_blocks - 2, 0, m)
    acc_ref[...] = jnp.zeros_like(acc_ref)

    def last(c, _):
        probs(c, n_blocks - 1, e_refs[1], m)
        return 0

    lax.fori_loop(0, win_chunks, last, 0)
    finish_block(n_blocks - 1, m)


def _attn_specs():
    def col(j):
        return pl.BlockSpec((1, SEQ, GROUP_W), lambda b: (b, 0, j))
    return [col(0), col(1), col(2), col(3)]


def _attn_scratch(n_pairs, key_chunk, win_chunks, tab_rows):
    tq = WIDTH // n_pairs
    return [pltpu.VMEM((N_HEADS, tab_rows, tq), F32),
            pltpu.VMEM((SEQ // key_chunk, N_HEADS * VT_ROWS, key_chunk), BF16),
            pltpu.VMEM((win_chunks * key_chunk, WIDTH), F32),
            pltpu.VMEM((win_chunks * key_chunk, WIDTH), F32),
            pltpu.VMEM((GROUP_W, WIDTH), BF16),
            pltpu.VMEM((N_HEADS, VT_ROWS, PAIR_LANES), F32)]


def _mask_values(v_ref, vm_ref):
    v = v_ref[0]
    for h in range(N_HEADS):
        vm_ref[h * SEQ:(h + 1) * SEQ, :] = jnp.where(_head_mask(h), v, jnp.zeros_like(v))


A_PAIRS = N_HEADS
A_TQ = WIDTH // A_PAIRS
A_CHUNK = 256
A_WIN = 3
A_FAR = DILATED_PATTERNS[-1][1]
A_NEAR_PATTERNS = DILATED_PATTERNS[:-1]
A_NEAR_REACH = max(r * (w // (2 * r)) for w, r in A_NEAR_PATTERNS)
assert A_NEAR_REACH <= A_CHUNK and A_TQ == A_CHUNK
A_TAB_OFF = 2 * A_CHUNK
A_TAB_ROWS = A_TAB_OFF + A_WIN * A_CHUNK
A_CLASS = SEQ // A_FAR
A_PART = GROUP_W + LANES


def _a_key_start(i):
    return jnp.clip(i - 1, 0, SEQ // A_CHUNK - A_WIN)


def _mixer_a_near_kernel(q_ref, k_ref, v_ref, part_ref, tab_ref, vt_ref, e0_ref, e1_ref, qt_ref, acc_ref, *, slopes):
    @pl.when(pl.program_id(0) == 0)
    def _():
        _build_toeplitz_t(tab_ref, slopes, A_NEAR_PATTERNS, A_TAB_OFF)

    def bias(h, i, c):
        t0 = pl.multiple_of((_a_key_start(i) + c - i) * A_CHUNK + A_TAB_OFF, A_CHUNK)
        return tab_ref[h, pl.ds(t0, A_CHUNK), :]

    def finish(i, m, sums, acc_ref):
        q0 = pl.multiple_of(i * A_TQ, A_TQ)
        ot = jnp.concatenate([acc_ref[h, 0:HEAD_DIM, :] for h in range(N_HEADS)], axis=0)
        stats = jnp.concatenate([m[:, h * A_TQ:(h + 1) * A_TQ] for h in range(N_HEADS)]
                                + [sums[:, h * A_TQ:(h + 1) * A_TQ] for h in range(N_HEADS)]
                                + [jnp.zeros((LANES - 2 * N_HEADS, A_TQ), F32)], axis=0)
        rec = jnp.concatenate([ot.T, stats.T], axis=1)
        for j in range(A_PART // LANES):
            part_ref[0, j, pl.ds(q0, A_TQ), :] = rec[:, j * LANES:(j + 1) * LANES]

    _attention_t(q_ref, k_ref, v_ref, vt_ref, (e0_ref, e1_ref), qt_ref, acc_ref, n_pairs=A_PAIRS,
                 key_chunk=A_CHUNK, win_chunks=A_WIN, key_start=_a_key_start, bias=bias, finish=finish)


def _mixer_a_far_kernel(p_ref, part_ref, o_ref, tab_ref, x_ref, y_ref, *, slopes):
    @pl.when(pl.program_id(0) == 0)
    def _():
        _build_toeplitz_t(tab_ref, slopes, DILATED_PATTERNS[-1:], 0, unit=A_FAR)

    def widen(i, _):
        r0 = pl.multiple_of(i * CONV_TILE, CONV_TILE)
        x = p_ref[0, pl.ds(r0, CONV_TILE), :].astype(F32)
        for j in range(4 * GROUP_W // LANES):
            x_ref[j, pl.ds(r0, CONV_TILE), :] = x[:, j * LANES:(j + 1) * LANES]
        return 0

    lax.fori_loop(0, SEQ // CONV_TILE, widen, 0)

    n = A_CLASS
    feat = lax.broadcasted_iota(jnp.int32, (GROUP_W, n), 0)
    ones_rows = jnp.where(lax.broadcasted_iota(jnp.int32, (VT_ROWS - HEAD_DIM, n), 0) == 0, 1.0, 0.0).astype(BF16)
    for rho in range(A_FAR):
        cls = pl.ds(rho, n, stride=A_FAR)
        tiles = GROUP_W // LANES
        rows_of = lambda ref, j0, nt: jnp.concatenate([ref[j, cls, :] for j in range(j0, j0 + nt)], axis=1)
        cols = lambda j: rows_of(x_ref, j * tiles, tiles)
        qt = cols(0).T
        qt_all = jnp.concatenate(
            [jnp.where((feat >= h * HEAD_DIM) & (feat < (h + 1) * HEAD_DIM), qt, 0.0) for h in range(N_HEADS)],
            axis=1).astype(BF16)
        s = _dot(cols(1).astype(BF16), qt_all)
        e = jnp.concatenate([s[:, h * n:(h + 1) * n] + tab_ref[h] for h in range(N_HEADS)], axis=1)
        m_far = jnp.max(e, axis=0, keepdims=True)
        pb = jnp.exp2(e - m_far).astype(BF16)
        vt = cols(2).T.astype(BF16)
        near = rows_of(part_ref.at[0], 0, tiles).T
        stats = part_ref[0, tiles, cls, :].T
        outs = []
        for h in range(N_HEADS):
            far = _dot(jnp.concatenate([vt[h * HEAD_DIM:(h + 1) * HEAD_DIM], ones_rows], axis=0),
                       pb[:, h * n:(h + 1) * n])
            m_near = stats[h:h + 1]
            m_h = m_far[:, h * n:(h + 1) * n]
            top = jnp.maximum(m_near, m_h)
            w_near = jnp.exp2(m_near - top)
            w_far = jnp.exp2(m_h - top)
            denom = stats[N_HEADS + h:N_HEADS + h + 1] * w_near + far[HEAD_DIM:HEAD_DIM + 1] * w_far
            outs.append((near[h * HEAD_DIM:(h + 1) * HEAD_DIM] * w_near + far[0:HEAD_DIM] * w_far) / denom)
        y = jnp.concatenate(outs, axis=0).T * _silu(cols(3))
        for j in range(tiles):
            y_ref[j, cls, :] = y[:, j * LANES:(j + 1) * LANES]

    def narrow(i, _):
        r0 = pl.multiple_of(i * CONV_TILE, CONV_TILE)
        o_ref[0, pl.ds(r0, CONV_TILE), :] = jnp.concatenate(
            [y_ref[j, pl.ds(r0, CONV_TILE), :] for j in range(GROUP_W // LANES)], axis=1).astype(BF16)
        return 0

    lax.fori_loop(0, SEQ // CONV_TILE, narrow, 0)


def _mixer_a(pa, slopes):
    bsz = pa.shape[0]
    col = lambda j: pl.BlockSpec((1, SEQ, GROUP_W), lambda b: (b, 0, j))
    part = pl.pallas_call(
        functools.partial(_mixer_a_near_kernel, slopes=slopes),
        grid=(bsz,),
        in_specs=[col(0), col(1), col(2)],
        out_specs=pl.BlockSpec((1, A_PART // LANES, SEQ, LANES), lambda b: (b, 0, 0, 0)),
        out_shape=jax.ShapeDtypeStruct((bsz, A_PART // LANES, SEQ, LANES), F32),
        scratch_shapes=_attn_scratch(A_PAIRS, A_CHUNK, A_WIN, A_TAB_ROWS),
        compiler_params=_params(1),
        name="mixer_a_near",
    )(pa, pa, pa)
    whole = lambda width: pl.BlockSpec((1, SEQ, width), lambda b: (b, 0, 0))
    return pl.pallas_call(
        functools.partial(_mixer_a_far_kernel, slopes=slopes),
        grid=(bsz,),
        in_specs=[whole(4 * GROUP_W), pl.BlockSpec((1, A_PART // LANES, SEQ, LANES), lambda b: (b, 0, 0, 0))],
        out_specs=whole(GROUP_W),
        out_shape=jax.ShapeDtypeStruct((bsz, SEQ, GROUP_W), BF16),
        scratch_shapes=[pltpu.VMEM((N_HEADS, A_CLASS, A_CLASS), F32),
                        pltpu.VMEM((4 * GROUP_W // LANES, SEQ, LANES), F32),
                        pltpu.VMEM((GROUP_W // LANES, SEQ, LANES), F32)],
        compiler_params=_params(1),
        name="mixer_a_far",
    )(pa, part)


B_PAIRS = 2 * N_HEADS
B_TQ = WIDTH // B_PAIRS


def _mixer_b_kernel(q_ref, k_ref, v_ref, g_ref, lam_ref, nw_ref, o_ref, tab_ref, vt_ref, e0_ref, e1_ref, qt_ref,
                    acc_ref, *, slopes, lam_init):
    tq = B_TQ

    @pl.when(pl.program_id(0) == 0)
    def _():
        _build_toeplitz_t(tab_ref, slopes, None, SEQ - tq)

    def bias(h, i, c):
        return tab_ref[h, pl.ds(pl.multiple_of(SEQ - tq - i * tq + c * KEY_CHUNK, LANES), KEY_CHUNK), :]

    lv = lam_ref[...]
    lam = (jnp.exp(jnp.sum(lv[0:1] * lv[1:2], axis=-1, keepdims=True))
           - jnp.exp(jnp.sum(lv[2:3] * lv[3:4], axis=-1, keepdims=True)) + lam_init)
    nw = nw_ref[...]

    def finish(i, m, sums, acc_ref):
        q0 = pl.multiple_of(i * tq, tq)
        inv = 1.0 / sums
        outs = []
        for h in range(N_HEADS):
            acc = acc_ref[h, 0:HEAD_DIM, :]
            o = (acc[:, 0:tq] * inv[:, 2 * h * tq:(2 * h + 1) * tq]
                 - lam * (acc[:, tq:2 * tq] * inv[:, (2 * h + 1) * tq:(2 * h + 2) * tq]))
            ms = jnp.mean(o * o, axis=0, keepdims=True)
            outs.append(o * lax.rsqrt(ms + EPS))
        on = jnp.concatenate(outs, axis=0).T * nw * (1.0 - lam_init)
        g = g_ref[0, pl.ds(q0, tq), :].astype(F32)
        o_ref[0, pl.ds(q0, tq), :] = (on * _silu(g)).astype(BF16)

    _attention_t(q_ref, k_ref, v_ref, vt_ref, (e0_ref, e1_ref), qt_ref, acc_ref, n_pairs=B_PAIRS,
                 key_chunk=KEY_CHUNK, win_chunks=SEQ // KEY_CHUNK, key_start=lambda i: 0, bias=bias, finish=finish)


def _mixer_b(pb, lam_p, nw256, slopes, lam_init):
    bsz = pb.shape[0]
    return pl.pallas_call(
        functools.partial(_mixer_b_kernel, slopes=slopes, lam_init=lam_init),
        grid=(bsz,),
        in_specs=_attn_specs() + [pl.BlockSpec((4, DIFF_HEAD_DIM), lambda b: (0, 0)),
                                  pl.BlockSpec((1, GROUP_W), lambda b: (0, 0))],
        out_specs=pl.BlockSpec((1, SEQ, GROUP_W), lambda b: (b, 0, 0)),
        out_shape=jax.ShapeDtypeStruct((bsz, SEQ, GROUP_W), BF16),
        scratch_shapes=_attn_scratch(B_PAIRS, KEY_CHUNK, SEQ // KEY_CHUNK, 2 * SEQ - B_TQ),
        compiler_params=_params(1),
        name="mixer_b_diff",
    )(pb, pb, pb, pb, lam_p, nw256)


CONV_TILE = 256
CONV_HALO = SUBLANES
N_CHUNK = SEQ // SSM_CHUNK


def _expand_heads(x, expand3):
    hi = x.astype(BF16)
    r1 = x - hi.astype(F32)
    mid = r1.astype(BF16)
    lo = (r1 - mid.astype(F32)).astype(BF16)
    return _dot(jnp.concatenate([hi, mid, lo], axis=1), expand3)


def _cumsum_rows(a):
    row = lax.broadcasted_iota(jnp.int32, a.shape, 0)
    s = 1
    while s < a.shape[0]:
        a = a + jnp.where(row >= s, pltpu.roll(a, s, 0), 0.0)
        s *= 2
    return a


def _mixer_c_kernel(p_ref, dt_ref, cw_ref, cb_ref, alog_ref, alogx_ref, dtb_ref, dskip_ref, nw_ref, exp_ref,
                    o_ref, xpad, xc, y_s, sb_s, cs_s, db_s):
    L = SSM_CHUNK
    zero_rows = jnp.zeros((CONV_HALO, SSM_XBC), F32)
    xpad[0:CONV_HALO, :] = zero_rows
    xpad[CONV_HALO + SEQ:CONV_HALO + SEQ + CONV_HALO, :] = zero_rows

    def fill(i, _):
        r0 = pl.multiple_of(i * CONV_TILE, CONV_TILE)
        xpad[pl.ds(CONV_HALO + r0, CONV_TILE), :] = p_ref[0, pl.ds(r0, CONV_TILE), GROUP_W:].astype(F32)
        return 0

    lax.fori_loop(0, SEQ // CONV_TILE, fill, 0)

    def conv(i, _):
        r0 = pl.multiple_of(i * CONV_TILE, CONV_TILE)
        rows = CONV_TILE + 2 * CONV_HALO
        win = xpad[pl.ds(r0, rows), :]
        acc = jnp.zeros((CONV_TILE, SSM_XBC), F32) + cb_ref[...]
        for j in range(SSM_CONV):
            back = (SSM_CONV // 2 - j) % rows
            tap = win if back == 0 else pltpu.roll(win, back, 0)
            acc = acc + cw_ref[j:j + 1, :] * tap[CONV_HALO:CONV_HALO + CONV_TILE, :]
        xc[pl.ds(r0, CONV_TILE), :] = _silu(acc)
        return 0

    lax.fori_loop(0, SEQ // CONV_TILE, conv, 0)

    a_neg = -jnp.exp(alog_ref[...])
    a_neg_x = -jnp.exp(alogx_ref[...])
    expand = exp_ref[...]
    li = lax.broadcasted_iota(jnp.int32, (L, L), 0)
    si = lax.broadcasted_iota(jnp.int32, (L, L), 1)
    lower = si <= li
    upper = si >= li
    hmasks = [_head_mask(h) for h in range(N_HEADS)]

    def chunk_terms(t0):
        dtr = dt_ref[0, pl.ds(t0, L), :] + dtb_ref[...]
        dt = jnp.maximum(dtr, 0.0) + jnp.log(1.0 + jnp.exp(-jnp.abs(dtr)))
        a = dt * a_neg
        ainc = _cumsum_rows(a)
        aexc = ainc - a
        return dt, ainc, aexc

    def fwd(c, hf):
        t0 = pl.multiple_of(c * L, L)
        dt, ainc, aexc = chunk_terms(t0)
        both_x = _expand_heads(jnp.concatenate([dt, ainc], axis=0), expand)
        dt_x = both_x[0:L]
        ainc_x = both_x[L:2 * L]
        aexc_x = ainc_x - dt_x * a_neg_x
        ainc_t = ainc.T
        aexc_t = aexc.T
        xs = xc[pl.ds(t0, L), 0:GROUP_W]
        bm = xc[pl.ds(t0, L), GROUP_W:2 * GROUP_W]
        cm = xc[pl.ds(t0, L), 2 * GROUP_W:3 * GROUP_W]
        xf = xs * dt_x[:, 0:GROUP_W]
        xb = xs * dt_x[:, GROUP_W:2 * GROUP_W]
        xcat = jnp.concatenate([xf, xb], axis=0).astype(BF16)
        tot_f = ainc_x[L - 1:L, 0:GROUP_W]
        tot_b = ainc_x[L - 1:L, GROUP_W:2 * GROUP_W]
        y = xs * dskip_ref[...]
        cbs = []
        for g in range(SSM_GROUPS):
            gs = slice(g * SSM_STATE, (g + 1) * SSM_STATE)
            cbs.append(_dot_nt(cm[:, gs].astype(BF16), bm[:, gs].astype(BF16)))
        for h in range(N_HEADS):
            cb = cbs[h // (N_HEADS // SSM_GROUPS)]
            col_f = ainc[:, h:h + 1]
            row_f = ainc_t[h:h + 1, :]
            col_b = aexc[:, N_HEADS + h:N_HEADS + h + 1]
            row_b = aexc_t[N_HEADS + h:N_HEADS + h + 1, :]
            lf = jnp.exp(jnp.where(lower, col_f - row_f, NEG))
            ub = jnp.exp(jnp.where(upper, row_b - col_b, NEG))
            mcat = jnp.concatenate([cb * lf, cb * ub], axis=1).astype(BF16)
            y = y + jnp.where(hmasks[h], _dot(mcat, xcat), 0.0)
        wf = (jnp.exp(tot_f - ainc_x[:, 0:GROUP_W]) * xf).astype(BF16)
        wb = (jnp.exp(aexc_x[:, GROUP_W:2 * GROUP_W]) * xb).astype(BF16)
        ef = jnp.exp(ainc_x[:, 0:GROUP_W])
        cs_s[pl.ds(t0, L), :] = jnp.exp(tot_b - aexc_x[:, GROUP_W:2 * GROUP_W])
        db_s[pl.ds(pl.multiple_of(c * SUBLANES, SUBLANES), SUBLANES), :] = jnp.broadcast_to(
            jnp.exp(tot_b), (SUBLANES, GROUP_W))
        dec_f = jnp.exp(tot_f)
        hf_new = []
        yoff = []
        for g in range(SSM_GROUPS):
            gs = slice(g * SSM_STATE, (g + 1) * SSM_STATE)
            bt = bm[:, gs].T.astype(BF16)
            yoff.append(_dot(cm[:, gs].astype(BF16), hf[g].astype(BF16)))
            hf_new.append(dec_f[:, gs] * hf[g] + _dot(bt, wf[:, gs]))
            sb_s[pl.ds(t0, L), gs] = _dot(bt, wb[:, gs])
        y = y + jnp.concatenate(yoff, axis=1) * ef
        y_s[pl.ds(t0, L), :] = y
        return tuple(hf_new)

    h0 = tuple(jnp.zeros((SSM_STATE, SSM_STATE), F32) for _ in range(SSM_GROUPS))
    lax.fori_loop(0, N_CHUNK, fwd, h0, unroll=2)

    def bwd(i, hb):
        c = N_CHUNK - 1 - i
        t0 = pl.multiple_of(c * L, L)
        cm = xc[pl.ds(t0, L), 2 * GROUP_W:3 * GROUP_W]
        dec_b = db_s[pl.ds(pl.multiple_of(c * SUBLANES, SUBLANES), 1), :]
        yoff = []
        hb_new = []
        for g in range(SSM_GROUPS):
            gs = slice(g * SSM_STATE, (g + 1) * SSM_STATE)
            yoff.append(_dot(cm[:, gs].astype(BF16), hb[g].astype(BF16)))
            hb_new.append(dec_b[:, gs] * hb[g] + sb_s[pl.ds(t0, L), gs])
        y_s[pl.ds(t0, L), :] = y_s[pl.ds(t0, L), :] + jnp.concatenate(yoff, axis=1) * cs_s[pl.ds(t0, L), :]
        return tuple(hb_new)

    lax.fori_loop(0, N_CHUNK, bwd, h0, unroll=4)

    def fin(i, _):
        r0 = pl.multiple_of(i * CONV_TILE, CONV_TILE)
        z = p_ref[0, pl.ds(r0, CONV_TILE), 0:GROUP_W].astype(F32)
        y = y_s[pl.ds(r0, CONV_TILE), :] * _silu(z)
        parts = []
        for g in range(SSM_GROUPS):
            yg = y[:, g * SSM_STATE:(g + 1) * SSM_STATE]
            parts.append(yg * lax.rsqrt(jnp.mean(yg * yg, axis=-1, keepdims=True) + EPS))
        o_ref[0, pl.ds(r0, CONV_TILE), :] = (jnp.concatenate(parts, axis=1) * nw_ref[...]).astype(BF16)
        return 0

    lax.fori_loop(0, SEQ // CONV_TILE, fin, 0)


def _head_expand_matrix():
    j = lax.broadcasted_iota(jnp.int32, (LANES, 2 * GROUP_W), 0)
    c = lax.broadcasted_iota(jnp.int32, (LANES, 2 * GROUP_W), 1)
    return jnp.tile((j == c // HEAD_DIM).astype(BF16), (3, 1))


def _mixer_c(pc, pdt, conv_w, conv_b, a_log, dt_bias, d_skip, norm_w):
    bsz = pc.shape[0]
    pad8 = lambda v: jnp.pad(v.reshape(1, 2 * N_HEADS), ((0, 0), (0, LANES - 2 * N_HEADS)))
    small = lambda shape: pl.BlockSpec(shape, lambda b: (0,) * len(shape))
    return pl.pallas_call(
        _mixer_c_kernel,
        grid=(bsz,),
        in_specs=[pl.BlockSpec((1, SEQ, 1024), lambda b: (b, 0, 0)),
                  pl.BlockSpec((1, SEQ, DT_PAD), lambda b: (b, 0, 0)),
                  small((SSM_CONV, SSM_XBC)), small((1, SSM_XBC)),
                  small((1, LANES)), small((1, 2 * GROUP_W)), small((1, LANES)),
                  small((1, GROUP_W)), small((1, GROUP_W)),
                  small((3 * LANES, 2 * GROUP_W))],
        out_specs=pl.BlockSpec((1, SEQ, GROUP_W), lambda b: (b, 0, 0)),
        out_shape=jax.ShapeDtypeStruct((bsz, SEQ, GROUP_W), BF16),
        scratch_shapes=[pltpu.VMEM((SEQ + 2 * CONV_HALO, SSM_XBC), F32),
                        pltpu.VMEM((SEQ, SSM_XBC), F32),
                        pltpu.VMEM((SEQ, GROUP_W), F32),
                        pltpu.VMEM((SEQ, GROUP_W), F32),
                        pltpu.VMEM((SEQ, GROUP_W), F32),
                        pltpu.VMEM((N_CHUNK * SUBLANES, GROUP_W), F32)],
        compiler_params=_params(1),
        name="mixer_c_ssd",
    )(pc, pdt, conv_w, conv_b.reshape(1, SSM_XBC), pad8(a_log),
      jnp.repeat(a_log.reshape(-1), HEAD_DIM).reshape(1, 2 * GROUP_W), pad8(dt_bias),
      jnp.repeat(d_skip, HEAD_DIM).reshape(1, GROUP_W), norm_w.reshape(1, GROUP_W),
      _head_expand_matrix())


NA_ROWS = SEQ // GRID_W
NA_DR = 2 * NA_WIN_H - 1
NA_DC = 2 * NA_WIN_W - 1
NA_PAIRS = NA_DR - 1


def _na_bias_kernel(rpb_ref, o_ref):
    lh = pl.program_id(0)
    cq = lax.broadcasted_iota(jnp.int32, (GRID_W, LANES), 0)
    lane = lax.broadcasted_iota(jnp.int32, (GRID_W, LANES), 1)
    second = lane >= GRID_W
    ck = jnp.where(second, lane - GRID_W, lane)
    cs = jnp.clip(cq - NA_WIN_W // 2, 0, GRID_W - NA_WIN_W)
    inside = (ck >= cs) & (ck < cs + NA_WIN_W)
    dc = ck - cq + NA_WIN_W - 1
    base = lh * (NA_DR * NA_DC)
    for p in range(NA_PAIRS):
        acc = jnp.zeros((GRID_W, LANES), F32)
        for j in range(NA_DC):
            lo = rpb_ref[base + p * NA_DC + j]
            hi = rpb_ref[base + (p + 1) * NA_DC + j]
            acc = acc + jnp.where(dc == j, jnp.where(second, hi, lo), 0.0)
        o_ref[0, p] = jnp.where(inside, acc, NEG)


def _na_bias(na_rpb):
    n = DEPTH * N_HEADS
    return pl.pallas_call(
        _na_bias_kernel,
        grid=(n,),
        in_specs=[pl.BlockSpec(memory_space=pltpu.SMEM)],
        out_specs=pl.BlockSpec((1, NA_PAIRS, GRID_W, LANES), lambda i: (i, 0, 0, 0)),
        out_shape=jax.ShapeDtypeStruct((n, NA_PAIRS, GRID_W, LANES), F32),
        compiler_params=_params(1),
        name="na_bias_table",
    )(na_rpb.reshape(-1))


NA_UNROLL = 4


def _mixer_d_kernel(q_ref, k_ref, v_ref, g_ref, bias_ref, o_ref, vm_ref):
    nkeys = NA_WIN_H * GRID_W
    _mask_values(v_ref, vm_ref)

    def one_row(r):
        rs = jnp.clip(r - NA_WIN_H // 2, 0, NA_ROWS - NA_WIN_H)
        delta = r - rs
        q0 = pl.multiple_of(r * GRID_W, GRID_W)
        k0 = pl.multiple_of(rs * GRID_W, GRID_W)
        q = q_ref[0, pl.ds(q0, GRID_W), :]
        qs = jnp.concatenate([jnp.where(_head_mask(h), q, jnp.zeros_like(q)) for h in range(N_HEADS)], axis=0)
        s_all = _dot_nt(qs, k_ref[0, pl.ds(k0, nkeys), :])
        ps, invs = [], []
        for h in range(N_HEADS):
            bias = jnp.concatenate(
                [bias_ref[h, 2 * i - delta + NA_WIN_H - 1] for i in range(NA_WIN_H // 2)], axis=-1)
            e = s_all[h * GRID_W:(h + 1) * GRID_W] + bias
            m = jnp.max(e, axis=-1, keepdims=True)
            p = jnp.exp(e - m)
            invs.append(1.0 / jnp.sum(p, axis=-1, keepdims=True))
            ps.append(p.astype(BF16))
        vcat = jnp.concatenate([vm_ref[pl.ds(pl.multiple_of(h * SEQ + k0, GRID_W), nkeys), :]
                                for h in range(N_HEADS)], axis=0)
        acc = _dot(jnp.concatenate(ps, axis=-1), vcat)
        inv = jnp.broadcast_to(invs[N_HEADS - 1], (GRID_W, GROUP_W))
        for h in range(N_HEADS - 2, -1, -1):
            inv = jnp.where(_head_mask(h), invs[h], inv)
        g = g_ref[0, pl.ds(q0, GRID_W), :].astype(F32)
        o_ref[0, pl.ds(q0, GRID_W), :] = (acc * inv * _silu(g)).astype(BF16)

    def rstep(i, _):
        for u in range(NA_UNROLL):
            one_row(i * NA_UNROLL + u)
        return 0

    lax.fori_loop(0, NA_ROWS // NA_UNROLL, rstep, 0)


def _mixer_d(pd, bias, l):
    bsz = pd.shape[0]
    return pl.pallas_call(
        _mixer_d_kernel,
        grid=(bsz,),
        in_specs=_attn_specs() + [pl.BlockSpec((N_HEADS, NA_PAIRS, GRID_W, LANES), lambda b: (l, 0, 0, 0))],
        out_specs=pl.BlockSpec((1, SEQ, GROUP_W), lambda b: (b, 0, 0)),
        out_shape=jax.ShapeDtypeStruct((bsz, SEQ, GROUP_W), BF16),
        scratch_shapes=[pltpu.VMEM((N_HEADS * SEQ, GROUP_W), BF16)],
        compiler_params=_params(1),
        name="mixer_d_neighbourhood",
    )(pd, pd, pd, pd, bias)


def kernel(x, c, norm_w, ada_w, ada_b, w_in, diff_lambda, diff_norm_w, conv_w, conv_b, ssm_a_log,
           ssm_dt_bias, ssm_d, ssm_norm_w, na_rpb, w_out, final_norm_w):
    bsz = x.shape[0]
    n = 2 * N_HEADS
    slopes = [2.0 ** (-8.0 * i / n) for i in range(1, n + 1)]
    slopes_a, slopes_b = tuple(slopes[0::2]), tuple(slopes[1::2])
    mod4 = _adaln(c, ada_w, ada_b).reshape(DEPTH, bsz, 1, 3 * D_MODEL)
    na_bias = _na_bias(na_rpb)
    w_main, w_dt = _prep_w_in(w_in)
    norm_w3 = norm_w.reshape(DEPTH, 1, D_MODEL)
    final_w = final_norm_w.reshape(1, D_MODEL)
    for l in range(DEPTH):
        pa, pb, pc, pd, pdt = _inproj(x, mod4, norm_w3, w_main, w_dt, l)
        lam_init = 0.8 - 0.6 * math.exp(-0.3 * l)
        ya = _mixer_a(pa, slopes_a)
        yb = _mixer_b(pb, diff_lambda[l], jnp.tile(diff_norm_w[l], N_HEADS).reshape(1, GROUP_W),
                      slopes_b, lam_init)
        yc = _mixer_c(pc, pdt, conv_w[l], conv_b[l], ssm_a_log[l], ssm_dt_bias[l], ssm_d[l], ssm_norm_w[l])
        yd = _mixer_d(pd, na_bias, l)
        x = _outproj((ya, yb, yc, yd), w_out, x, mod4, final_w, l)
    return x
```

```python
import functools
import math

import jax
import jax.numpy as jnp
from jax import lax
from jax.experimental import pallas as pl
from jax.experimental.pallas import tpu as pltpu

D_MODEL = 1024
SEQ = 2048
DEPTH = 2
HEAD_DIM = 64
GROUP_W = 256
N_HEADS = 4
EPS = 1e-6
DILATED_PATTERNS = ((128, 1), (512, 4), (2048, 16))
DIFF_HEAD_DIM = 32
SSM_GROUPS = 2
SSM_STATE = 128
SSM_CONV = 5
SSM_CHUNK = 128
SSM_XBC = 768
GRID_W = 64
NA_WIN_H = 8
NA_WIN_W = 16
D_IN = 13 * GROUP_W + SSM_XBC + 2 * N_HEADS

LANES = 128
SUBLANES = 8
VMEM_LIMIT = 56 * 1024 * 1024

NEG = -1e30
LOG2E = math.log2(math.e)
F32 = jnp.float32
BF16 = jnp.bfloat16
HIGHEST = lax.Precision.HIGHEST

ROW_TILE = 512
DT_PAD = LANES


def _silu(x):
    return x / (1.0 + jnp.exp(-x))


def _dot_nt(a, b):
    return lax.dot_general(a, b, (((1,), (1,)), ((), ())), preferred_element_type=F32)


def _dot(a, b):
    return jnp.dot(a, b, preferred_element_type=F32)


def _params(n_grid):
    return pltpu.CompilerParams(dimension_semantics=("arbitrary",) * n_grid,
                                vmem_limit_bytes=VMEM_LIMIT)


def _head_mask(h, width=HEAD_DIM, total=GROUP_W):
    lane = lax.broadcasted_iota(jnp.int32, (1, total), 1)
    return (lane >= h * width) & (lane < (h + 1) * width)


def _mod_kernel(c_ref, w_ref, b_ref, o_ref):
    c = c_ref[...]
    o_ref[0] = jnp.dot(_silu(c), w_ref[0], precision=HIGHEST,
                       preferred_element_type=F32) + b_ref[0]


def _adaln(c, ada_w, ada_b):
    bsz = c.shape[0]
    tn = 768
    return pl.pallas_call(
        _mod_kernel,
        grid=(DEPTH, 3 * D_MODEL // tn),
        in_specs=[pl.BlockSpec((bsz, D_MODEL), lambda l, j: (0, 0)),
                  pl.BlockSpec((1, D_MODEL, tn), lambda l, j: (l, 0, j)),
                  pl.BlockSpec((1, 1, tn), lambda l, j: (l, 0, j))],
        out_specs=pl.BlockSpec((1, bsz, tn), lambda l, j: (l, 0, j)),
        out_shape=jax.ShapeDtypeStruct((DEPTH, bsz, 3 * D_MODEL), F32),
        compiler_params=_params(2),
        name="adaln_mod",
    )(c, ada_w, ada_b.reshape(DEPTH, 1, 3 * D_MODEL))


W_MAIN = 16 * GROUP_W
DT_COL0 = 9 * GROUP_W + SSM_XBC
PREP_TILE = 512


def _prep_kernel(w_ref, wd_ref, wdt_ref, sc_ref, scdt_ref, om_ref, odt_ref):
    j = pl.program_id(1)

    @pl.when(j < DT_COL0 // PREP_TILE)
    def _():
        om_ref[0] = (w_ref[0] * sc_ref[...]).astype(BF16)

    @pl.when(j >= DT_COL0 // PREP_TILE)
    def _():
        om_ref[0] = (wd_ref[0] * sc_ref[...]).astype(BF16)

    @pl.when(j == 0)
    def _():
        odt_ref[0] = (wdt_ref[0] * scdt_ref[...]).astype(BF16)


def _prep_w_in(w_in):
    scale = [1.0] * W_MAIN
    for c0, s in ((0, HEAD_DIM ** -0.5 * LOG2E), (4 * GROUP_W, DIFF_HEAD_DIM ** -0.5 * LOG2E),
                  (12 * GROUP_W, HEAD_DIM ** -0.5)):
        scale[c0:c0 + GROUP_W] = [s] * GROUP_W
    sc = jnp.asarray(scale, F32).reshape(1, W_MAIN)
    scdt = jnp.asarray([1.0] * (2 * N_HEADS) + [0.0] * (DT_PAD - 2 * N_HEADS), F32).reshape(1, DT_PAD)
    w_lo = w_in[:, :, :DT_COL0]
    w_d = w_in[:, :, DT_COL0 + 2 * N_HEADS:]
    w_dt = jnp.pad(w_in[:, :, DT_COL0:DT_COL0 + 2 * N_HEADS], ((0, 0), (0, 0), (0, DT_PAD - 2 * N_HEADS)))
    n_lo = DT_COL0 // PREP_TILE
    return pl.pallas_call(
        _prep_kernel,
        grid=(DEPTH, W_MAIN // PREP_TILE),
        in_specs=[pl.BlockSpec((1, D_MODEL, PREP_TILE), lambda l, j: (l, 0, jnp.minimum(j, n_lo - 1))),
                  pl.BlockSpec((1, D_MODEL, PREP_TILE), lambda l, j: (l, 0, jnp.maximum(j - n_lo, 0))),
                  pl.BlockSpec((1, D_MODEL, DT_PAD), lambda l, j: (l, 0, 0)),
                  pl.BlockSpec((1, PREP_TILE), lambda l, j: (0, j)),
                  pl.BlockSpec((1, DT_PAD), lambda l, j: (0, 0))],
        out_specs=[pl.BlockSpec((1, D_MODEL, PREP_TILE), lambda l, j: (l, 0, j)),
                   pl.BlockSpec((1, D_MODEL, DT_PAD), lambda l, j: (l, 0, 0))],
        out_shape=[jax.ShapeDtypeStruct((DEPTH, D_MODEL, W_MAIN), BF16),
                   jax.ShapeDtypeStruct((DEPTH, D_MODEL, DT_PAD), BF16)],
        compiler_params=_params(2),
        name="prep_w_in",
    )(w_lo, w_d, w_dt, sc, scdt)


def _inproj_kernel(x_ref, mod_ref, nw_ref, w_ref, wdt_ref, pa_ref, pb_ref, pc_ref, pd_ref, pdt_ref):
    x = x_ref[0]
    shift = mod_ref[:, 0:D_MODEL]
    scale = mod_ref[:, D_MODEL:2 * D_MODEL]
    y = x * lax.rsqrt(jnp.mean(x * x, axis=-1, keepdims=True) + EPS) * nw_ref[...]
    h = (y * (1.0 + scale) + shift).astype(BF16)
    for i, ref in enumerate((pa_ref, pb_ref, pc_ref, pd_ref)):
        ref[0] = _dot(h, w_ref[:, i * 1024:(i + 1) * 1024]).astype(BF16)
    pdt_ref[0] = _dot(h, wdt_ref[...])


def _inproj(x, mod4, norm_w, w_main, w_dt, l):
    bsz = x.shape[0]
    row = lambda b, t: (b, t, 0)
    big = pl.BlockSpec((1, ROW_TILE, 1024), row)
    return pl.pallas_call(
        _inproj_kernel,
        grid=(bsz, SEQ // ROW_TILE),
        in_specs=[pl.BlockSpec((1, ROW_TILE, D_MODEL), row),
                  pl.BlockSpec((None, None, 1, 3 * D_MODEL), lambda b, t: (l, b, 0, 0)),
                  pl.BlockSpec((None, 1, D_MODEL), lambda b, t: (l, 0, 0)),
                  pl.BlockSpec((None, D_MODEL, W_MAIN), lambda b, t: (l, 0, 0)),
                  pl.BlockSpec((None, D_MODEL, DT_PAD), lambda b, t: (l, 0, 0))],
        out_specs=[big, big, big, big, pl.BlockSpec((1, ROW_TILE, DT_PAD), row)],
        out_shape=[jax.ShapeDtypeStruct((bsz, SEQ, 1024), BF16)] * 4
        + [jax.ShapeDtypeStruct((bsz, SEQ, DT_PAD), F32)],
        compiler_params=_params(2),
        name="inproj",
    )(x, mod4, norm_w, w_main, w_dt)


def _outproj_kernel(ya_ref, yb_ref, yc_ref, yd_ref, w_ref, x_ref, mod_ref, fw_ref, o_ref, wb_ref, *, final):
    @pl.when((pl.program_id(0) == 0) & (pl.program_id(1) == 0))
    def _():
        wb_ref[...] = w_ref[...].astype(BF16)

    acc = None
    for i, ref in enumerate((ya_ref, yb_ref, yc_ref, yd_ref)):
        part = _dot(ref[0], wb_ref[i * GROUP_W:(i + 1) * GROUP_W, :])
        acc = part if acc is None else acc + part
    gate = mod_ref[:, 2 * D_MODEL:3 * D_MODEL]
    xn = x_ref[0] + gate * acc
    if final:
        xn = xn * lax.rsqrt(jnp.mean(xn * xn, axis=-1, keepdims=True) + EPS) * fw_ref[...]
    o_ref[0] = xn


def _outproj(ys, w_out, x, mod4, final_w, l):
    bsz = x.shape[0]
    final = l == DEPTH - 1
    row = lambda b, t: (b, t, 0)
    yspec = pl.BlockSpec((1, ROW_TILE, GROUP_W), row)
    return pl.pallas_call(
        functools.partial(_outproj_kernel, final=final),
        grid=(bsz, SEQ // ROW_TILE),
        in_specs=[yspec, yspec, yspec, yspec,
                  pl.BlockSpec((None, D_MODEL, D_MODEL), lambda b, t: (l, 0, 0)),
                  pl.BlockSpec((1, ROW_TILE, D_MODEL), row),
                  pl.BlockSpec((None, None, 1, 3 * D_MODEL), lambda b, t: (l, b, 0, 0)),
                  pl.BlockSpec((1, D_MODEL), lambda b, t: (0, 0))],
        out_specs=pl.BlockSpec((1, ROW_TILE, D_MODEL), row),
        out_shape=jax.ShapeDtypeStruct((bsz, SEQ, D_MODEL), F32),
        scratch_shapes=[pltpu.VMEM((D_MODEL, D_MODEL), BF16)],
        compiler_params=_params(2),
        name="outproj_final" if final else "outproj",
    )(*ys, w_out, x, mod4, final_w)


KEY_CHUNK = 256
WIDTH = 1024
PAIR_LANES = WIDTH // N_HEADS
VT_ROWS = HEAD_DIM + 16


def _build_toeplitz_t(tab_ref, slopes, patterns, off, unit=1):
    _, rows, tq = tab_ref.shape
    step = min(rows, 512)
    i_io = lax.broadcasted_iota(jnp.int32, (step, tq), 0)
    r_io = lax.broadcasted_iota(jnp.int32, (step, tq), 1)
    base = (i_io - r_io - off) * unit

    def write(i0, n):
        d = base[:n] + i0 * unit
        ad = jnp.abs(d)
        adf = ad.astype(F32)
        logm, valid = 0.0, None
        if patterns is not None:
            mult = jnp.zeros(d.shape, F32)
            for w, r in patterns:
                reach = r * (w // (2 * r))
                mult = mult + jnp.where(ad <= reach, jnp.where((d & (r - 1)) == 0, 1.0, 0.0), 0.0)
            valid = mult > 0.5
            logm = jnp.log(jnp.maximum(mult, 1.0))
        for h, slope in enumerate(slopes):
            val = (logm - slope * adf) * LOG2E
            if valid is not None:
                val = jnp.where(valid, val, NEG)
            tab_ref[h, pl.ds(i0, n), :] = val

    def body(j, _):
        write(pl.multiple_of(j * step, step), step)
        return 0

    lax.fori_loop(0, rows // step, body, 0)
    if rows % step:
        write(rows - rows % step, rows % step)


def _sublane_groups(x):
    return x.reshape(x.shape[0] // SUBLANES, SUBLANES, x.shape[1])


def _attention_t(q_ref, k_ref, v_ref, vt_ref, e_refs, qt_ref, acc_ref, *, n_pairs, key_chunk, win_chunks,
                 key_start, bias, finish):
    tq = WIDTH // n_pairs
    fw = GROUP_W // n_pairs
    n_blocks = SEQ // tq
    ones_row = jnp.where(lax.broadcasted_iota(jnp.int32, (VT_ROWS - HEAD_DIM, key_chunk), 0) == 0, 1.0, 0.0)
    for c in range(SEQ // key_chunk):
        rows = slice(c * key_chunk, (c + 1) * key_chunk)
        vt = v_ref[0, rows, :].astype(F32).T.astype(BF16)
        for h in range(N_HEADS):
            vt_ref[c, h * VT_ROWS:h * VT_ROWS + HEAD_DIM, :] = vt[h * HEAD_DIM:(h + 1) * HEAD_DIM]
            vt_ref[c, h * VT_ROWS + HEAD_DIM:(h + 1) * VT_ROWS, :] = ones_row.astype(BF16)

    feat = lax.broadcasted_iota(jnp.int32, (GROUP_W, tq), 0)
    mx0 = jnp.full((SUBLANES, WIDTH), NEG, F32)

    def load_queries(i):
        qt = q_ref[0, pl.ds(pl.multiple_of(i * tq, tq), tq), :].astype(F32).T
        for p in range(n_pairs):
            keep = (feat >= p * fw) & (feat < (p + 1) * fw)
            qt_ref[:, p * tq:(p + 1) * tq] = jnp.where(keep, qt, 0.0).astype(BF16)

    pph = n_pairs // N_HEADS

    def scores_head(c, i, h, e_ref, mx):
        lanes = slice(h * PAIR_LANES, (h + 1) * PAIR_LANES)
        k0 = pl.multiple_of((key_start(i) + c) * key_chunk, key_chunk)
        s = _dot(k_ref[0, pl.ds(k0, key_chunk), :], qt_ref[:, lanes])
        b = bias(h, i, c)
        e = jnp.concatenate([s[:, p * tq:(p + 1) * tq] + b for p in range(pph)], axis=1)
        e_ref[pl.ds(pl.multiple_of(c * key_chunk, key_chunk), key_chunk), lanes] = e
        return jnp.maximum(mx, jnp.max(_sublane_groups(e), axis=0))

    def scores(c, i, e_ref, mx):
        parts = [scores_head(c, i, h, e_ref, mx[:, h * PAIR_LANES:(h + 1) * PAIR_LANES]) for h in range(N_HEADS)]
        return jnp.concatenate(parts, axis=1)

    def probs_head(c, i, h, e_ref, m):
        lanes = slice(h * PAIR_LANES, (h + 1) * PAIR_LANES)
        r0 = pl.multiple_of(c * key_chunk, key_chunk)
        pb = jnp.exp2(e_ref[pl.ds(r0, key_chunk), lanes] - m[:, lanes]).astype(BF16)
        acc_ref[h] += _dot(vt_ref[key_start(i) + c, h * VT_ROWS:(h + 1) * VT_ROWS, :], pb)

    def probs(c, i, e_ref, m):
        for h in range(N_HEADS):
            probs_head(c, i, h, e_ref, m)

    def finish_block(i, m):
        sums = jnp.concatenate([acc_ref[h, HEAD_DIM:HEAD_DIM + 1, :] for h in range(N_HEADS)], axis=1)
        finish(i, m, sums, acc_ref)

    load_queries(0)
    mx = lax.fori_loop(0, win_chunks, lambda c, mx: scores(c, 0, e_refs[0], mx), mx0, unroll=True)

    def step(i, slot, m):
        load_queries(i + 1)
        acc_ref[...] = jnp.zeros_like(acc_ref)

        def both(c, mx):
            parts = []
            for h in range(N_HEADS):
                parts.append(scores_head(c, i + 1, h, e_refs[1 - slot], mx[:, h * PAIR_LANES:(h + 1) * PAIR_LANES]))
                probs_head(c, i, h, e_refs[slot], m)
            return jnp.concatenate(parts, axis=1)

        mx = lax.fori_loop(0, win_chunks, both, mx0, unroll=True)
        finish_block(i, m)
        return jnp.max(mx, axis=0, keepdims=True)

    def two_steps(j, m):
        return step(2 * j + 1, 1, step(2 * j, 0, m))

    m = lax.fori_loop(0, n_blocks // 2 - 1, two_steps, jnp.max(mx, axis=0, keepdims=True))
    m = step(n_blocks - 2, 0, m)
    acc_ref[...] = jnp.zeros_like(acc_ref)

    def last(c, _):
        probs(c, n_blocks - 1, e_refs[1], m)
        return 0

    lax.fori_loop(0, win_chunks, last, 0, unroll=True)
    finish_block(n_blocks - 1, m)


def _attn_specs():
    def col(j):
        return pl.BlockSpec((1, SEQ, GROUP_W), lambda b: (b, 0, j))
    return [col(0), col(1), col(2), col(3)]


def _attn_scratch(n_pairs, key_chunk, win_chunks, tab_rows):
    tq = WIDTH // n_pairs
    return [pltpu.VMEM((N_HEADS, tab_rows, tq), F32),
            pltpu.VMEM((SEQ // key_chunk, N_HEADS * VT_ROWS, key_chunk), BF16),
            pltpu.VMEM((win_chunks * key_chunk, WIDTH), F32),
            pltpu.VMEM((win_chunks * key_chunk, WIDTH), F32),
            pltpu.VMEM((GROUP_W, WIDTH), BF16),
            pltpu.VMEM((N_HEADS, VT_ROWS, PAIR_LANES), F32)]


def _mask_values(v_ref, vm_ref):
    v = v_ref[0]
    for h in range(N_HEADS):
        vm_ref[h * SEQ:(h + 1) * SEQ, :] = jnp.where(_head_mask(h), v, jnp.zeros_like(v))


A_PAIRS = N_HEADS
A_TQ = WIDTH // A_PAIRS
A_CHUNK = 256
A_WIN = 3
A_FAR = DILATED_PATTERNS[-1][1]
A_NEAR_PATTERNS = DILATED_PATTERNS[:-1]
A_NEAR_REACH = max(r * (w // (2 * r)) for w, r in A_NEAR_PATTERNS)
assert A_NEAR_REACH <= A_CHUNK and A_TQ == A_CHUNK
A_TAB_OFF = 2 * A_CHUNK
A_TAB_ROWS = A_TAB_OFF + A_WIN * A_CHUNK
A_CLASS = SEQ // A_FAR
A_PART = GROUP_W + LANES


def _a_key_start(i):
    return jnp.clip(i - 1, 0, SEQ // A_CHUNK - A_WIN)


def _mixer_a_near_kernel(q_ref, k_ref, v_ref, part_ref, tab_ref, vt_ref, e0_ref, e1_ref, qt_ref, acc_ref, *, slopes):
    @pl.when(pl.program_id(0) == 0)
    def _():
        _build_toeplitz_t(tab_ref, slopes, A_NEAR_PATTERNS, A_TAB_OFF)

    def bias(h, i, c):
        t0 = pl.multiple_of((_a_key_start(i) + c - i) * A_CHUNK + A_TAB_OFF, A_CHUNK)
        return tab_ref[h, pl.ds(t0, A_CHUNK), :]

    def finish(i, m, sums, acc_ref):
        q0 = pl.multiple_of(i * A_TQ, A_TQ)
        ot = jnp.concatenate([acc_ref[h, 0:HEAD_DIM, :] for h in range(N_HEADS)], axis=0)
        stats = jnp.concatenate([m[:, h * A_TQ:(h + 1) * A_TQ] for h in range(N_HEADS)]
                                + [sums[:, h * A_TQ:(h + 1) * A_TQ] for h in range(N_HEADS)]
                                + [jnp.zeros((LANES - 2 * N_HEADS, A_TQ), F32)], axis=0)
        rec = jnp.concatenate([ot.T, stats.T], axis=1)
        for j in range(A_PART // LANES):
            part_ref[0, j, pl.ds(q0, A_TQ), :] = rec[:, j * LANES:(j + 1) * LANES]

    _attention_t(q_ref, k_ref, v_ref, vt_ref, (e0_ref, e1_ref), qt_ref, acc_ref, n_pairs=A_PAIRS,
                 key_chunk=A_CHUNK, win_chunks=A_WIN, key_start=_a_key_start, bias=bias, finish=finish)


def _mixer_a_far_kernel(p_ref, part_ref, o_ref, tab_ref, x_ref, y_ref, *, slopes):
    @pl.when(pl.program_id(0) == 0)
    def _():
        _build_toeplitz_t(tab_ref, slopes, DILATED_PATTERNS[-1:], 0, unit=A_FAR)

    def widen(i, _):
        r0 = pl.multiple_of(i * CONV_TILE, CONV_TILE)
        x = p_ref[0, pl.ds(r0, CONV_TILE), :].astype(F32)
        for j in range(4 * GROUP_W // LANES):
            x_ref[j, pl.ds(r0, CONV_TILE), :] = x[:, j * LANES:(j + 1) * LANES]
        return 0

    lax.fori_loop(0, SEQ // CONV_TILE, widen, 0)

    n = A_CLASS
    feat = lax.broadcasted_iota(jnp.int32, (GROUP_W, n), 0)
    ones_rows = jnp.where(lax.broadcasted_iota(jnp.int32, (VT_ROWS - HEAD_DIM, n), 0) == 0, 1.0, 0.0).astype(BF16)
    for rho in range(A_FAR):
        cls = pl.ds(rho, n, stride=A_FAR)
        tiles = GROUP_W // LANES
        rows_of = lambda ref, j0, nt: jnp.concatenate([ref[j, cls, :] for j in range(j0, j0 + nt)], axis=1)
        cols = lambda j: rows_of(x_ref, j * tiles, tiles)
        qt = cols(0).T
        qt_all = jnp.concatenate(
            [jnp.where((feat >= h * HEAD_DIM) & (feat < (h + 1) * HEAD_DIM), qt, 0.0) for h in range(N_HEADS)],
            axis=1).astype(BF16)
        s = _dot(cols(1).astype(BF16), qt_all)
        e = jnp.concatenate([s[:, h * n:(h + 1) * n] + tab_ref[h] for h in range(N_HEADS)], axis=1)
        m_far = jnp.max(e, axis=0, keepdims=True)
        pb = jnp.exp2(e - m_far).astype(BF16)
        vt = cols(2).T.astype(BF16)
        near = rows_of(part_ref.at[0], 0, tiles).T
        stats = part_ref[0, tiles, cls, :].T
        outs = []
        for h in range(N_HEADS):
            far = _dot(jnp.concatenate([vt[h * HEAD_DIM:(h + 1) * HEAD_DIM], ones_rows], axis=0),
                       pb[:, h * n:(h + 1) * n])
            m_near = stats[h:h + 1]
            m_h = m_far[:, h * n:(h + 1) * n]
            top = jnp.maximum(m_near, m_h)
            w_near = jnp.exp2(m_near - top)
            w_far = jnp.exp2(m_h - top)
            denom = stats[N_HEADS + h:N_HEADS + h + 1] * w_near + far[HEAD_DIM:HEAD_DIM + 1] * w_far
            outs.append((near[h * HEAD_DIM:(h + 1) * HEAD_DIM] * w_near + far[0:HEAD_DIM] * w_far) / denom)
        y = jnp.concatenate(outs, axis=0).T * _silu(cols(3))
        for j in range(tiles):
            y_ref[j, cls, :] = y[:, j * LANES:(j + 1) * LANES]

    def narrow(i, _):
        r0 = pl.multiple_of(i * CONV_TILE, CONV_TILE)
        o_ref[0, pl.ds(r0, CONV_TILE), :] = jnp.concatenate(
            [y_ref[j, pl.ds(r0, CONV_TILE), :] for j in range(GROUP_W // LANES)], axis=1).astype(BF16)
        return 0

    lax.fori_loop(0, SEQ // CONV_TILE, narrow, 0)


def _mixer_a(pa, slopes):
    bsz = pa.shape[0]
    col = lambda j: pl.BlockSpec((1, SEQ, GROUP_W), lambda b: (b, 0, j))
    part = pl.pallas_call(
        functools.partial(_mixer_a_near_kernel, slopes=slopes),
        grid=(bsz,),
        in_specs=[col(0), col(1), col(2)],
        out_specs=pl.BlockSpec((1, A_PART // LANES, SEQ, LANES), lambda b: (b, 0, 0, 0)),
        out_shape=jax.ShapeDtypeStruct((bsz, A_PART // LANES, SEQ, LANES), F32),
        scratch_shapes=_attn_scratch(A_PAIRS, A_CHUNK, A_WIN, A_TAB_ROWS),
        compiler_params=_params(1),
        name="mixer_a_near",
    )(pa, pa, pa)
    whole = lambda width: pl.BlockSpec((1, SEQ, width), lambda b: (b, 0, 0))
    return pl.pallas_call(
        functools.partial(_mixer_a_far_kernel, slopes=slopes),
        grid=(bsz,),
        in_specs=[whole(4 * GROUP_W), pl.BlockSpec((1, A_PART // LANES, SEQ, LANES), lambda b: (b, 0, 0, 0))],
        out_specs=whole(GROUP_W),
        out_shape=jax.ShapeDtypeStruct((bsz, SEQ, GROUP_W), BF16),
        scratch_shapes=[pltpu.VMEM((N_HEADS, A_CLASS, A_CLASS), F32),
                        pltpu.VMEM((4 * GROUP_W // LANES, SEQ, LANES), F32),
                        pltpu.VMEM((GROUP_W // LANES, SEQ, LANES), F32)],
        compiler_params=_params(1),
        name="mixer_a_far",
    )(pa, part)


B_PAIRS = 2 * N_HEADS
B_TQ = WIDTH // B_PAIRS


def _mixer_b_kernel(q_ref, k_ref, v_ref, g_ref, lam_ref, nw_ref, o_ref, tab_ref, vt_ref, e0_ref, e1_ref, qt_ref,
                    acc_ref, *, slopes, lam_init):
    tq = B_TQ

    @pl.when(pl.program_id(0) == 0)
    def _():
        _build_toeplitz_t(tab_ref, slopes, None, SEQ - tq)

    def bias(h, i, c):
        return tab_ref[h, pl.ds(pl.multiple_of(SEQ - tq - i * tq + c * KEY_CHUNK, LANES), KEY_CHUNK), :]

    lv = lam_ref[...]
    lam = (jnp.exp(jnp.sum(lv[0:1] * lv[1:2], axis=-1, keepdims=True))
           - jnp.exp(jnp.sum(lv[2:3] * lv[3:4], axis=-1, keepdims=True)) + lam_init)
    nw = nw_ref[...]

    def finish(i, m, sums, acc_ref):
        q0 = pl.multiple_of(i * tq, tq)
        inv = 1.0 / sums
        outs = []
        for h in range(N_HEADS):
            acc = acc_ref[h, 0:HEAD_DIM, :]
            o = (acc[:, 0:tq] * inv[:, 2 * h * tq:(2 * h + 1) * tq]
                 - lam * (acc[:, tq:2 * tq] * inv[:, (2 * h + 1) * tq:(2 * h + 2) * tq]))
            ms = jnp.mean(o * o, axis=0, keepdims=True)
            outs.append(o * lax.rsqrt(ms + EPS))
        on = jnp.concatenate(outs, axis=0).T * nw * (1.0 - lam_init)
        g = g_ref[0, pl.ds(q0, tq), :].astype(F32)
        o_ref[0, pl.ds(q0, tq), :] = (on * _silu(g)).astype(BF16)

    _attention_t(q_ref, k_ref, v_ref, vt_ref, (e0_ref, e1_ref), qt_ref, acc_ref, n_pairs=B_PAIRS,
                 key_chunk=KEY_CHUNK, win_chunks=SEQ // KEY_CHUNK, key_start=lambda i: 0, bias=bias, finish=finish)


def _mixer_b(pb, lam_p, nw256, slopes, lam_init):
    bsz = pb.shape[0]
    return pl.pallas_call(
        functools.partial(_mixer_b_kernel, slopes=slopes, lam_init=lam_init),
        grid=(bsz,),
        in_specs=_attn_specs() + [pl.BlockSpec((4, DIFF_HEAD_DIM), lambda b: (0, 0)),
                                  pl.BlockSpec((1, GROUP_W), lambda b: (0, 0))],
        out_specs=pl.BlockSpec((1, SEQ, GROUP_W), lambda b: (b, 0, 0)),
        out_shape=jax.ShapeDtypeStruct((bsz, SEQ, GROUP_W), BF16),
        scratch_shapes=_attn_scratch(B_PAIRS, KEY_CHUNK, SEQ // KEY_CHUNK, 2 * SEQ - B_TQ),
        compiler_params=_params(1),
        name="mixer_b_diff",
    )(pb, pb, pb, pb, lam_p, nw256)


CONV_TILE = 256
CONV_HALO = SUBLANES
N_CHUNK = SEQ // SSM_CHUNK


def _expand_heads(x, expand3):
    hi = x.astype(BF16)
    r1 = x - hi.astype(F32)
    mid = r1.astype(BF16)
    lo = (r1 - mid.astype(F32)).astype(BF16)
    return _dot(jnp.concatenate([hi, mid, lo], axis=1), expand3)


def _cumsum_rows(a):
    row = lax.broadcasted_iota(jnp.int32, a.shape, 0)
    s = 1
    while s < a.shape[0]:
        a = a + jnp.where(row >= s, pltpu.roll(a, s, 0), 0.0)
        s *= 2
    return a


def _mixer_c_kernel(p_ref, dt_ref, cw_ref, cb_ref, alog_ref, alogx_ref, dtb_ref, dskip_ref, nw_ref, exp_ref,
                    o_ref, xpad, xc, y_s, sb_s, cs_s, db_s):
    L = SSM_CHUNK
    zero_rows = jnp.zeros((CONV_HALO, SSM_XBC), F32)
    xpad[0:CONV_HALO, :] = zero_rows
    xpad[CONV_HALO + SEQ:CONV_HALO + SEQ + CONV_HALO, :] = zero_rows

    def fill(i, _):
        r0 = pl.multiple_of(i * CONV_TILE, CONV_TILE)
        xpad[pl.ds(CONV_HALO + r0, CONV_TILE), :] = p_ref[0, pl.ds(r0, CONV_TILE), GROUP_W:].astype(F32)
        return 0

    lax.fori_loop(0, SEQ // CONV_TILE, fill, 0)

    def conv(i, _):
        r0 = pl.multiple_of(i * CONV_TILE, CONV_TILE)
        rows = CONV_TILE + 2 * CONV_HALO
        win = xpad[pl.ds(r0, rows), :]
        acc = jnp.zeros((CONV_TILE, SSM_XBC), F32) + cb_ref[...]
        for j in range(SSM_CONV):
            back = (SSM_CONV // 2 - j) % rows
            tap = win if back == 0 else pltpu.roll(win, back, 0)
            acc = acc + cw_ref[j:j + 1, :] * tap[CONV_HALO:CONV_HALO + CONV_TILE, :]
        xc[pl.ds(r0, CONV_TILE), :] = _silu(acc)
        return 0

    lax.fori_loop(0, SEQ // CONV_TILE, conv, 0)

    a_neg = -jnp.exp(alog_ref[...])
    a_neg_x = -jnp.exp(alogx_ref[...])
    expand = exp_ref[...]
    li = lax.broadcasted_iota(jnp.int32, (L, L), 0)
    si = lax.broadcasted_iota(jnp.int32, (L, L), 1)
    lower = si <= li
    upper = si >= li
    hmasks = [_head_mask(h) for h in range(N_HEADS)]

    def chunk_terms(t0):
        dtr = dt_ref[0, pl.ds(t0, L), :] + dtb_ref[...]
        dt = jnp.maximum(dtr, 0.0) + jnp.log(1.0 + jnp.exp(-jnp.abs(dtr)))
        a = dt * a_neg
        ainc = _cumsum_rows(a)
        aexc = ainc - a
        return dt, ainc, aexc

    def fwd(c, hf):
        t0 = pl.multiple_of(c * L, L)
        dt, ainc, aexc = chunk_terms(t0)
        both_x = _expand_heads(jnp.concatenate([dt, ainc], axis=0), expand)
        dt_x = both_x[0:L]
        ainc_x = both_x[L:2 * L]
        aexc_x = ainc_x - dt_x * a_neg_x
        ainc_t = ainc.T
        aexc_t = aexc.T
        xs = xc[pl.ds(t0, L), 0:GROUP_W]
        bm = xc[pl.ds(t0, L), GROUP_W:2 * GROUP_W]
        cm = xc[pl.ds(t0, L), 2 * GROUP_W:3 * GROUP_W]
        xf = xs * dt_x[:, 0:GROUP_W]
        xb = xs * dt_x[:, GROUP_W:2 * GROUP_W]
        xcat = jnp.concatenate([xf, xb], axis=0).astype(BF16)
        tot_f = ainc_x[L - 1:L, 0:GROUP_W]
        tot_b = ainc_x[L - 1:L, GROUP_W:2 * GROUP_W]
        y = xs * dskip_ref[...]
        cbs = []
        for g in range(SSM_GROUPS):
            gs = slice(g * SSM_STATE, (g + 1) * SSM_STATE)
            cbs.append(_dot_nt(cm[:, gs].astype(BF16), bm[:, gs].astype(BF16)))
        for h in range(N_HEADS):
            cb = cbs[h // (N_HEADS // SSM_GROUPS)]
            col_f = ainc[:, h:h + 1]
            row_f = ainc_t[h:h + 1, :]
            col_b = aexc[:, N_HEADS + h:N_HEADS + h + 1]
            row_b = aexc_t[N_HEADS + h:N_HEADS + h + 1, :]
            lf = jnp.exp(jnp.where(lower, col_f - row_f, NEG))
            ub = jnp.exp(jnp.where(upper, row_b - col_b, NEG))
            mcat = jnp.concatenate([cb * lf, cb * ub], axis=1).astype(BF16)
            y = y + jnp.where(hmasks[h], _dot(mcat, xcat), 0.0)
        wf = (jnp.exp(tot_f - ainc_x[:, 0:GROUP_W]) * xf).astype(BF16)
        wb = (jnp.exp(aexc_x[:, GROUP_W:2 * GROUP_W]) * xb).astype(BF16)
        ef = jnp.exp(ainc_x[:, 0:GROUP_W])
        cs_s[pl.ds(t0, L), :] = jnp.exp(tot_b - aexc_x[:, GROUP_W:2 * GROUP_W])
        db_s[pl.ds(pl.multiple_of(c * SUBLANES, SUBLANES), SUBLANES), :] = jnp.broadcast_to(
            jnp.exp(tot_b), (SUBLANES, GROUP_W))
        dec_f = jnp.exp(tot_f)
        hf_new = []
        yoff = []
        for g in range(SSM_GROUPS):
            gs = slice(g * SSM_STATE, (g + 1) * SSM_STATE)
            bt = bm[:, gs].T.astype(BF16)
            yoff.append(_dot(cm[:, gs].astype(BF16), hf[g].astype(BF16)))
            hf_new.append(dec_f[:, gs] * hf[g] + _dot(bt, wf[:, gs]))
            sb_s[pl.ds(t0, L), gs] = _dot(bt, wb[:, gs])
        y = y + jnp.concatenate(yoff, axis=1) * ef
        y_s[pl.ds(t0, L), :] = y
        return tuple(hf_new)

    h0 = tuple(jnp.zeros((SSM_STATE, SSM_STATE), F32) for _ in range(SSM_GROUPS))
    lax.fori_loop(0, N_CHUNK, fwd, h0, unroll=2)

    def bwd(i, hb):
        c = N_CHUNK - 1 - i
        t0 = pl.multiple_of(c * L, L)
        cm = xc[pl.ds(t0, L), 2 * GROUP_W:3 * GROUP_W]
        dec_b = db_s[pl.ds(pl.multiple_of(c * SUBLANES, SUBLANES), 1), :]
        yoff = []
        hb_new = []
        for g in range(SSM_GROUPS):
            gs = slice(g * SSM_STATE, (g + 1) * SSM_STATE)
            yoff.append(_dot(cm[:, gs].astype(BF16), hb[g].astype(BF16)))
            hb_new.append(dec_b[:, gs] * hb[g] + sb_s[pl.ds(t0, L), gs])
        y_s[pl.ds(t0, L), :] = y_s[pl.ds(t0, L), :] + jnp.concatenate(yoff, axis=1) * cs_s[pl.ds(t0, L), :]
        return tuple(hb_new)

    lax.fori_loop(0, N_CHUNK, bwd, h0, unroll=4)

    def fin(i, _):
        r0 = pl.multiple_of(i * CONV_TILE, CONV_TILE)
        z = p_ref[0, pl.ds(r0, CONV_TILE), 0:GROUP_W].astype(F32)
        y = y_s[pl.ds(r0, CONV_TILE), :] * _silu(z)
        parts = []
        for g in range(SSM_GROUPS):
            yg = y[:, g * SSM_STATE:(g + 1) * SSM_STATE]
            parts.append(yg * lax.rsqrt(jnp.mean(yg * yg, axis=-1, keepdims=True) + EPS))
        o_ref[0, pl.ds(r0, CONV_TILE), :] = (jnp.concatenate(parts, axis=1) * nw_ref[...]).astype(BF16)
        return 0

    lax.fori_loop(0, SEQ // CONV_TILE, fin, 0)


def _head_expand_matrix():
    j = lax.broadcasted_iota(jnp.int32, (LANES, 2 * GROUP_W), 0)
    c = lax.broadcasted_iota(jnp.int32, (LANES, 2 * GROUP_W), 1)
    return jnp.tile((j == c // HEAD_DIM).astype(BF16), (3, 1))


def _mixer_c(pc, pdt, conv_w, conv_b, a_log, dt_bias, d_skip, norm_w):
    bsz = pc.shape[0]
    pad8 = lambda v: jnp.pad(v.reshape(1, 2 * N_HEADS), ((0, 0), (0, LANES - 2 * N_HEADS)))
    small = lambda shape: pl.BlockSpec(shape, lambda b: (0,) * len(shape))
    return pl.pallas_call(
        _mixer_c_kernel,
        grid=(bsz,),
        in_specs=[pl.BlockSpec((1, SEQ, 1024), lambda b: (b, 0, 0)),
                  pl.BlockSpec((1, SEQ, DT_PAD), lambda b: (b, 0, 0)),
                  small((SSM_CONV, SSM_XBC)), small((1, SSM_XBC)),
                  small((1, LANES)), small((1, 2 * GROUP_W)), small((1, LANES)),
                  small((1, GROUP_W)), small((1, GROUP_W)),
                  small((3 * LANES, 2 * GROUP_W))],
        out_specs=pl.BlockSpec((1, SEQ, GROUP_W), lambda b: (b, 0, 0)),
        out_shape=jax.ShapeDtypeStruct((bsz, SEQ, GROUP_W), BF16),
        scratch_shapes=[pltpu.VMEM((SEQ + 2 * CONV_HALO, SSM_XBC), F32),
                        pltpu.VMEM((SEQ, SSM_XBC), F32),
                        pltpu.VMEM((SEQ, GROUP_W), F32),
                        pltpu.VMEM((SEQ, GROUP_W), F32),
                        pltpu.VMEM((SEQ, GROUP_W), F32),
                        pltpu.VMEM((N_CHUNK * SUBLANES, GROUP_W), F32)],
        compiler_params=_params(1),
        name="mixer_c_ssd",
    )(pc, pdt, conv_w, conv_b.reshape(1, SSM_XBC), pad8(a_log),
      jnp.repeat(a_log.reshape(-1), HEAD_DIM).reshape(1, 2 * GROUP_W), pad8(dt_bias),
      jnp.repeat(d_skip, HEAD_DIM).reshape(1, GROUP_W), norm_w.reshape(1, GROUP_W),
      _head_expand_matrix())


NA_ROWS = SEQ // GRID_W
NA_DR = 2 * NA_WIN_H - 1
NA_DC = 2 * NA_WIN_W - 1
NA_PAIRS = NA_DR - 1


def _na_bias_kernel(rpb_ref, o_ref):
    lh = pl.program_id(0)
    cq = lax.broadcasted_iota(jnp.int32, (GRID_W, LANES), 0)
    lane = lax.broadcasted_iota(jnp.int32, (GRID_W, LANES), 1)
    second = lane >= GRID_W
    ck = jnp.where(second, lane - GRID_W, lane)
    cs = jnp.clip(cq - NA_WIN_W // 2, 0, GRID_W - NA_WIN_W)
    inside = (ck >= cs) & (ck < cs + NA_WIN_W)
    dc = ck - cq + NA_WIN_W - 1
    base = lh * (NA_DR * NA_DC)
    rows = []
    for dr in range(NA_DR):
        acc = jnp.zeros((GRID_W, LANES), F32)
        for j in range(NA_DC):
            acc = jnp.where(dc == j, rpb_ref[base + dr * NA_DC + j], acc)
        rows.append(acc)
    for p in range(NA_PAIRS):
        o_ref[0, p] = jnp.where(inside, jnp.where(second, rows[p + 1], rows[p]), NEG)


def _na_bias(na_rpb):
    n = DEPTH * N_HEADS
    return pl.pallas_call(
        _na_bias_kernel,
        grid=(n,),
        in_specs=[pl.BlockSpec(memory_space=pltpu.SMEM)],
        out_specs=pl.BlockSpec((1, NA_PAIRS, GRID_W, LANES), lambda i: (i, 0, 0, 0)),
        out_shape=jax.ShapeDtypeStruct((n, NA_PAIRS, GRID_W, LANES), F32),
        compiler_params=_params(1),
        name="na_bias_table",
    )(na_rpb.reshape(-1))


NA_UNROLL = 8


def _mixer_d_kernel(q_ref, k_ref, v_ref, g_ref, bias_ref, o_ref, vm_ref):
    nkeys = NA_WIN_H * GRID_W
    _mask_values(v_ref, vm_ref)

    def score_row(r):
        rs = jnp.clip(r - NA_WIN_H // 2, 0, NA_ROWS - NA_WIN_H)
        k0 = pl.multiple_of(rs * GRID_W, GRID_W)
        q = q_ref[0, pl.ds(pl.multiple_of(r * GRID_W, GRID_W), GRID_W), :]
        qs = jnp.concatenate([jnp.where(_head_mask(h), q, jnp.zeros_like(q)) for h in range(N_HEADS)], axis=0)
        return _dot_nt(qs, k_ref[0, pl.ds(k0, nkeys), :])

    def prob_row(r, s_all):
        delta = r - jnp.clip(r - NA_WIN_H // 2, 0, NA_ROWS - NA_WIN_H)
        ps, invs = [], []
        for h in range(N_HEADS):
            bias = jnp.concatenate(
                [bias_ref[h, 2 * i - delta + NA_WIN_H - 1] for i in range(NA_WIN_H // 2)], axis=-1)
            e = s_all[h * GRID_W:(h + 1) * GRID_W] + bias
            m = jnp.max(e, axis=-1, keepdims=True)
            p = jnp.exp(e - m)
            invs.append(1.0 / jnp.sum(p, axis=-1, keepdims=True))
            ps.append(p.astype(BF16))
        inv = jnp.broadcast_to(invs[N_HEADS - 1], (GRID_W, GROUP_W))
        for h in range(N_HEADS - 2, -1, -1):
            inv = jnp.where(_head_mask(h), invs[h], inv)
        return jnp.concatenate(ps, axis=-1), inv

    def value_row(r, pcat, inv):
        rs = jnp.clip(r - NA_WIN_H // 2, 0, NA_ROWS - NA_WIN_H)
        k0 = pl.multiple_of(rs * GRID_W, GRID_W)
        q0 = pl.multiple_of(r * GRID_W, GRID_W)
        vcat = jnp.concatenate([vm_ref[pl.ds(pl.multiple_of(h * SEQ + k0, GRID_W), nkeys), :]
                                for h in range(N_HEADS)], axis=0)
        acc = _dot(pcat, vcat)
        g = g_ref[0, pl.ds(q0, GRID_W), :].astype(F32)
        o_ref[0, pl.ds(q0, GRID_W), :] = (acc * inv * _silu(g)).astype(BF16)

    def rstep(i, _):
        rows = [i * NA_UNROLL + u for u in range(NA_UNROLL)]
        s_next = score_row(rows[0])
        for u, r in enumerate(rows):
            s_cur = s_next
            if u + 1 < NA_UNROLL:
                s_next = score_row(rows[u + 1])
            value_row(r, *prob_row(r, s_cur))
        return 0

    lax.fori_loop(0, NA_ROWS // NA_UNROLL, rstep, 0)


def _mixer_d(pd, bias, l):
    bsz = pd.shape[0]
    return pl.pallas_call(
        _mixer_d_kernel,
        grid=(bsz,),
        in_specs=_attn_specs() + [pl.BlockSpec((N_HEADS, NA_PAIRS, GRID_W, LANES), lambda b: (l, 0, 0, 0))],
        out_specs=pl.BlockSpec((1, SEQ, GROUP_W), lambda b: (b, 0, 0)),
        out_shape=jax.ShapeDtypeStruct((bsz, SEQ, GROUP_W), BF16),
        scratch_shapes=[pltpu.VMEM((N_HEADS * SEQ, GROUP_W), BF16)],
        compiler_params=_params(1),
        name="mixer_d_neighbourhood",
    )(pd, pd, pd, pd, bias)


def kernel(x, c, norm_w, ada_w, ada_b, w_in, diff_lambda, diff_norm_w, conv_w, conv_b, ssm_a_log,
           ssm_dt_bias, ssm_d, ssm_norm_w, na_rpb, w_out, final_norm_w):
    bsz = x.shape[0]
    n = 2 * N_HEADS
    slopes = [2.0 ** (-8.0 * i / n) for i in range(1, n + 1)]
    slopes_a, slopes_b = tuple(slopes[0::2]), tuple(slopes[1::2])
    mod4 = _adaln(c, ada_w, ada_b).reshape(DEPTH, bsz, 1, 3 * D_MODEL)
    na_bias = _na_bias(na_rpb)
    w_main, w_dt = _prep_w_in(w_in)
    norm_w3 = norm_w.reshape(DEPTH, 1, D_MODEL)
    final_w = final_norm_w.reshape(1, D_MODEL)
    for l in range(DEPTH):
        pa, pb, pc, pd, pdt = _inproj(x, mod4, norm_w3, w_main, w_dt, l)
        lam_init = 0.8 - 0.6 * math.exp(-0.3 * l)
        ya = _mixer_a(pa, slopes_a)
        yb = _mixer_b(pb, diff_lambda[l], jnp.tile(diff_norm_w[l], N_HEADS).reshape(1, GROUP_W),
                      slopes_b, lam_init)
        yc = _mixer_c(pc, pdt, conv_w[l], conv_b[l], ssm_a_log[l], ssm_dt_bias[l], ssm_d[l], ssm_norm_w[l])
        yd = _mixer_d(pd, na_bias, l)
        x = _outproj((ya, yb, yc, yd), w_out, x, mod4, final_w, l)
    return x
```

```python
import functools
import math

import jax
import jax.numpy as jnp
from jax import lax
from jax.experimental import pallas as pl
from jax.experimental.pallas import tpu as pltpu

D_MODEL = 1024
SEQ = 2048
DEPTH = 2
HEAD_DIM = 64
GROUP_W = 256
N_HEADS = 4
EPS = 1e-6
DILATED_PATTERNS = ((128, 1), (512, 4), (2048, 16))
DIFF_HEAD_DIM = 32
SSM_GROUPS = 2
SSM_STATE = 128
SSM_CONV = 5
SSM_CHUNK = 128
SSM_XBC = 768
GRID_W = 64
NA_WIN_H = 8
NA_WIN_W = 16
D_IN = 13 * GROUP_W + SSM_XBC + 2 * N_HEADS

LANES = 128
SUBLANES = 8
VMEM_LIMIT = 56 * 1024 * 1024

NEG = -1e30
LOG2E = math.log2(math.e)
F32 = jnp.float32
BF16 = jnp.bfloat16
HIGHEST = lax.Precision.HIGHEST

ROW_TILE = 512
DT_PAD = LANES


def _silu(x):
    return x / (1.0 + jnp.exp(-x))


def _dot_nt(a, b):
    return lax.dot_general(a, b, (((1,), (1,)), ((), ())), preferred_element_type=F32)


def _dot(a, b):
    return jnp.dot(a, b, preferred_element_type=F32)


def _params(n_grid):
    return pltpu.CompilerParams(dimension_semantics=("arbitrary",) * n_grid,
                                vmem_limit_bytes=VMEM_LIMIT)


def _head_mask(h, width=HEAD_DIM, total=GROUP_W):
    lane = lax.broadcasted_iota(jnp.int32, (1, total), 1)
    return (lane >= h * width) & (lane < (h + 1) * width)


def _mod_kernel(c_ref, w_ref, b_ref, o_ref):
    c = c_ref[...]
    o_ref[0] = jnp.dot(_silu(c), w_ref[0], precision=HIGHEST,
                       preferred_element_type=F32) + b_ref[0]


def _adaln(c, ada_w, ada_b):
    bsz = c.shape[0]
    tn = 768
    return pl.pallas_call(
        _mod_kernel,
        grid=(DEPTH, 3 * D_MODEL // tn),
        in_specs=[pl.BlockSpec((bsz, D_MODEL), lambda l, j: (0, 0)),
                  pl.BlockSpec((1, D_MODEL, tn), lambda l, j: (l, 0, j)),
                  pl.BlockSpec((1, 1, tn), lambda l, j: (l, 0, j))],
        out_specs=pl.BlockSpec((1, bsz, tn), lambda l, j: (l, 0, j)),
        out_shape=jax.ShapeDtypeStruct((DEPTH, bsz, 3 * D_MODEL), F32),
        compiler_params=_params(2),
        name="adaln_mod",
    )(c, ada_w, ada_b.reshape(DEPTH, 1, 3 * D_MODEL))


W_MAIN = 16 * GROUP_W
DT_COL0 = 9 * GROUP_W + SSM_XBC
PREP_TILE = 512


def _prep_kernel(w_ref, wd_ref, wdt_ref, sc_ref, scdt_ref, om_ref, odt_ref):
    j = pl.program_id(1)

    @pl.when(j < DT_COL0 // PREP_TILE)
    def _():
        om_ref[0] = (w_ref[0] * sc_ref[...]).astype(BF16)

    @pl.when(j >= DT_COL0 // PREP_TILE)
    def _():
        om_ref[0] = (wd_ref[0] * sc_ref[...]).astype(BF16)

    @pl.when(j == 0)
    def _():
        odt_ref[0] = (wdt_ref[0] * scdt_ref[...]).astype(BF16)


def _prep_w_in(w_in):
    scale = [1.0] * W_MAIN
    for c0, s in ((0, HEAD_DIM ** -0.5 * LOG2E), (4 * GROUP_W, DIFF_HEAD_DIM ** -0.5 * LOG2E),
                  (12 * GROUP_W, HEAD_DIM ** -0.5)):
        scale[c0:c0 + GROUP_W] = [s] * GROUP_W
    sc = jnp.asarray(scale, F32).reshape(1, W_MAIN)
    scdt = jnp.asarray([1.0] * (2 * N_HEADS) + [0.0] * (DT_PAD - 2 * N_HEADS), F32).reshape(1, DT_PAD)
    w_lo = w_in[:, :, :DT_COL0]
    w_d = w_in[:, :, DT_COL0 + 2 * N_HEADS:]
    w_dt = jnp.pad(w_in[:, :, DT_COL0:DT_COL0 + 2 * N_HEADS], ((0, 0), (0, 0), (0, DT_PAD - 2 * N_HEADS)))
    n_lo = DT_COL0 // PREP_TILE
    return pl.pallas_call(
        _prep_kernel,
        grid=(DEPTH, W_MAIN // PREP_TILE),
        in_specs=[pl.BlockSpec((1, D_MODEL, PREP_TILE), lambda l, j: (l, 0, jnp.minimum(j, n_lo - 1))),
                  pl.BlockSpec((1, D_MODEL, PREP_TILE), lambda l, j: (l, 0, jnp.maximum(j - n_lo, 0))),
                  pl.BlockSpec((1, D_MODEL, DT_PAD), lambda l, j: (l, 0, 0)),
                  pl.BlockSpec((1, PREP_TILE), lambda l, j: (0, j)),
                  pl.BlockSpec((1, DT_PAD), lambda l, j: (0, 0))],
        out_specs=[pl.BlockSpec((1, D_MODEL, PREP_TILE), lambda l, j: (l, 0, j)),
                   pl.BlockSpec((1, D_MODEL, DT_PAD), lambda l, j: (l, 0, 0))],
        out_shape=[jax.ShapeDtypeStruct((DEPTH, D_MODEL, W_MAIN), BF16),
                   jax.ShapeDtypeStruct((DEPTH, D_MODEL, DT_PAD), BF16)],
        compiler_params=_params(2),
        name="prep_w_in",
    )(w_lo, w_d, w_dt, sc, scdt)


def _inproj_kernel(x_ref, mod_ref, nw_ref, w_ref, wdt_ref, pa_ref, pb_ref, pc_ref, pd_ref, pdt_ref):
    x = x_ref[0]
    shift = mod_ref[:, 0:D_MODEL]
    scale = mod_ref[:, D_MODEL:2 * D_MODEL]
    y = x * lax.rsqrt(jnp.mean(x * x, axis=-1, keepdims=True) + EPS) * nw_ref[...]
    h = (y * (1.0 + scale) + shift).astype(BF16)
    for i, ref in enumerate((pa_ref, pb_ref, pc_ref, pd_ref)):
        ref[0] = _dot(h, w_ref[:, i * 1024:(i + 1) * 1024]).astype(BF16)
    pdt_ref[0] = _dot(h, wdt_ref[...])


def _inproj(x, mod4, norm_w, w_main, w_dt, l):
    bsz = x.shape[0]
    row = lambda b, t: (b, t, 0)
    big = pl.BlockSpec((1, ROW_TILE, 1024), row)
    return pl.pallas_call(
        _inproj_kernel,
        grid=(bsz, SEQ // ROW_TILE),
        in_specs=[pl.BlockSpec((1, ROW_TILE, D_MODEL), row),
                  pl.BlockSpec((None, None, 1, 3 * D_MODEL), lambda b, t: (l, b, 0, 0)),
                  pl.BlockSpec((None, 1, D_MODEL), lambda b, t: (l, 0, 0)),
                  pl.BlockSpec((None, D_MODEL, W_MAIN), lambda b, t: (l, 0, 0)),
                  pl.BlockSpec((None, D_MODEL, DT_PAD), lambda b, t: (l, 0, 0))],
        out_specs=[big, big, big, big, pl.BlockSpec((1, ROW_TILE, DT_PAD), row)],
        out_shape=[jax.ShapeDtypeStruct((bsz, SEQ, 1024), BF16)] * 4
        + [jax.ShapeDtypeStruct((bsz, SEQ, DT_PAD), F32)],
        compiler_params=_params(2),
        name="inproj",
    )(x, mod4, norm_w, w_main, w_dt)


def _outproj_kernel(ya_ref, yb_ref, yc_ref, yd_ref, w_ref, x_ref, mod_ref, fw_ref, o_ref, wb_ref, *, final):
    @pl.when((pl.program_id(0) == 0) & (pl.program_id(1) == 0))
    def _():
        wb_ref[...] = w_ref[...].astype(BF16)

    acc = None
    for i, ref in enumerate((ya_ref, yb_ref, yc_ref, yd_ref)):
        part = _dot(ref[0], wb_ref[i * GROUP_W:(i + 1) * GROUP_W, :])
        acc = part if acc is None else acc + part
    gate = mod_ref[:, 2 * D_MODEL:3 * D_MODEL]
    xn = x_ref[0] + gate * acc
    if final:
        xn = xn * lax.rsqrt(jnp.mean(xn * xn, axis=-1, keepdims=True) + EPS) * fw_ref[...]
    o_ref[0] = xn


def _outproj(ys, w_out, x, mod4, final_w, l):
    bsz = x.shape[0]
    final = l == DEPTH - 1
    row = lambda b, t: (b, t, 0)
    yspec = pl.BlockSpec((1, ROW_TILE, GROUP_W), row)
    return pl.pallas_call(
        functools.partial(_outproj_kernel, final=final),
        grid=(bsz, SEQ // ROW_TILE),
        in_specs=[yspec, yspec, yspec, yspec,
                  pl.BlockSpec((None, D_MODEL, D_MODEL), lambda b, t: (l, 0, 0)),
                  pl.BlockSpec((1, ROW_TILE, D_MODEL), row),
                  pl.BlockSpec((None, None, 1, 3 * D_MODEL), lambda b, t: (l, b, 0, 0)),
                  pl.BlockSpec((1, D_MODEL), lambda b, t: (0, 0))],
        out_specs=pl.BlockSpec((1, ROW_TILE, D_MODEL), row),
        out_shape=jax.ShapeDtypeStruct((bsz, SEQ, D_MODEL), F32),
        scratch_shapes=[pltpu.VMEM((D_MODEL, D_MODEL), BF16)],
        compiler_params=_params(2),
        name="outproj_final" if final else "outproj",
    )(*ys, w_out, x, mod4, final_w)


KEY_CHUNK = 256
WIDTH = 1024
PAIR_LANES = WIDTH // N_HEADS
VT_ROWS = HEAD_DIM + 16


def _build_toeplitz_t(tab_ref, slopes, patterns, off, unit=1):
    _, rows, tq = tab_ref.shape
    step = min(rows, 512)
    i_io = lax.broadcasted_iota(jnp.int32, (step, tq), 0)
    r_io = lax.broadcasted_iota(jnp.int32, (step, tq), 1)
    base = (i_io - r_io - off) * unit

    def write(i0, n):
        d = base[:n] + i0 * unit
        ad = jnp.abs(d)
        adf = ad.astype(F32)
        logm, valid = 0.0, None
        if patterns is not None:
            mult = jnp.zeros(d.shape, F32)
            for w, r in patterns:
                reach = r * (w // (2 * r))
                mult = mult + jnp.where(ad <= reach, jnp.where((d & (r - 1)) == 0, 1.0, 0.0), 0.0)
            valid = mult > 0.5
            logm = jnp.log(jnp.maximum(mult, 1.0))
        for h, slope in enumerate(slopes):
            val = (logm - slope * adf) * LOG2E
            if valid is not None:
                val = jnp.where(valid, val, NEG)
            tab_ref[h, pl.ds(i0, n), :] = val

    def body(j, _):
        write(pl.multiple_of(j * step, step), step)
        return 0

    lax.fori_loop(0, rows // step, body, 0)
    if rows % step:
        write(rows - rows % step, rows % step)


def _sublane_groups(x):
    return x.reshape(x.shape[0] // SUBLANES, SUBLANES, x.shape[1])


def _attention_t(q_ref, k_ref, v_ref, vt_ref, e_refs, qt_ref, acc_ref, *, n_pairs, key_chunk, win_chunks,
                 key_start, bias, finish):
    tq = WIDTH // n_pairs
    fw = GROUP_W // n_pairs
    n_blocks = SEQ // tq
    ones_row = jnp.where(lax.broadcasted_iota(jnp.int32, (VT_ROWS - HEAD_DIM, key_chunk), 0) == 0, 1.0, 0.0)
    for c in range(SEQ // key_chunk):
        rows = slice(c * key_chunk, (c + 1) * key_chunk)
        vt = v_ref[0, rows, :].astype(F32).T.astype(BF16)
        for h in range(N_HEADS):
            vt_ref[c, h * VT_ROWS:h * VT_ROWS + HEAD_DIM, :] = vt[h * HEAD_DIM:(h + 1) * HEAD_DIM]
            vt_ref[c, h * VT_ROWS + HEAD_DIM:(h + 1) * VT_ROWS, :] = ones_row.astype(BF16)

    feat = lax.broadcasted_iota(jnp.int32, (GROUP_W, tq), 0)
    mx0 = jnp.full((SUBLANES, WIDTH), NEG, F32)

    def load_queries(i):
        qt = q_ref[0, pl.ds(pl.multiple_of(i * tq, tq), tq), :].astype(F32).T
        for p in range(n_pairs):
            keep = (feat >= p * fw) & (feat < (p + 1) * fw)
            qt_ref[:, p * tq:(p + 1) * tq] = jnp.where(keep, qt, 0.0).astype(BF16)

    pph = n_pairs // N_HEADS

    def scores_head(c, i, h, e_ref, mx):
        lanes = slice(h * PAIR_LANES, (h + 1) * PAIR_LANES)
        k0 = pl.multiple_of((key_start(i) + c) * key_chunk, key_chunk)
        s = _dot(k_ref[0, pl.ds(k0, key_chunk), :], qt_ref[:, lanes])
        b = bias(h, i, c)
        e = jnp.concatenate([s[:, p * tq:(p + 1) * tq] + b for p in range(pph)], axis=1)
        e_ref[pl.ds(pl.multiple_of(c * key_chunk, key_chunk), key_chunk), lanes] = e
        return jnp.maximum(mx, jnp.max(_sublane_groups(e), axis=0))

    def scores(c, i, e_ref, mx):
        parts = [scores_head(c, i, h, e_ref, mx[:, h * PAIR_LANES:(h + 1) * PAIR_LANES]) for h in range(N_HEADS)]
        return jnp.concatenate(parts, axis=1)

    def probs_head(c, i, h, e_ref, m):
        lanes = slice(h * PAIR_LANES, (h + 1) * PAIR_LANES)
        r0 = pl.multiple_of(c * key_chunk, key_chunk)
        pb = jnp.exp2(e_ref[pl.ds(r0, key_chunk), lanes] - m[:, lanes]).astype(BF16)
        acc_ref[h] += _dot(vt_ref[key_start(i) + c, h * VT_ROWS:(h + 1) * VT_ROWS, :], pb)

    def probs(c, i, e_ref, m):
        for h in range(N_HEADS):
            probs_head(c, i, h, e_ref, m)

    def finish_block(i, m):
        sums = jnp.concatenate([acc_ref[h, HEAD_DIM:HEAD_DIM + 1, :] for h in range(N_HEADS)], axis=1)
        finish(i, m, sums, acc_ref)

    load_queries(0)
    mx = lax.fori_loop(0, win_chunks, lambda c, mx: scores(c, 0, e_refs[0], mx), mx0, unroll=True)

    def step(i, slot, m):
        load_queries(i + 1)
        acc_ref[...] = jnp.zeros_like(acc_ref)

        def both(c, mx):
            parts = []
            for h in range(N_HEADS):
                parts.append(scores_head(c, i + 1, h, e_refs[1 - slot], mx[:, h * PAIR_LANES:(h + 1) * PAIR_LANES]))
                probs_head(c, i, h, e_refs[slot], m)
            return jnp.concatenate(parts, axis=1)

        mx = lax.fori_loop(0, win_chunks, both, mx0, unroll=True)
        finish_block(i, m)
        return jnp.max(mx, axis=0, keepdims=True)

    def two_steps(j, m):
        return step(2 * j + 1, 1, step(2 * j, 0, m))

    m = lax.fori_loop(0, n_blocks // 2 - 1, two_steps, jnp.max(mx, axis=0, keepdims=True))
    m = step(n_blocks - 2, 0, m)
    acc_ref[...] = jnp.zeros_like(acc_ref)

    def last(c, _):
        probs(c, n_blocks - 1, e_refs[1], m)
        return 0

    lax.fori_loop(0, win_chunks, last, 0, unroll=True)
    finish_block(n_blocks - 1, m)


def _attn_specs():
    def col(j):
        return pl.BlockSpec((1, SEQ, GROUP_W), lambda b: (b, 0, j))
    return [col(0), col(1), col(2), col(3)]


def _attn_scratch(n_pairs, key_chunk, win_chunks, tab_rows):
    tq = WIDTH // n_pairs
    return [pltpu.VMEM((N_HEADS, tab_rows, tq), F32),
            pltpu.VMEM((SEQ // key_chunk, N_HEADS * VT_ROWS, key_chunk), BF16),
            pltpu.VMEM((win_chunks * key_chunk, WIDTH), F32),
            pltpu.VMEM((win_chunks * key_chunk, WIDTH), F32),
            pltpu.VMEM((GROUP_W, WIDTH), BF16),
            pltpu.VMEM((N_HEADS, VT_ROWS, PAIR_LANES), F32)]


def _mask_values(v_ref, vm_ref):
    v = v_ref[0]
    for h in range(N_HEADS):
        vm_ref[h * SEQ:(h + 1) * SEQ, :] = jnp.where(_head_mask(h), v, jnp.zeros_like(v))


A_PAIRS = N_HEADS
A_TQ = WIDTH // A_PAIRS
A_CHUNK = 256
A_WIN = 3
A_FAR = DILATED_PATTERNS[-1][1]
A_NEAR_PATTERNS = DILATED_PATTERNS[:-1]
A_NEAR_REACH = max(r * (w // (2 * r)) for w, r in A_NEAR_PATTERNS)
assert A_NEAR_REACH <= A_CHUNK and A_TQ == A_CHUNK
A_TAB_OFF = 2 * A_CHUNK
A_TAB_ROWS = A_TAB_OFF + A_WIN * A_CHUNK
A_CLASS = SEQ // A_FAR
A_PART = GROUP_W + LANES


def _a_key_start(i):
    return jnp.clip(i - 1, 0, SEQ // A_CHUNK - A_WIN)


def _a_far_partials(q_ref, k_ref, v_ref, tab_ref, x_ref, part_ref):
    tiles = GROUP_W // LANES

    def widen(i, _):
        r0 = pl.multiple_of(i * CONV_TILE, CONV_TILE)
        for a, ref in enumerate((q_ref, k_ref, v_ref)):
            x = ref[0, pl.ds(r0, CONV_TILE), :].astype(F32)
            for j in range(tiles):
                x_ref[a * tiles + j, pl.ds(r0, CONV_TILE), :] = x[:, j * LANES:(j + 1) * LANES]
        return 0

    lax.fori_loop(0, SEQ // CONV_TILE, widen, 0)

    n = A_CLASS
    feat = lax.broadcasted_iota(jnp.int32, (GROUP_W, n), 0)
    ones_rows = jnp.where(lax.broadcasted_iota(jnp.int32, (VT_ROWS - HEAD_DIM, n), 0) == 0, 1.0, 0.0).astype(BF16)
    pad = jnp.zeros((LANES - 2 * N_HEADS, n), F32)
    for rho in range(A_FAR):
        cls = pl.ds(rho, n, stride=A_FAR)
        cols = lambda a: jnp.concatenate([x_ref[a * tiles + j, cls, :] for j in range(tiles)], axis=1)
        qt = cols(0).T
        qt_all = jnp.concatenate(
            [jnp.where((feat >= h * HEAD_DIM) & (feat < (h + 1) * HEAD_DIM), qt, 0.0) for h in range(N_HEADS)],
            axis=1).astype(BF16)
        s = _dot(cols(1).astype(BF16), qt_all)
        e = jnp.concatenate([s[:, h * n:(h + 1) * n] + tab_ref[h] for h in range(N_HEADS)], axis=1)
        m = jnp.max(e, axis=0, keepdims=True)
        pb = jnp.exp2(e - m).astype(BF16)
        vt = cols(2).T.astype(BF16)
        accs = [_dot(jnp.concatenate([vt[h * HEAD_DIM:(h + 1) * HEAD_DIM], ones_rows], axis=0),
                     pb[:, h * n:(h + 1) * n]) for h in range(N_HEADS)]
        stats = jnp.concatenate([m[:, h * n:(h + 1) * n] for h in range(N_HEADS)]
                                + [a[HEAD_DIM:HEAD_DIM + 1] for a in accs] + [pad], axis=0)
        rec = jnp.concatenate([jnp.concatenate([a[0:HEAD_DIM] for a in accs], axis=0).T, stats.T], axis=1)
        for j in range(A_PART // LANES):
            part_ref[j, cls, :] = rec[:, j * LANES:(j + 1) * LANES]


def _mixer_a_kernel(q_ref, k_ref, v_ref, g_ref, o_ref, tab_ref, vt_ref, e0_ref, e1_ref, qt_ref, acc_ref,
                    far_tab_ref, x_ref, part_ref, *, slopes):
    @pl.when(pl.program_id(0) == 0)
    def _():
        _build_toeplitz_t(tab_ref, slopes, A_NEAR_PATTERNS, A_TAB_OFF)
        _build_toeplitz_t(far_tab_ref, slopes, DILATED_PATTERNS[-1:], 0, unit=A_FAR)

    _a_far_partials(q_ref, k_ref, v_ref, far_tab_ref, x_ref, part_ref)

    def bias(h, i, c):
        t0 = pl.multiple_of((_a_key_start(i) + c - i) * A_CHUNK + A_TAB_OFF, A_CHUNK)
        return tab_ref[h, pl.ds(t0, A_CHUNK), :]

    def finish(i, m, sums, acc_ref):
        q0 = pl.multiple_of(i * A_TQ, A_TQ)
        far = jnp.concatenate([part_ref[j, pl.ds(q0, A_TQ), :] for j in range(GROUP_W // LANES)], axis=1).T
        stats = part_ref[GROUP_W // LANES, pl.ds(q0, A_TQ), :].T
        outs = []
        for h in range(N_HEADS):
            lanes = slice(h * A_TQ, (h + 1) * A_TQ)
            m_far = stats[h:h + 1]
            top = jnp.maximum(m[:, lanes], m_far)
            w_near = jnp.exp2(m[:, lanes] - top)
            w_far = jnp.exp2(m_far - top)
            denom = sums[:, lanes] * w_near + stats[N_HEADS + h:N_HEADS + h + 1] * w_far
            outs.append((acc_ref[h, 0:HEAD_DIM, :] * w_near + far[h * HEAD_DIM:(h + 1) * HEAD_DIM] * w_far) / denom)
        g = g_ref[0, pl.ds(q0, A_TQ), :].astype(F32)
        o_ref[0, pl.ds(q0, A_TQ), :] = (jnp.concatenate(outs, axis=0).T * _silu(g)).astype(BF16)

    _attention_t(q_ref, k_ref, v_ref, vt_ref, (e0_ref, e1_ref), qt_ref, acc_ref, n_pairs=A_PAIRS,
                 key_chunk=A_CHUNK, win_chunks=A_WIN, key_start=_a_key_start, bias=bias, finish=finish)


def _mixer_a(pa, slopes):
    bsz = pa.shape[0]
    return pl.pallas_call(
        functools.partial(_mixer_a_kernel, slopes=slopes),
        grid=(bsz,),
        in_specs=_attn_specs(),
        out_specs=pl.BlockSpec((1, SEQ, GROUP_W), lambda b: (b, 0, 0)),
        out_shape=jax.ShapeDtypeStruct((bsz, SEQ, GROUP_W), BF16),
        scratch_shapes=_attn_scratch(A_PAIRS, A_CHUNK, A_WIN, A_TAB_ROWS) + [
            pltpu.VMEM((N_HEADS, A_CLASS, A_CLASS), F32),
            pltpu.VMEM((3 * GROUP_W // LANES, SEQ, LANES), F32),
            pltpu.VMEM((A_PART // LANES, SEQ, LANES), F32)],
        compiler_params=_params(1),
        name="mixer_a_dilated",
    )(pa, pa, pa, pa)


B_PAIRS = 2 * N_HEADS
B_TQ = WIDTH // B_PAIRS


def _mixer_b_kernel(q_ref, k_ref, v_ref, g_ref, lam_ref, nw_ref, o_ref, tab_ref, vt_ref, e0_ref, e1_ref, qt_ref,
                    acc_ref, *, slopes, lam_init):
    tq = B_TQ

    @pl.when(pl.program_id(0) == 0)
    def _():
        _build_toeplitz_t(tab_ref, slopes, None, SEQ - tq)

    def bias(h, i, c):
        return tab_ref[h, pl.ds(pl.multiple_of(SEQ - tq - i * tq + c * KEY_CHUNK, LANES), KEY_CHUNK), :]

    lv = lam_ref[...]
    lam = (jnp.exp(jnp.sum(lv[0:1] * lv[1:2], axis=-1, keepdims=True))
           - jnp.exp(jnp.sum(lv[2:3] * lv[3:4], axis=-1, keepdims=True)) + lam_init)
    nw = nw_ref[...]

    def finish(i, m, sums, acc_ref):
        q0 = pl.multiple_of(i * tq, tq)
        inv = 1.0 / sums
        outs = []
        for h in range(N_HEADS):
            acc = acc_ref[h, 0:HEAD_DIM, :]
            o = (acc[:, 0:tq] * inv[:, 2 * h * tq:(2 * h + 1) * tq]
                 - lam * (acc[:, tq:2 * tq] * inv[:, (2 * h + 1) * tq:(2 * h + 2) * tq]))
            ms = jnp.mean(o * o, axis=0, keepdims=True)
            outs.append(o * lax.rsqrt(ms + EPS))
        on = jnp.concatenate(outs, axis=0).T * nw * (1.0 - lam_init)
        g = g_ref[0, pl.ds(q0, tq), :].astype(F32)
        o_ref[0, pl.ds(q0, tq), :] = (on * _silu(g)).astype(BF16)

    _attention_t(q_ref, k_ref, v_ref, vt_ref, (e0_ref, e1_ref), qt_ref, acc_ref, n_pairs=B_PAIRS,
                 key_chunk=KEY_CHUNK, win_chunks=SEQ // KEY_CHUNK, key_start=lambda i: 0, bias=bias, finish=finish)


def _mixer_b(pb, lam_p, nw256, slopes, lam_init):
    bsz = pb.shape[0]
    return pl.pallas_call(
        functools.partial(_mixer_b_kernel, slopes=slopes, lam_init=lam_init),
        grid=(bsz,),
        in_specs=_attn_specs() + [pl.BlockSpec((4, DIFF_HEAD_DIM), lambda b: (0, 0)),
                                  pl.BlockSpec((1, GROUP_W), lambda b: (0, 0))],
        out_specs=pl.BlockSpec((1, SEQ, GROUP_W), lambda b: (b, 0, 0)),
        out_shape=jax.ShapeDtypeStruct((bsz, SEQ, GROUP_W), BF16),
        scratch_shapes=_attn_scratch(B_PAIRS, KEY_CHUNK, SEQ // KEY_CHUNK, 2 * SEQ - B_TQ),
        compiler_params=_params(1),
        name="mixer_b_diff",
    )(pb, pb, pb, pb, lam_p, nw256)


CONV_TILE = 256
CONV_HALO = SUBLANES
N_CHUNK = SEQ // SSM_CHUNK


def _expand_heads(x, expand3):
    hi = x.astype(BF16)
    r1 = x - hi.astype(F32)
    mid = r1.astype(BF16)
    lo = (r1 - mid.astype(F32)).astype(BF16)
    return _dot(jnp.concatenate([hi, mid, lo], axis=1), expand3)


def _cumsum_rows(a):
    row = lax.broadcasted_iota(jnp.int32, a.shape, 0)
    s = 1
    while s < a.shape[0]:
        a = a + jnp.where(row >= s, pltpu.roll(a, s, 0), 0.0)
        s *= 2
    return a


def _mixer_c_kernel(p_ref, dt_ref, cw_ref, cb_ref, alog_ref, alogx_ref, dtb_ref, dskip_ref, nw_ref, exp_ref,
                    o_ref, xpad, xc, y_s, sb_s, cs_s, db_s):
    L = SSM_CHUNK
    zero_rows = jnp.zeros((CONV_HALO, SSM_XBC), F32)
    xpad[0:CONV_HALO, :] = zero_rows
    xpad[CONV_HALO + SEQ:CONV_HALO + SEQ + CONV_HALO, :] = zero_rows

    def fill(i, _):
        r0 = pl.multiple_of(i * CONV_TILE, CONV_TILE)
        xpad[pl.ds(CONV_HALO + r0, CONV_TILE), :] = p_ref[0, pl.ds(r0, CONV_TILE), GROUP_W:].astype(F32)
        return 0

    lax.fori_loop(0, SEQ // CONV_TILE, fill, 0)

    def conv(i, _):
        r0 = pl.multiple_of(i * CONV_TILE, CONV_TILE)
        rows = CONV_TILE + 2 * CONV_HALO
        win = xpad[pl.ds(r0, rows), :]
        acc = jnp.zeros((CONV_TILE, SSM_XBC), F32) + cb_ref[...]
        for j in range(SSM_CONV):
            back = (SSM_CONV // 2 - j) % rows
            tap = win if back == 0 else pltpu.roll(win, back, 0)
            acc = acc + cw_ref[j:j + 1, :] * tap[CONV_HALO:CONV_HALO + CONV_TILE, :]
        xc[pl.ds(r0, CONV_TILE), :] = _silu(acc)
        return 0

    lax.fori_loop(0, SEQ // CONV_TILE, conv, 0)

    a_neg = -jnp.exp(alog_ref[...])
    a_neg_x = -jnp.exp(alogx_ref[...])
    expand = exp_ref[...]
    li = lax.broadcasted_iota(jnp.int32, (L, L), 0)
    si = lax.broadcasted_iota(jnp.int32, (L, L), 1)
    lower = si <= li
    upper = si >= li
    hmasks = [_head_mask(h) for h in range(N_HEADS)]

    def chunk_terms(t0):
        dtr = dt_ref[0, pl.ds(t0, L), :] + dtb_ref[...]
        dt = jnp.maximum(dtr, 0.0) + jnp.log(1.0 + jnp.exp(-jnp.abs(dtr)))
        a = dt * a_neg
        ainc = _cumsum_rows(a)
        aexc = ainc - a
        return dt, ainc, aexc

    def fwd(c, hf):
        t0 = pl.multiple_of(c * L, L)
        dt, ainc, aexc = chunk_terms(t0)
        both_x = _expand_heads(jnp.concatenate([dt, ainc], axis=0), expand)
        dt_x = both_x[0:L]
        ainc_x = both_x[L:2 * L]
        aexc_x = ainc_x - dt_x * a_neg_x
        ainc_t = ainc.T
        aexc_t = aexc.T
        xs = xc[pl.ds(t0, L), 0:GROUP_W]
        bm = xc[pl.ds(t0, L), GROUP_W:2 * GROUP_W]
        cm = xc[pl.ds(t0, L), 2 * GROUP_W:3 * GROUP_W]
        xf = xs * dt_x[:, 0:GROUP_W]
        xb = xs * dt_x[:, GROUP_W:2 * GROUP_W]
        xcat = jnp.concatenate([xf, xb], axis=0).astype(BF16)
        tot_f = ainc_x[L - 1:L, 0:GROUP_W]
        tot_b = ainc_x[L - 1:L, GROUP_W:2 * GROUP_W]
        y = xs * dskip_ref[...]
        cbs = []
        for g in range(SSM_GROUPS):
            gs = slice(g * SSM_STATE, (g + 1) * SSM_STATE)
            cbs.append(_dot_nt(cm[:, gs].astype(BF16), bm[:, gs].astype(BF16)))
        for h in range(N_HEADS):
            cb = cbs[h // (N_HEADS // SSM_GROUPS)]
            col_f = ainc[:, h:h + 1]
            row_f = ainc_t[h:h + 1, :]
            col_b = aexc[:, N_HEADS + h:N_HEADS + h + 1]
            row_b = aexc_t[N_HEADS + h:N_HEADS + h + 1, :]
            lf = jnp.exp(jnp.where(lower, col_f - row_f, NEG))
            ub = jnp.exp(jnp.where(upper, row_b - col_b, NEG))
            mcat = jnp.concatenate([cb * lf, cb * ub], axis=1).astype(BF16)
            y = y + jnp.where(hmasks[h], _dot(mcat, xcat), 0.0)
        wf = (jnp.exp(tot_f - ainc_x[:, 0:GROUP_W]) * xf).astype(BF16)
        wb = (jnp.exp(aexc_x[:, GROUP_W:2 * GROUP_W]) * xb).astype(BF16)
        ef = jnp.exp(ainc_x[:, 0:GROUP_W])
        cs_s[pl.ds(t0, L), :] = jnp.exp(tot_b - aexc_x[:, GROUP_W:2 * GROUP_W])
        db_s[pl.ds(pl.multiple_of(c * SUBLANES, SUBLANES), SUBLANES), :] = jnp.broadcast_to(
            jnp.exp(tot_b), (SUBLANES, GROUP_W))
        dec_f = jnp.exp(tot_f)
        hf_new = []
        yoff = []
        for g in range(SSM_GROUPS):
            gs = slice(g * SSM_STATE, (g + 1) * SSM_STATE)
            bt = bm[:, gs].T.astype(BF16)
            yoff.append(_dot(cm[:, gs].astype(BF16), hf[g].astype(BF16)))
            hf_new.append(dec_f[:, gs] * hf[g] + _dot(bt, wf[:, gs]))
            sb_s[pl.ds(t0, L), gs] = _dot(bt, wb[:, gs])
        y = y + jnp.concatenate(yoff, axis=1) * ef
        y_s[pl.ds(t0, L), :] = y
        return tuple(hf_new)

    h0 = tuple(jnp.zeros((SSM_STATE, SSM_STATE), F32) for _ in range(SSM_GROUPS))
    lax.fori_loop(0, N_CHUNK, fwd, h0, unroll=2)

    def bwd(i, hb):
        c = N_CHUNK - 1 - i
        t0 = pl.multiple_of(c * L, L)
        cm = xc[pl.ds(t0, L), 2 * GROUP_W:3 * GROUP_W]
        dec_b = db_s[pl.ds(pl.multiple_of(c * SUBLANES, SUBLANES), 1), :]
        yoff = []
        hb_new = []
        for g in range(SSM_GROUPS):
            gs = slice(g * SSM_STATE, (g + 1) * SSM_STATE)
            yoff.append(_dot(cm[:, gs].astype(BF16), hb[g].astype(BF16)))
            hb_new.append(dec_b[:, gs] * hb[g] + sb_s[pl.ds(t0, L), gs])
        y_s[pl.ds(t0, L), :] = y_s[pl.ds(t0, L), :] + jnp.concatenate(yoff, axis=1) * cs_s[pl.ds(t0, L), :]
        return tuple(hb_new)

    lax.fori_loop(0, N_CHUNK, bwd, h0, unroll=4)

    def fin(i, _):
        r0 = pl.multiple_of(i * CONV_TILE, CONV_TILE)
        z = p_ref[0, pl.ds(r0, CONV_TILE), 0:GROUP_W].astype(F32)
        y = y_s[pl.ds(r0, CONV_TILE), :] * _silu(z)
        parts = []
        for g in range(SSM_GROUPS):
            yg = y[:, g * SSM_STATE:(g + 1) * SSM_STATE]
            parts.append(yg * lax.rsqrt(jnp.mean(yg * yg, axis=-1, keepdims=True) + EPS))
        o_ref[0, pl.ds(r0, CONV_TILE), :] = (jnp.concatenate(parts, axis=1) * nw_ref[...]).astype(BF16)
        return 0

    lax.fori_loop(0, SEQ // CONV_TILE, fin, 0)


def _head_expand_matrix():
    j = lax.broadcasted_iota(jnp.int32, (LANES, 2 * GROUP_W), 0)
    c = lax.broadcasted_iota(jnp.int32, (LANES, 2 * GROUP_W), 1)
    return jnp.tile((j == c // HEAD_DIM).astype(BF16), (3, 1))


def _mixer_c(pc, pdt, conv_w, conv_b, a_log, dt_bias, d_skip, norm_w):
    bsz = pc.shape[0]
    pad8 = lambda v: jnp.pad(v.reshape(1, 2 * N_HEADS), ((0, 0), (0, LANES - 2 * N_HEADS)))
    small = lambda shape: pl.BlockSpec(shape, lambda b: (0,) * len(shape))
    return pl.pallas_call(
        _mixer_c_kernel,
        grid=(bsz,),
        in_specs=[pl.BlockSpec((1, SEQ, 1024), lambda b: (b, 0, 0)),
                  pl.BlockSpec((1, SEQ, DT_PAD), lambda b: (b, 0, 0)),
                  small((SSM_CONV, SSM_XBC)), small((1, SSM_XBC)),
                  small((1, LANES)), small((1, 2 * GROUP_W)), small((1, LANES)),
                  small((1, GROUP_W)), small((1, GROUP_W)),
                  small((3 * LANES, 2 * GROUP_W))],
        out_specs=pl.BlockSpec((1, SEQ, GROUP_W), lambda b: (b, 0, 0)),
        out_shape=jax.ShapeDtypeStruct((bsz, SEQ, GROUP_W), BF16),
        scratch_shapes=[pltpu.VMEM((SEQ + 2 * CONV_HALO, SSM_XBC), F32),
                        pltpu.VMEM((SEQ, SSM_XBC), F32),
                        pltpu.VMEM((SEQ, GROUP_W), F32),
                        pltpu.VMEM((SEQ, GROUP_W), F32),
                        pltpu.VMEM((SEQ, GROUP_W), F32),
                        pltpu.VMEM((N_CHUNK * SUBLANES, GROUP_W), F32)],
        compiler_params=_params(1),
        name="mixer_c_ssd",
    )(pc, pdt, conv_w, conv_b.reshape(1, SSM_XBC), pad8(a_log),
      jnp.repeat(a_log.reshape(-1), HEAD_DIM).reshape(1, 2 * GROUP_W), pad8(dt_bias),
      jnp.repeat(d_skip, HEAD_DIM).reshape(1, GROUP_W), norm_w.reshape(1, GROUP_W),
      _head_expand_matrix())


NA_ROWS = SEQ // GRID_W
NA_DR = 2 * NA_WIN_H - 1
NA_DC = 2 * NA_WIN_W - 1
NA_PAIRS = NA_DR - 1


def _na_bias_kernel(rpb_ref, o_ref):
    lh = pl.program_id(0)
    cq = lax.broadcasted_iota(jnp.int32, (GRID_W, LANES), 0)
    lane = lax.broadcasted_iota(jnp.int32, (GRID_W, LANES), 1)
    second = lane >= GRID_W
    ck = jnp.where(second, lane - GRID_W, lane)
    cs = jnp.clip(cq - NA_WIN_W // 2, 0, GRID_W - NA_WIN_W)
    inside = (ck >= cs) & (ck < cs + NA_WIN_W)
    dc = ck - cq + NA_WIN_W - 1
    base = lh * (NA_DR * NA_DC)
    rows = []
    for dr in range(NA_DR):
        acc = jnp.zeros((GRID_W, LANES), F32)
        for j in range(NA_DC):
            acc = jnp.where(dc == j, rpb_ref[base + dr * NA_DC + j], acc)
        rows.append(acc)
    for p in range(NA_PAIRS):
        o_ref[0, p] = jnp.where(inside, jnp.where(second, rows[p + 1], rows[p]), NEG)


def _na_bias(na_rpb):
    n = DEPTH * N_HEADS
    return pl.pallas_call(
        _na_bias_kernel,
        grid=(n,),
        in_specs=[pl.BlockSpec(memory_space=pltpu.SMEM)],
        out_specs=pl.BlockSpec((1, NA_PAIRS, GRID_W, LANES), lambda i: (i, 0, 0, 0)),
        out_shape=jax.ShapeDtypeStruct((n, NA_PAIRS, GRID_W, LANES), F32),
        compiler_params=_params(1),
        name="na_bias_table",
    )(na_rpb.reshape(-1))


NA_UNROLL = 8


def _mixer_d_kernel(q_ref, k_ref, v_ref, g_ref, bias_ref, o_ref, vm_ref):
    nkeys = NA_WIN_H * GRID_W
    _mask_values(v_ref, vm_ref)

    def score_row(r):
        rs = jnp.clip(r - NA_WIN_H // 2, 0, NA_ROWS - NA_WIN_H)
        k0 = pl.multiple_of(rs * GRID_W, GRID_W)
        q = q_ref[0, pl.ds(pl.multiple_of(r * GRID_W, GRID_W), GRID_W), :]
        qs = jnp.concatenate([jnp.where(_head_mask(h), q, jnp.zeros_like(q)) for h in range(N_HEADS)], axis=0)
        return _dot_nt(qs, k_ref[0, pl.ds(k0, nkeys), :])

    def prob_row(r, s_all):
        delta = r - jnp.clip(r - NA_WIN_H // 2, 0, NA_ROWS - NA_WIN_H)
        ps, invs = [], []
        for h in range(N_HEADS):
            bias = jnp.concatenate(
                [bias_ref[h, 2 * i - delta + NA_WIN_H - 1] for i in range(NA_WIN_H // 2)], axis=-1)
            e = s_all[h * GRID_W:(h + 1) * GRID_W] + bias
            m = jnp.max(e, axis=-1, keepdims=True)
            p = jnp.exp(e - m)
            invs.append(1.0 / jnp.sum(p, axis=-1, keepdims=True))
            ps.append(p.astype(BF16))
        inv = jnp.broadcast_to(invs[N_HEADS - 1], (GRID_W, GROUP_W))
        for h in range(N_HEADS - 2, -1, -1):
            inv = jnp.where(_head_mask(h), invs[h], inv)
        return jnp.concatenate(ps, axis=-1), inv

    def value_row(r, pcat, inv):
        rs = jnp.clip(r - NA_WIN_H // 2, 0, NA_ROWS - NA_WIN_H)
        k0 = pl.multiple_of(rs * GRID_W, GRID_W)
        q0 = pl.multiple_of(r * GRID_W, GRID_W)
        vcat = jnp.concatenate([vm_ref[pl.ds(pl.multiple_of(h * SEQ + k0, GRID_W), nkeys), :]
                                for h in range(N_HEADS)], axis=0)
        acc = _dot(pcat, vcat)
        g = g_ref[0, pl.ds(q0, GRID_W), :].astype(F32)
        o_ref[0, pl.ds(q0, GRID_W), :] = (acc * inv * _silu(g)).astype(BF16)

    def rstep(i, _):
        rows = [i * NA_UNROLL + u for u in range(NA_UNROLL)]
        s_next = score_row(rows[0])
        for u, r in enumerate(rows):
            s_cur = s_next
            if u + 1 < NA_UNROLL:
                s_next = score_row(rows[u + 1])
            value_row(r, *prob_row(r, s_cur))
        return 0

    lax.fori_loop(0, NA_ROWS // NA_UNROLL, rstep, 0)


def _mixer_d(pd, bias, l):
    bsz = pd.shape[0]
    return pl.pallas_call(
        _mixer_d_kernel,
        grid=(bsz,),
        in_specs=_attn_specs() + [pl.BlockSpec((N_HEADS, NA_PAIRS, GRID_W, LANES), lambda b: (l, 0, 0, 0))],
        out_specs=pl.BlockSpec((1, SEQ, GROUP_W), lambda b: (b, 0, 0)),
        out_shape=jax.ShapeDtypeStruct((bsz, SEQ, GROUP_W), BF16),
        scratch_shapes=[pltpu.VMEM((N_HEADS * SEQ, GROUP_W), BF16)],
        compiler_params=_params(1),
        name="mixer_d_neighbourhood",
    )(pd, pd, pd, pd, bias)


def kernel(x, c, norm_w, ada_w, ada_b, w_in, diff_lambda, diff_norm_w, conv_w, conv_b, ssm_a_log,
           ssm_dt_bias, ssm_d, ssm_norm_w, na_rpb, w_out, final_norm_w):
    bsz = x.shape[0]
    n = 2 * N_HEADS
    slopes = [2.0 ** (-8.0 * i / n) for i in range(1, n + 1)]
    slopes_a, slopes_b = tuple(slopes[0::2]), tuple(slopes[1::2])
    mod4 = _adaln(c, ada_w, ada_b).reshape(DEPTH, bsz, 1, 3 * D_MODEL)
    na_bias = _na_bias(na_rpb)
    w_main, w_dt = _prep_w_in(w_in)
    norm_w3 = norm_w.reshape(DEPTH, 1, D_MODEL)
    final_w = final_norm_w.reshape(1, D_MODEL)
    for l in range(DEPTH):
        pa, pb, pc, pd, pdt = _inproj(x, mod4, norm_w3, w_main, w_dt, l)
        lam_init = 0.8 - 0.6 * math.exp(-0.3 * l)
        ya = _mixer_a(pa, slopes_a)
        yb = _mixer_b(pb, diff_lambda[l], jnp.tile(diff_norm_w[l], N_HEADS).reshape(1, GROUP_W),
                      slopes_b, lam_init)
        yc = _mixer_c(pc, pdt, conv_w[l], conv_b[l], ssm_a_log[l], ssm_dt_bias[l], ssm_d[l], ssm_norm_w[l])
        yd = _mixer_d(pd, na_bias, l)
        x = _outproj((ya, yb, yc, yd), w_out, x, mod4, final_w, l)
    return x
```

```python
import functools
import math

import jax
import jax.numpy as jnp
from jax import lax
from jax.experimental import pallas as pl
from jax.experimental.pallas import tpu as pltpu

D_MODEL = 1024
SEQ = 2048
DEPTH = 2
HEAD_DIM = 64
GROUP_W = 256
N_HEADS = 4
EPS = 1e-6
DILATED_PATTERNS = ((128, 1), (512, 4), (2048, 16))
DIFF_HEAD_DIM = 32
SSM_GROUPS = 2
SSM_STATE = 128
SSM_CONV = 5
SSM_CHUNK = 128
SSM_XBC = 768
GRID_W = 64
NA_WIN_H = 8
NA_WIN_W = 16
D_IN = 13 * GROUP_W + SSM_XBC + 2 * N_HEADS

LANES = 128
SUBLANES = 8
VMEM_LIMIT = 56 * 1024 * 1024

NEG = -1e30
LOG2E = math.log2(math.e)
F32 = jnp.float32
BF16 = jnp.bfloat16
HIGHEST = lax.Precision.HIGHEST

ROW_TILE = 512
DT_PAD = LANES


def _silu(x):
    return x / (1.0 + jnp.exp(-x))


def _dot_nt(a, b):
    return lax.dot_general(a, b, (((1,), (1,)), ((), ())), preferred_element_type=F32)


def _dot(a, b):
    return jnp.dot(a, b, preferred_element_type=F32)


def _params(n_grid):
    return pltpu.CompilerParams(dimension_semantics=("arbitrary",) * n_grid,
                                vmem_limit_bytes=VMEM_LIMIT)


def _head_mask(h, width=HEAD_DIM, total=GROUP_W):
    lane = lax.broadcasted_iota(jnp.int32, (1, total), 1)
    return (lane >= h * width) & (lane < (h + 1) * width)


def _mod_kernel(c_ref, w_ref, b_ref, o_ref):
    c = c_ref[...]
    o_ref[0] = jnp.dot(_silu(c), w_ref[0], precision=HIGHEST,
                       preferred_element_type=F32) + b_ref[0]


def _adaln(c, ada_w, ada_b):
    bsz = c.shape[0]
    tn = 768
    return pl.pallas_call(
        _mod_kernel,
        grid=(DEPTH, 3 * D_MODEL // tn),
        in_specs=[pl.BlockSpec((bsz, D_MODEL), lambda l, j: (0, 0)),
                  pl.BlockSpec((1, D_MODEL, tn), lambda l, j: (l, 0, j)),
                  pl.BlockSpec((1, 1, tn), lambda l, j: (l, 0, j))],
        out_specs=pl.BlockSpec((1, bsz, tn), lambda l, j: (l, 0, j)),
        out_shape=jax.ShapeDtypeStruct((DEPTH, bsz, 3 * D_MODEL), F32),
        compiler_params=_params(2),
        name="adaln_mod",
    )(c, ada_w, ada_b.reshape(DEPTH, 1, 3 * D_MODEL))


W_MAIN = 16 * GROUP_W
DT_COL0 = 9 * GROUP_W + SSM_XBC


def _prep_w_in(w_in):
    scale = [1.0] * W_MAIN
    for c0, s in ((0, HEAD_DIM ** -0.5 * LOG2E), (4 * GROUP_W, DIFF_HEAD_DIM ** -0.5 * LOG2E),
                  (12 * GROUP_W, HEAD_DIM ** -0.5)):
        scale[c0:c0 + GROUP_W] = [s] * GROUP_W
    sc = jnp.asarray(scale, F32)
    d0 = DT_COL0 + 2 * N_HEADS
    w_main = jnp.concatenate([(w_in[:, :, :DT_COL0] * sc[:DT_COL0]).astype(BF16),
                              (w_in[:, :, d0:] * sc[DT_COL0:]).astype(BF16)], axis=2)
    w_dt = jnp.pad(w_in[:, :, DT_COL0:d0].astype(BF16), ((0, 0), (0, 0), (0, DT_PAD - 2 * N_HEADS)))
    return w_main, w_dt


def _inproj_kernel(x_ref, mod_ref, nw_ref, w_ref, wdt_ref, pa_ref, pb_ref, pc_ref, pd_ref, pdt_ref):
    x = x_ref[0]
    shift = mod_ref[:, 0:D_MODEL]
    scale = mod_ref[:, D_MODEL:2 * D_MODEL]
    y = x * lax.rsqrt(jnp.mean(x * x, axis=-1, keepdims=True) + EPS) * nw_ref[...]
    h = (y * (1.0 + scale) + shift).astype(BF16)
    for i, ref in enumerate((pa_ref, pb_ref, pc_ref, pd_ref)):
        ref[0] = _dot(h, w_ref[:, i * 1024:(i + 1) * 1024]).astype(BF16)
    pdt_ref[0] = _dot(h, wdt_ref[...])


def _inproj(x, mod4, norm_w, w_main, w_dt, l):
    bsz = x.shape[0]
    row = lambda b, t: (b, t, 0)
    big = pl.BlockSpec((1, ROW_TILE, 1024), row)
    return pl.pallas_call(
        _inproj_kernel,
        grid=(bsz, SEQ // ROW_TILE),
        in_specs=[pl.BlockSpec((1, ROW_TILE, D_MODEL), row),
                  pl.BlockSpec((None, None, 1, 3 * D_MODEL), lambda b, t: (l, b, 0, 0)),
                  pl.BlockSpec((None, 1, D_MODEL), lambda b, t: (l, 0, 0)),
                  pl.BlockSpec((None, D_MODEL, W_MAIN), lambda b, t: (l, 0, 0)),
                  pl.BlockSpec((None, D_MODEL, DT_PAD), lambda b, t: (l, 0, 0))],
        out_specs=[big, big, big, big, pl.BlockSpec((1, ROW_TILE, DT_PAD), row)],
        out_shape=[jax.ShapeDtypeStruct((bsz, SEQ, 1024), BF16)] * 4
        + [jax.ShapeDtypeStruct((bsz, SEQ, DT_PAD), F32)],
        compiler_params=_params(2),
        name="inproj",
    )(x, mod4, norm_w, w_main, w_dt)


def _outproj_kernel(ya_ref, yb_ref, yc_ref, yd_ref, w_ref, x_ref, mod_ref, fw_ref, o_ref, wb_ref, *, final):
    @pl.when((pl.program_id(0) == 0) & (pl.program_id(1) == 0))
    def _():
        wb_ref[...] = w_ref[...].astype(BF16)

    acc = None
    for i, ref in enumerate((ya_ref, yb_ref, yc_ref, yd_ref)):
        part = _dot(ref[0], wb_ref[i * GROUP_W:(i + 1) * GROUP_W, :])
        acc = part if acc is None else acc + part
    gate = mod_ref[:, 2 * D_MODEL:3 * D_MODEL]
    xn = x_ref[0] + gate * acc
    if final:
        xn = xn * lax.rsqrt(jnp.mean(xn * xn, axis=-1, keepdims=True) + EPS) * fw_ref[...]
    o_ref[0] = xn


def _outproj(ys, w_out, x, mod4, final_w, l):
    bsz = x.shape[0]
    final = l == DEPTH - 1
    row = lambda b, t: (b, t, 0)
    yspec = pl.BlockSpec((1, ROW_TILE, GROUP_W), row)
    return pl.pallas_call(
        functools.partial(_outproj_kernel, final=final),
        grid=(bsz, SEQ // ROW_TILE),
        in_specs=[yspec, yspec, yspec, yspec,
                  pl.BlockSpec((None, D_MODEL, D_MODEL), lambda b, t: (l, 0, 0)),
                  pl.BlockSpec((1, ROW_TILE, D_MODEL), row),
                  pl.BlockSpec((None, None, 1, 3 * D_MODEL), lambda b, t: (l, b, 0, 0)),
                  pl.BlockSpec((1, D_MODEL), lambda b, t: (0, 0))],
        out_specs=pl.BlockSpec((1, ROW_TILE, D_MODEL), row),
        out_shape=jax.ShapeDtypeStruct((bsz, SEQ, D_MODEL), F32),
        scratch_shapes=[pltpu.VMEM((D_MODEL, D_MODEL), BF16)],
        compiler_params=_params(2),
        name="outproj_final" if final else "outproj",
    )(*ys, w_out, x, mod4, final_w)


KEY_CHUNK = 256
WIDTH = 1024
PAIR_LANES = WIDTH // N_HEADS
VT_ROWS = HEAD_DIM + 16


def _build_toeplitz_t(tab_ref, slopes, patterns, off, unit=1):
    _, rows, tq = tab_ref.shape
    step = min(rows, 512)
    i_io = lax.broadcasted_iota(jnp.int32, (step, tq), 0)
    r_io = lax.broadcasted_iota(jnp.int32, (step, tq), 1)
    base = (i_io - r_io - off) * unit

    def write(i0, n):
        d = base[:n] + i0 * unit
        ad = jnp.abs(d)
        adf = ad.astype(F32)
        logm, valid = 0.0, None
        if patterns is not None:
            mult = jnp.zeros(d.shape, F32)
            for w, r in patterns:
                reach = r * (w // (2 * r))
                mult = mult + jnp.where(ad <= reach, jnp.where((d & (r - 1)) == 0, 1.0, 0.0), 0.0)
            valid = mult > 0.5
            logm = jnp.log(jnp.maximum(mult, 1.0))
        for h, slope in enumerate(slopes):
            val = (logm - slope * adf) * LOG2E
            if valid is not None:
                val = jnp.where(valid, val, NEG)
            tab_ref[h, pl.ds(i0, n), :] = val

    def body(j, _):
        write(pl.multiple_of(j * step, step), step)
        return 0

    lax.fori_loop(0, rows // step, body, 0)
    if rows % step:
        write(rows - rows % step, rows % step)


def _sublane_groups(x):
    return x.reshape(x.shape[0] // SUBLANES, SUBLANES, x.shape[1])


def _attention_t(q_ref, k_ref, v_ref, vt_ref, e_refs, qt_ref, acc_ref, *, n_pairs, key_chunk, win_chunks,
                 key_start, bias, finish):
    tq = WIDTH // n_pairs
    fw = GROUP_W // n_pairs
    n_blocks = SEQ // tq
    ones_row = jnp.where(lax.broadcasted_iota(jnp.int32, (VT_ROWS - HEAD_DIM, key_chunk), 0) == 0, 1.0, 0.0)
    for c in range(SEQ // key_chunk):
        rows = slice(c * key_chunk, (c + 1) * key_chunk)
        vt = v_ref[0, rows, :].astype(F32).T.astype(BF16)
        for h in range(N_HEADS):
            vt_ref[c, h * VT_ROWS:h * VT_ROWS + HEAD_DIM, :] = vt[h * HEAD_DIM:(h + 1) * HEAD_DIM]
            vt_ref[c, h * VT_ROWS + HEAD_DIM:(h + 1) * VT_ROWS, :] = ones_row.astype(BF16)

    feat = lax.broadcasted_iota(jnp.int32, (GROUP_W, tq), 0)
    mx0 = jnp.full((SUBLANES, WIDTH), NEG, F32)

    def load_queries(i):
        qt = q_ref[0, pl.ds(pl.multiple_of(i * tq, tq), tq), :].astype(F32).T
        for p in range(n_pairs):
            keep = (feat >= p * fw) & (feat < (p + 1) * fw)
            qt_ref[:, p * tq:(p + 1) * tq] = jnp.where(keep, qt, 0.0).astype(BF16)

    pph = n_pairs // N_HEADS

    def scores_head(c, i, h, e_ref, mx):
        lanes = slice(h * PAIR_LANES, (h + 1) * PAIR_LANES)
        k0 = pl.multiple_of((key_start(i) + c) * key_chunk, key_chunk)
        s = _dot(k_ref[0, pl.ds(k0, key_chunk), :], qt_ref[:, lanes])
        b = bias(h, i, c)
        e = jnp.concatenate([s[:, p * tq:(p + 1) * tq] + b for p in range(pph)], axis=1)
        e_ref[pl.ds(pl.multiple_of(c * key_chunk, key_chunk), key_chunk), lanes] = e
        return jnp.maximum(mx, jnp.max(_sublane_groups(e), axis=0))

    def scores(c, i, e_ref, mx):
        parts = [scores_head(c, i, h, e_ref, mx[:, h * PAIR_LANES:(h + 1) * PAIR_LANES]) for h in range(N_HEADS)]
        return jnp.concatenate(parts, axis=1)

    def probs_head(c, i, h, e_ref, m):
        lanes = slice(h * PAIR_LANES, (h + 1) * PAIR_LANES)
        r0 = pl.multiple_of(c * key_chunk, key_chunk)
        pb = jnp.exp2(e_ref[pl.ds(r0, key_chunk), lanes] - m[:, lanes]).astype(BF16)
        acc_ref[h] += _dot(vt_ref[key_start(i) + c, h * VT_ROWS:(h + 1) * VT_ROWS, :], pb)

    def probs(c, i, e_ref, m):
        for h in range(N_HEADS):
            probs_head(c, i, h, e_ref, m)

    def finish_block(i, m):
        sums = jnp.concatenate([acc_ref[h, HEAD_DIM:HEAD_DIM + 1, :] for h in range(N_HEADS)], axis=1)
        finish(i, m, sums, acc_ref)

    load_queries(0)
    mx = lax.fori_loop(0, win_chunks, lambda c, mx: scores(c, 0, e_refs[0], mx), mx0, unroll=True)

    def step(i, slot, m):
        load_queries(i + 1)
        acc_ref[...] = jnp.zeros_like(acc_ref)

        def both(c, mx):
            parts = []
            for h in range(N_HEADS):
                parts.append(scores_head(c, i + 1, h, e_refs[1 - slot], mx[:, h * PAIR_LANES:(h + 1) * PAIR_LANES]))
                probs_head(c, i, h, e_refs[slot], m)
            return jnp.concatenate(parts, axis=1)

        mx = lax.fori_loop(0, win_chunks, both, mx0, unroll=True)
        finish_block(i, m)
        return jnp.max(mx, axis=0, keepdims=True)

    def two_steps(j, m):
        return step(2 * j + 1, 1, step(2 * j, 0, m))

    m = lax.fori_loop(0, n_blocks // 2 - 1, two_steps, jnp.max(mx, axis=0, keepdims=True))
    m = step(n_blocks - 2, 0, m)
    acc_ref[...] = jnp.zeros_like(acc_ref)

    def last(c, _):
        probs(c, n_blocks - 1, e_refs[1], m)
        return 0

    lax.fori_loop(0, win_chunks, last, 0, unroll=True)
    finish_block(n_blocks - 1, m)


def _attn_specs():
    def col(j):
        return pl.BlockSpec((1, SEQ, GROUP_W), lambda b: (b, 0, j))
    return [col(0), col(1), col(2), col(3)]


def _attn_scratch(n_pairs, key_chunk, win_chunks, tab_rows):
    tq = WIDTH // n_pairs
    return [pltpu.VMEM((N_HEADS, tab_rows, tq), F32),
            pltpu.VMEM((SEQ // key_chunk, N_HEADS * VT_ROWS, key_chunk), BF16),
            pltpu.VMEM((win_chunks * key_chunk, WIDTH), F32),
            pltpu.VMEM((win_chunks * key_chunk, WIDTH), F32),
            pltpu.VMEM((GROUP_W, WIDTH), BF16),
            pltpu.VMEM((N_HEADS, VT_ROWS, PAIR_LANES), F32)]


def _mask_values(v_ref, vm_ref):
    v = v_ref[0]
    for h in range(N_HEADS):
        vm_ref[h * SEQ:(h + 1) * SEQ, :] = jnp.where(_head_mask(h), v, jnp.zeros_like(v))


A_PAIRS = N_HEADS
A_TQ = WIDTH // A_PAIRS
A_CHUNK = 256
A_WIN = 3
A_FAR = DILATED_PATTERNS[-1][1]
A_NEAR_PATTERNS = DILATED_PATTERNS[:-1]
A_NEAR_REACH = max(r * (w // (2 * r)) for w, r in A_NEAR_PATTERNS)
assert A_NEAR_REACH <= A_CHUNK and A_TQ == A_CHUNK
A_TAB_OFF = 2 * A_CHUNK
A_TAB_ROWS = A_TAB_OFF + A_WIN * A_CHUNK
A_CLASS = SEQ // A_FAR
A_PART = GROUP_W + LANES


def _a_key_start(i):
    return jnp.clip(i - 1, 0, SEQ // A_CHUNK - A_WIN)


def _a_far_partials(q_ref, k_ref, v_ref, tab_ref, x_ref, part_ref):
    tiles = GROUP_W // LANES

    n = A_CLASS
    for a, ref in enumerate((q_ref, k_ref, v_ref)):
        for j in range(tiles):
            x = ref[0, :, j * LANES:(j + 1) * LANES].astype(F32)
            x_ref[a * tiles + j] = jnp.swapaxes(x.reshape(n, A_FAR, LANES), 0, 1)

    feat = lax.broadcasted_iota(jnp.int32, (GROUP_W, n), 0)
    ones_rows = jnp.where(lax.broadcasted_iota(jnp.int32, (VT_ROWS - HEAD_DIM, n), 0) == 0, 1.0, 0.0).astype(BF16)
    pad = jnp.zeros((LANES - 2 * N_HEADS, n), F32)
    for rho in range(A_FAR):
        cls = pl.ds(rho, n, stride=A_FAR)
        cols = lambda a: jnp.concatenate([x_ref[a * tiles + j, rho] for j in range(tiles)], axis=1)
        qt = cols(0).T
        qt_all = jnp.concatenate(
            [jnp.where((feat >= h * HEAD_DIM) & (feat < (h + 1) * HEAD_DIM), qt, 0.0) for h in range(N_HEADS)],
            axis=1).astype(BF16)
        s = _dot(cols(1).astype(BF16), qt_all)
        e = jnp.concatenate([s[:, h * n:(h + 1) * n] + tab_ref[h] for h in range(N_HEADS)], axis=1)
        m = jnp.max(e, axis=0, keepdims=True)
        pb = jnp.exp2(e - m).astype(BF16)
        vt = cols(2).T.astype(BF16)
        accs = [_dot(jnp.concatenate([vt[h * HEAD_DIM:(h + 1) * HEAD_DIM], ones_rows], axis=0),
                     pb[:, h * n:(h + 1) * n]) for h in range(N_HEADS)]
        stats = jnp.concatenate([m[:, h * n:(h + 1) * n] for h in range(N_HEADS)]
                                + [a[HEAD_DIM:HEAD_DIM + 1] for a in accs] + [pad], axis=0)
        rec = jnp.concatenate([jnp.concatenate([a[0:HEAD_DIM] for a in accs], axis=0).T, stats.T], axis=1)
        for j in range(A_PART // LANES):
            part_ref[j, cls, :] = rec[:, j * LANES:(j + 1) * LANES]


def _mixer_a_kernel(q_ref, k_ref, v_ref, g_ref, o_ref, tab_ref, vt_ref, e0_ref, e1_ref, qt_ref, acc_ref,
                    far_tab_ref, x_ref, part_ref, *, slopes):
    @pl.when(pl.program_id(0) == 0)
    def _():
        _build_toeplitz_t(tab_ref, slopes, A_NEAR_PATTERNS, A_TAB_OFF)
        _build_toeplitz_t(far_tab_ref, slopes, DILATED_PATTERNS[-1:], 0, unit=A_FAR)

    _a_far_partials(q_ref, k_ref, v_ref, far_tab_ref, x_ref, part_ref)

    def bias(h, i, c):
        t0 = pl.multiple_of((_a_key_start(i) + c - i) * A_CHUNK + A_TAB_OFF, A_CHUNK)
        return tab_ref[h, pl.ds(t0, A_CHUNK), :]

    def finish(i, m, sums, acc_ref):
        q0 = pl.multiple_of(i * A_TQ, A_TQ)
        far = jnp.concatenate([part_ref[j, pl.ds(q0, A_TQ), :] for j in range(GROUP_W // LANES)], axis=1).T
        stats = part_ref[GROUP_W // LANES, pl.ds(q0, A_TQ), :].T
        outs = []
        for h in range(N_HEADS):
            lanes = slice(h * A_TQ, (h + 1) * A_TQ)
            m_far = stats[h:h + 1]
            top = jnp.maximum(m[:, lanes], m_far)
            w_near = jnp.exp2(m[:, lanes] - top)
            w_far = jnp.exp2(m_far - top)
            denom = sums[:, lanes] * w_near + stats[N_HEADS + h:N_HEADS + h + 1] * w_far
            outs.append((acc_ref[h, 0:HEAD_DIM, :] * w_near + far[h * HEAD_DIM:(h + 1) * HEAD_DIM] * w_far) / denom)
        g = g_ref[0, pl.ds(q0, A_TQ), :].astype(F32)
        o_ref[0, pl.ds(q0, A_TQ), :] = (jnp.concatenate(outs, axis=0).T * _silu(g)).astype(BF16)

    _attention_t(q_ref, k_ref, v_ref, vt_ref, (e0_ref, e1_ref), qt_ref, acc_ref, n_pairs=A_PAIRS,
                 key_chunk=A_CHUNK, win_chunks=A_WIN, key_start=_a_key_start, bias=bias, finish=finish)


def _mixer_a(pa, slopes):
    bsz = pa.shape[0]
    return pl.pallas_call(
        functools.partial(_mixer_a_kernel, slopes=slopes),
        grid=(bsz,),
        in_specs=_attn_specs(),
        out_specs=pl.BlockSpec((1, SEQ, GROUP_W), lambda b: (b, 0, 0)),
        out_shape=jax.ShapeDtypeStruct((bsz, SEQ, GROUP_W), BF16),
        scratch_shapes=_attn_scratch(A_PAIRS, A_CHUNK, A_WIN, A_TAB_ROWS) + [
            pltpu.VMEM((N_HEADS, A_CLASS, A_CLASS), F32),
            pltpu.VMEM((3 * GROUP_W // LANES, A_FAR, A_CLASS, LANES), F32),
            pltpu.VMEM((A_PART // LANES, SEQ, LANES), F32)],
        compiler_params=_params(1),
        name="mixer_a_dilated",
    )(pa, pa, pa, pa)


B_PAIRS = 2 * N_HEADS
B_TQ = WIDTH // B_PAIRS


def _mixer_b_kernel(q_ref, k_ref, v_ref, g_ref, lam_ref, nw_ref, o_ref, tab_ref, vt_ref, e0_ref, e1_ref, qt_ref,
                    acc_ref, *, slopes, lam_init):
    tq = B_TQ

    @pl.when(pl.program_id(0) == 0)
    def _():
        _build_toeplitz_t(tab_ref, slopes, None, SEQ - tq)

    def bias(h, i, c):
        return tab_ref[h, pl.ds(pl.multiple_of(SEQ - tq - i * tq + c * KEY_CHUNK, LANES), KEY_CHUNK), :]

    lv = lam_ref[...]
    lam = (jnp.exp(jnp.sum(lv[0:1] * lv[1:2], axis=-1, keepdims=True))
           - jnp.exp(jnp.sum(lv[2:3] * lv[3:4], axis=-1, keepdims=True)) + lam_init)
    nw = nw_ref[...]

    def finish(i, m, sums, acc_ref):
        q0 = pl.multiple_of(i * tq, tq)
        inv = 1.0 / sums
        outs = []
        for h in range(N_HEADS):
            acc = acc_ref[h, 0:HEAD_DIM, :]
            o = (acc[:, 0:tq] * inv[:, 2 * h * tq:(2 * h + 1) * tq]
                 - lam * (acc[:, tq:2 * tq] * inv[:, (2 * h + 1) * tq:(2 * h + 2) * tq]))
            ms = jnp.mean(o * o, axis=0, keepdims=True)
            outs.append(o * lax.rsqrt(ms + EPS))
        on = jnp.concatenate(outs, axis=0).T * nw * (1.0 - lam_init)
        g = g_ref[0, pl.ds(q0, tq), :].astype(F32)
        o_ref[0, pl.ds(q0, tq), :] = (on * _silu(g)).astype(BF16)

    _attention_t(q_ref, k_ref, v_ref, vt_ref, (e0_ref, e1_ref), qt_ref, acc_ref, n_pairs=B_PAIRS,
                 key_chunk=KEY_CHUNK, win_chunks=SEQ // KEY_CHUNK, key_start=lambda i: 0, bias=bias, finish=finish)


def _mixer_b(pb, lam_p, nw256, slopes, lam_init):
    bsz = pb.shape[0]
    return pl.pallas_call(
        functools.partial(_mixer_b_kernel, slopes=slopes, lam_init=lam_init),
        grid=(bsz,),
        in_specs=_attn_specs() + [pl.BlockSpec((4, DIFF_HEAD_DIM), lambda b: (0, 0)),
                                  pl.BlockSpec((1, GROUP_W), lambda b: (0, 0))],
        out_specs=pl.BlockSpec((1, SEQ, GROUP_W), lambda b: (b, 0, 0)),
        out_shape=jax.ShapeDtypeStruct((bsz, SEQ, GROUP_W), BF16),
        scratch_shapes=_attn_scratch(B_PAIRS, KEY_CHUNK, SEQ // KEY_CHUNK, 2 * SEQ - B_TQ),
        compiler_params=_params(1),
        name="mixer_b_diff",
    )(pb, pb, pb, pb, lam_p, nw256)


CONV_TILE = 256
CONV_HALO = SUBLANES
N_CHUNK = SEQ // SSM_CHUNK


def _expand_heads(x, expand3):
    hi = x.astype(BF16)
    r1 = x - hi.astype(F32)
    mid = r1.astype(BF16)
    lo = (r1 - mid.astype(F32)).astype(BF16)
    return _dot(jnp.concatenate([hi, mid, lo], axis=1), expand3)


def _cumsum_rows(a):
    row = lax.broadcasted_iota(jnp.int32, a.shape, 0)
    s = 1
    while s < a.shape[0]:
        a = a + jnp.where(row >= s, pltpu.roll(a, s, 0), 0.0)
        s *= 2
    return a


def _mixer_c_kernel(p_ref, dt_ref, cw_ref, cb_ref, alog_ref, alogx_ref, dtb_ref, dskip_ref, nw_ref, exp_ref,
                    o_ref, xpad, xc, y_s, sb_s, cs_s, db_s):
    L = SSM_CHUNK
    zero_rows = jnp.zeros((CONV_HALO, SSM_XBC), F32)
    xpad[0:CONV_HALO, :] = zero_rows
    xpad[CONV_HALO + SEQ:CONV_HALO + SEQ + CONV_HALO, :] = zero_rows

    def fill(i, _):
        r0 = pl.multiple_of(i * CONV_TILE, CONV_TILE)
        xpad[pl.ds(CONV_HALO + r0, CONV_TILE), :] = p_ref[0, pl.ds(r0, CONV_TILE), GROUP_W:].astype(F32)
        return 0

    lax.fori_loop(0, SEQ // CONV_TILE, fill, 0)

    def conv(i, _):
        r0 = pl.multiple_of(i * CONV_TILE, CONV_TILE)
        rows = CONV_TILE + 2 * CONV_HALO
        win = xpad[pl.ds(r0, rows), :]
        acc = jnp.zeros((CONV_TILE, SSM_XBC), F32) + cb_ref[...]
        for j in range(SSM_CONV):
            back = (SSM_CONV // 2 - j) % rows
            tap = win if back == 0 else pltpu.roll(win, back, 0)
            acc = acc + cw_ref[j:j + 1, :] * tap[CONV_HALO:CONV_HALO + CONV_TILE, :]
        xc[pl.ds(r0, CONV_TILE), :] = _silu(acc)
        return 0

    lax.fori_loop(0, SEQ // CONV_TILE, conv, 0)

    a_neg = -jnp.exp(alog_ref[...])
    a_neg_x = -jnp.exp(alogx_ref[...])
    expand = exp_ref[...]
    li = lax.broadcasted_iota(jnp.int32, (L, L), 0)
    si = lax.broadcasted_iota(jnp.int32, (L, L), 1)
    lower = si <= li
    upper = si >= li
    hmasks = [_head_mask(h) for h in range(N_HEADS)]

    def chunk_terms(t0):
        dtr = dt_ref[0, pl.ds(t0, L), :] + dtb_ref[...]
        dt = jnp.maximum(dtr, 0.0) + jnp.log(1.0 + jnp.exp(-jnp.abs(dtr)))
        a = dt * a_neg
        ainc = _cumsum_rows(a)
        aexc = ainc - a
        return dt, ainc, aexc

    def fwd(c, hf):
        t0 = pl.multiple_of(c * L, L)
        dt, ainc, aexc = chunk_terms(t0)
        both_x = _expand_heads(jnp.concatenate([dt, ainc], axis=0), expand)
        dt_x = both_x[0:L]
        ainc_x = both_x[L:2 * L]
        aexc_x = ainc_x - dt_x * a_neg_x
        ainc_t = ainc.T
        aexc_t = aexc.T
        xs = xc[pl.ds(t0, L), 0:GROUP_W]
        bm = xc[pl.ds(t0, L), GROUP_W:2 * GROUP_W]
        cm = xc[pl.ds(t0, L), 2 * GROUP_W:3 * GROUP_W]
        xf = xs * dt_x[:, 0:GROUP_W]
        xb = xs * dt_x[:, GROUP_W:2 * GROUP_W]
        xcat = jnp.concatenate([xf, xb], axis=0).astype(BF16)
        tot_f = ainc_x[L - 1:L, 0:GROUP_W]
        tot_b = ainc_x[L - 1:L, GROUP_W:2 * GROUP_W]
        y = xs * dskip_ref[...]
        cbs = []
        for g in range(SSM_GROUPS):
            gs = slice(g * SSM_STATE, (g + 1) * SSM_STATE)
            cbs.append(_dot_nt(cm[:, gs].astype(BF16), bm[:, gs].astype(BF16)))
        for h in range(N_HEADS):
            cb = cbs[h // (N_HEADS // SSM_GROUPS)]
            col_f = ainc[:, h:h + 1]
            row_f = ainc_t[h:h + 1, :]
            col_b = aexc[:, N_HEADS + h:N_HEADS + h + 1]
            row_b = aexc_t[N_HEADS + h:N_HEADS + h + 1, :]
            lf = jnp.exp(jnp.where(lower, col_f - row_f, NEG))
            ub = jnp.exp(jnp.where(upper, row_b - col_b, NEG))
            mcat = jnp.concatenate([cb * lf, cb * ub], axis=1).astype(BF16)
            y = y + jnp.where(hmasks[h], _dot(mcat, xcat), 0.0)
        wf = (jnp.exp(tot_f - ainc_x[:, 0:GROUP_W]) * xf).astype(BF16)
        wb = (jnp.exp(aexc_x[:, GROUP_W:2 * GROUP_W]) * xb).astype(BF16)
        ef = jnp.exp(ainc_x[:, 0:GROUP_W])
        cs_s[pl.ds(t0, L), :] = jnp.exp(tot_b - aexc_x[:, GROUP_W:2 * GROUP_W])
        db_s[pl.ds(pl.multiple_of(c * SUBLANES, SUBLANES), SUBLANES), :] = jnp.broadcast_to(
            jnp.exp(tot_b), (SUBLANES, GROUP_W))
        dec_f = jnp.exp(tot_f)
        hf_new = []
        yoff = []
        for g in range(SSM_GROUPS):
            gs = slice(g * SSM_STATE, (g + 1) * SSM_STATE)
            bt = bm[:, gs].T.astype(BF16)
            yoff.append(_dot(cm[:, gs].astype(BF16), hf[g].astype(BF16)))
            hf_new.append(dec_f[:, gs] * hf[g] + _dot(bt, wf[:, gs]))
            sb_s[pl.ds(t0, L), gs] = _dot(bt, wb[:, gs])
        y = y + jnp.concatenate(yoff, axis=1) * ef
        y_s[pl.ds(t0, L), :] = y
        return tuple(hf_new)

    h0 = tuple(jnp.zeros((SSM_STATE, SSM_STATE), F32) for _ in range(SSM_GROUPS))
    lax.fori_loop(0, N_CHUNK, fwd, h0, unroll=2)

    def bwd(i, hb):
        c = N_CHUNK - 1 - i
        t0 = pl.multiple_of(c * L, L)
        cm = xc[pl.ds(t0, L), 2 * GROUP_W:3 * GROUP_W]
        dec_b = db_s[pl.ds(pl.multiple_of(c * SUBLANES, SUBLANES), 1), :]
        yoff = []
        hb_new = []
        for g in range(SSM_GROUPS):
            gs = slice(g * SSM_STATE, (g + 1) * SSM_STATE)
            yoff.append(_dot(cm[:, gs].astype(BF16), hb[g].astype(BF16)))
            hb_new.append(dec_b[:, gs] * hb[g] + sb_s[pl.ds(t0, L), gs])
        y_s[pl.ds(t0, L), :] = y_s[pl.ds(t0, L), :] + jnp.concatenate(yoff, axis=1) * cs_s[pl.ds(t0, L), :]
        return tuple(hb_new)

    lax.fori_loop(0, N_CHUNK, bwd, h0, unroll=4)

    def fin(i, _):
        r0 = pl.multiple_of(i * CONV_TILE, CONV_TILE)
        z = p_ref[0, pl.ds(r0, CONV_TILE), 0:GROUP_W].astype(F32)
        y = y_s[pl.ds(r0, CONV_TILE), :] * _silu(z)
        parts = []
        for g in range(SSM_GROUPS):
            yg = y[:, g * SSM_STATE:(g + 1) * SSM_STATE]
            parts.append(yg * lax.rsqrt(jnp.mean(yg * yg, axis=-1, keepdims=True) + EPS))
        o_ref[0, pl.ds(r0, CONV_TILE), :] = (jnp.concatenate(parts, axis=1) * nw_ref[...]).astype(BF16)
        return 0

    lax.fori_loop(0, SEQ // CONV_TILE, fin, 0)


def _head_expand_matrix():
    j = lax.broadcasted_iota(jnp.int32, (LANES, 2 * GROUP_W), 0)
    c = lax.broadcasted_iota(jnp.int32, (LANES, 2 * GROUP_W), 1)
    return jnp.tile((j == c // HEAD_DIM).astype(BF16), (3, 1))


def _mixer_c(pc, pdt, conv_w, conv_b, a_log, dt_bias, d_skip, norm_w):
    bsz = pc.shape[0]
    pad8 = lambda v: jnp.pad(v.reshape(1, 2 * N_HEADS), ((0, 0), (0, LANES - 2 * N_HEADS)))
    small = lambda shape: pl.BlockSpec(shape, lambda b: (0,) * len(shape))
    return pl.pallas_call(
        _mixer_c_kernel,
        grid=(bsz,),
        in_specs=[pl.BlockSpec((1, SEQ, 1024), lambda b: (b, 0, 0)),
                  pl.BlockSpec((1, SEQ, DT_PAD), lambda b: (b, 0, 0)),
                  small((SSM_CONV, SSM_XBC)), small((1, SSM_XBC)),
                  small((1, LANES)), small((1, 2 * GROUP_W)), small((1, LANES)),
                  small((1, GROUP_W)), small((1, GROUP_W)),
                  small((3 * LANES, 2 * GROUP_W))],
        out_specs=pl.BlockSpec((1, SEQ, GROUP_W), lambda b: (b, 0, 0)),
        out_shape=jax.ShapeDtypeStruct((bsz, SEQ, GROUP_W), BF16),
        scratch_shapes=[pltpu.VMEM((SEQ + 2 * CONV_HALO, SSM_XBC), F32),
                        pltpu.VMEM((SEQ, SSM_XBC), F32),
                        pltpu.VMEM((SEQ, GROUP_W), F32),
                        pltpu.VMEM((SEQ, GROUP_W), F32),
                        pltpu.VMEM((SEQ, GROUP_W), F32),
                        pltpu.VMEM((N_CHUNK * SUBLANES, GROUP_W), F32)],
        compiler_params=_params(1),
        name="mixer_c_ssd",
    )(pc, pdt, conv_w, conv_b.reshape(1, SSM_XBC), pad8(a_log),
      jnp.repeat(a_log.reshape(-1), HEAD_DIM).reshape(1, 2 * GROUP_W), pad8(dt_bias),
      jnp.repeat(d_skip, HEAD_DIM).reshape(1, GROUP_W), norm_w.reshape(1, GROUP_W),
      _head_expand_matrix())


NA_ROWS = SEQ // GRID_W
NA_DR = 2 * NA_WIN_H - 1
NA_DC = 2 * NA_WIN_W - 1
NA_PAIRS = NA_DR - 1


def _na_bias_kernel(rpb_ref, o_ref):
    lh = pl.program_id(0)
    cq = lax.broadcasted_iota(jnp.int32, (GRID_W, LANES), 0)
    lane = lax.broadcasted_iota(jnp.int32, (GRID_W, LANES), 1)
    second = lane >= GRID_W
    ck = jnp.where(second, lane - GRID_W, lane)
    cs = jnp.clip(cq - NA_WIN_W // 2, 0, GRID_W - NA_WIN_W)
    inside = (ck >= cs) & (ck < cs + NA_WIN_W)
    dc = ck - cq + NA_WIN_W - 1
    base = lh * (NA_DR * NA_DC)
    rows = []
    for dr in range(NA_DR):
        acc = jnp.zeros((GRID_W, LANES), F32)
        for j in range(NA_DC):
            acc = jnp.where(dc == j, rpb_ref[base + dr * NA_DC + j], acc)
        rows.append(acc)
    for p in range(NA_PAIRS):
        o_ref[0, p] = jnp.where(inside, jnp.where(second, rows[p + 1], rows[p]), NEG)


def _na_bias(na_rpb):
    n = DEPTH * N_HEADS
    return pl.pallas_call(
        _na_bias_kernel,
        grid=(n,),
        in_specs=[pl.BlockSpec(memory_space=pltpu.SMEM)],
        out_specs=pl.BlockSpec((1, NA_PAIRS, GRID_W, LANES), lambda i: (i, 0, 0, 0)),
        out_shape=jax.ShapeDtypeStruct((n, NA_PAIRS, GRID_W, LANES), F32),
        compiler_params=_params(1),
        name="na_bias_table",
    )(na_rpb.reshape(-1))


NA_UNROLL = 8


def _mixer_d_kernel(q_ref, k_ref, v_ref, g_ref, bias_ref, o_ref, vm_ref):
    nkeys = NA_WIN_H * GRID_W
    _mask_values(v_ref, vm_ref)

    def score_row(r):
        rs = jnp.clip(r - NA_WIN_H // 2, 0, NA_ROWS - NA_WIN_H)
        k0 = pl.multiple_of(rs * GRID_W, GRID_W)
        q = q_ref[0, pl.ds(pl.multiple_of(r * GRID_W, GRID_W), GRID_W), :]
        qs = jnp.concatenate([jnp.where(_head_mask(h), q, jnp.zeros_like(q)) for h in range(N_HEADS)], axis=0)
        return _dot_nt(qs, k_ref[0, pl.ds(k0, nkeys), :])

    def prob_row(r, s_all):
        delta = r - jnp.clip(r - NA_WIN_H // 2, 0, NA_ROWS - NA_WIN_H)
        ps, invs = [], []
        for h in range(N_HEADS):
            bias = jnp.concatenate(
                [bias_ref[h, 2 * i - delta + NA_WIN_H - 1] for i in range(NA_WIN_H // 2)], axis=-1)
            e = s_all[h * GRID_W:(h + 1) * GRID_W] + bias
            m = jnp.max(e, axis=-1, keepdims=True)
            p = jnp.exp(e - m)
            invs.append(1.0 / jnp.sum(p, axis=-1, keepdims=True))
            ps.append(p.astype(BF16))
        inv = jnp.broadcast_to(invs[N_HEADS - 1], (GRID_W, GROUP_W))
        for h in range(N_HEADS - 2, -1, -1):
            inv = jnp.where(_head_mask(h), invs[h], inv)
        return jnp.concatenate(ps, axis=-1), inv

    def value_row(r, pcat, inv):
        rs = jnp.clip(r - NA_WIN_H // 2, 0, NA_ROWS - NA_WIN_H)
        k0 = pl.multiple_of(rs * GRID_W, GRID_W)
        q0 = pl.multiple_of(r * GRID_W, GRID_W)
        vcat = jnp.concatenate([vm_ref[pl.ds(pl.multiple_of(h * SEQ + k0, GRID_W), nkeys), :]
                                for h in range(N_HEADS)], axis=0)
        acc = _dot(pcat, vcat)
        g = g_ref[0, pl.ds(q0, GRID_W), :].astype(F32)
        o_ref[0, pl.ds(q0, GRID_W), :] = (acc * inv * _silu(g)).astype(BF16)

    def rstep(i, _):
        rows = [i * NA_UNROLL + u for u in range(NA_UNROLL)]
        s_next = score_row(rows[0])
        for u, r in enumerate(rows):
            s_cur = s_next
            if u + 1 < NA_UNROLL:
                s_next = score_row(rows[u + 1])
            value_row(r, *prob_row(r, s_cur))
        return 0

    lax.fori_loop(0, NA_ROWS // NA_UNROLL, rstep, 0)


def _mixer_d(pd, bias, l):
    bsz = pd.shape[0]
    return pl.pallas_call(
        _mixer_d_kernel,
        grid=(bsz,),
        in_specs=_attn_specs() + [pl.BlockSpec((N_HEADS, NA_PAIRS, GRID_W, LANES), lambda b: (l, 0, 0, 0))],
        out_specs=pl.BlockSpec((1, SEQ, GROUP_W), lambda b: (b, 0, 0)),
        out_shape=jax.ShapeDtypeStruct((bsz, SEQ, GROUP_W), BF16),
        scratch_shapes=[pltpu.VMEM((N_HEADS * SEQ, GROUP_W), BF16)],
        compiler_params=_params(1),
        name="mixer_d_neighbourhood",
    )(pd, pd, pd, pd, bias)


def kernel(x, c, norm_w, ada_w, ada_b, w_in, diff_lambda, diff_norm_w, conv_w, conv_b, ssm_a_log,
           ssm_dt_bias, ssm_d, ssm_norm_w, na_rpb, w_out, final_norm_w):
    bsz = x.shape[0]
    n = 2 * N_HEADS
    slopes = [2.0 ** (-8.0 * i / n) for i in range(1, n + 1)]
    slopes_a, slopes_b = tuple(slopes[0::2]), tuple(slopes[1::2])
    mod4 = _adaln(c, ada_w, ada_b).reshape(DEPTH, bsz, 1, 3 * D_MODEL)
    na_bias = _na_bias(na_rpb)
    w_main, w_dt = _prep_w_in(w_in)
    norm_w3 = norm_w.reshape(DEPTH, 1, D_MODEL)
    final_w = final_norm_w.reshape(1, D_MODEL)
    for l in range(DEPTH):
        pa, pb, pc, pd, pdt = _inproj(x, mod4, norm_w3, w_main, w_dt, l)
        lam_init = 0.8 - 0.6 * math.exp(-0.3 * l)
        ya = _mixer_a(pa, slopes_a)
        yb = _mixer_b(pb, diff_lambda[l], jnp.tile(diff_norm_w[l], N_HEADS).reshape(1, GROUP_W),
                      slopes_b, lam_init)
        yc = _mixer_c(pc, pdt, conv_w[l], conv_b[l], ssm_a_log[l], ssm_dt_bias[l], ssm_d[l], ssm_norm_w[l])
        yd = _mixer_d(pd, na_bias, l)
        x = _outproj((ya, yb, yc, yd), w_out, x, mod4, final_w, l)
    return x
```

```python
import functools
import math

import jax
import jax.numpy as jnp
from jax import lax
from jax.experimental import pallas as pl
from jax.experimental.pallas import tpu as pltpu

D_MODEL = 1024
SEQ = 2048
DEPTH = 2
HEAD_DIM = 64
GROUP_W = 256
N_HEADS = 4
EPS = 1e-6
DILATED_PATTERNS = ((128, 1), (512, 4), (2048, 16))
DIFF_HEAD_DIM = 32
SSM_GROUPS = 2
SSM_STATE = 128
SSM_CONV = 5
SSM_CHUNK = 128
SSM_XBC = 768
GRID_W = 64
NA_WIN_H = 8
NA_WIN_W = 16
D_IN = 13 * GROUP_W + SSM_XBC + 2 * N_HEADS

LANES = 128
SUBLANES = 8
VMEM_LIMIT = 56 * 1024 * 1024

NEG = -1e30
LOG2E = math.log2(math.e)
F32 = jnp.float32
BF16 = jnp.bfloat16
HIGHEST = lax.Precision.HIGHEST

ROW_TILE = 512
DT_PAD = LANES


def _silu(x):
    return x / (1.0 + jnp.exp(-x))


def _dot_nt(a, b):
    return lax.dot_general(a, b, (((1,), (1,)), ((), ())), preferred_element_type=F32)


def _dot(a, b):
    return jnp.dot(a, b, preferred_element_type=F32)


def _params(n_grid):
    return pltpu.CompilerParams(dimension_semantics=("arbitrary",) * n_grid,
                                vmem_limit_bytes=VMEM_LIMIT)


def _head_mask(h, width=HEAD_DIM, total=GROUP_W):
    lane = lax.broadcasted_iota(jnp.int32, (1, total), 1)
    return (lane >= h * width) & (lane < (h + 1) * width)


def _mod_kernel(c_ref, w_ref, b_ref, o_ref):
    c = c_ref[...]
    o_ref[0] = jnp.dot(_silu(c), w_ref[0], precision=HIGHEST,
                       preferred_element_type=F32) + b_ref[0]


def _adaln(c, ada_w, ada_b):
    bsz = c.shape[0]
    tn = 768
    return pl.pallas_call(
        _mod_kernel,
        grid=(DEPTH, 3 * D_MODEL // tn),
        in_specs=[pl.BlockSpec((bsz, D_MODEL), lambda l, j: (0, 0)),
                  pl.BlockSpec((1, D_MODEL, tn), lambda l, j: (l, 0, j)),
                  pl.BlockSpec((1, 1, tn), lambda l, j: (l, 0, j))],
        out_specs=pl.BlockSpec((1, bsz, tn), lambda l, j: (l, 0, j)),
        out_shape=jax.ShapeDtypeStruct((DEPTH, bsz, 3 * D_MODEL), F32),
        compiler_params=_params(2),
        name="adaln_mod",
    )(c, ada_w, ada_b.reshape(DEPTH, 1, 3 * D_MODEL))


W_MAIN = 16 * GROUP_W
DT_COL0 = 9 * GROUP_W + SSM_XBC


def _prep_w_in(w_in):
    scale = [1.0] * W_MAIN
    for c0, s in ((0, HEAD_DIM ** -0.5 * LOG2E), (4 * GROUP_W, DIFF_HEAD_DIM ** -0.5 * LOG2E),
                  (12 * GROUP_W, HEAD_DIM ** -0.5)):
        scale[c0:c0 + GROUP_W] = [s] * GROUP_W
    sc = jnp.asarray(scale, F32)
    d0 = DT_COL0 + 2 * N_HEADS
    w_main = jnp.concatenate([(w_in[:, :, :DT_COL0] * sc[:DT_COL0]).astype(BF16),
                              (w_in[:, :, d0:] * sc[DT_COL0:]).astype(BF16)], axis=2)
    w_dt = jnp.pad(w_in[:, :, DT_COL0:d0].astype(BF16), ((0, 0), (0, 0), (0, DT_PAD - 2 * N_HEADS)))
    return w_main, w_dt


def _inproj_kernel(x_ref, mod_ref, nw_ref, w_ref, wdt_ref, pa_ref, pb_ref, pc_ref, pd_ref, pdt_ref):
    x = x_ref[0]
    shift = mod_ref[:, 0:D_MODEL]
    scale = mod_ref[:, D_MODEL:2 * D_MODEL]
    y = x * lax.rsqrt(jnp.mean(x * x, axis=-1, keepdims=True) + EPS) * nw_ref[...]
    h = (y * (1.0 + scale) + shift).astype(BF16)
    for i, ref in enumerate((pa_ref, pb_ref, pc_ref, pd_ref)):
        ref[0] = _dot(h, w_ref[:, i * 1024:(i + 1) * 1024]).astype(BF16)
    pdt_ref[0] = _dot(h, wdt_ref[...])


def _inproj(x, mod4, norm_w, w_main, w_dt, l):
    bsz = x.shape[0]
    row = lambda b, t: (b, t, 0)
    big = pl.BlockSpec((1, ROW_TILE, 1024), row)
    return pl.pallas_call(
        _inproj_kernel,
        grid=(bsz, SEQ // ROW_TILE),
        in_specs=[pl.BlockSpec((1, ROW_TILE, D_MODEL), row),
                  pl.BlockSpec((None, None, 1, 3 * D_MODEL), lambda b, t: (l, b, 0, 0)),
                  pl.BlockSpec((None, 1, D_MODEL), lambda b, t: (l, 0, 0)),
                  pl.BlockSpec((None, D_MODEL, W_MAIN), lambda b, t: (l, 0, 0)),
                  pl.BlockSpec((None, D_MODEL, DT_PAD), lambda b, t: (l, 0, 0))],
        out_specs=[big, big, big, big, pl.BlockSpec((1, ROW_TILE, DT_PAD), row)],
        out_shape=[jax.ShapeDtypeStruct((bsz, SEQ, 1024), BF16)] * 4
        + [jax.ShapeDtypeStruct((bsz, SEQ, DT_PAD), F32)],
        compiler_params=_params(2),
        name="inproj",
    )(x, mod4, norm_w, w_main, w_dt)


def _outproj_kernel(ya_ref, yb_ref, yc_ref, yd_ref, w_ref, x_ref, mod_ref, fw_ref, o_ref, wb_ref, *, final):
    @pl.when((pl.program_id(0) == 0) & (pl.program_id(1) == 0))
    def _():
        wb_ref[...] = w_ref[...].astype(BF16)

    acc = None
    for i, ref in enumerate((ya_ref, yb_ref, yc_ref, yd_ref)):
        part = _dot(ref[0], wb_ref[i * GROUP_W:(i + 1) * GROUP_W, :])
        acc = part if acc is None else acc + part
    gate = mod_ref[:, 2 * D_MODEL:3 * D_MODEL]
    xn = x_ref[0] + gate * acc
    if final:
        xn = xn * lax.rsqrt(jnp.mean(xn * xn, axis=-1, keepdims=True) + EPS) * fw_ref[...]
    o_ref[0] = xn


def _outproj(ys, w_out, x, mod4, final_w, l):
    bsz = x.shape[0]
    final = l == DEPTH - 1
    row = lambda b, t: (b, t, 0)
    yspec = pl.BlockSpec((1, ROW_TILE, GROUP_W), row)
    return pl.pallas_call(
        functools.partial(_outproj_kernel, final=final),
        grid=(bsz, SEQ // ROW_TILE),
        in_specs=[yspec, yspec, yspec, yspec,
                  pl.BlockSpec((None, D_MODEL, D_MODEL), lambda b, t: (l, 0, 0)),
                  pl.BlockSpec((1, ROW_TILE, D_MODEL), row),
                  pl.BlockSpec((None, None, 1, 3 * D_MODEL), lambda b, t: (l, b, 0, 0)),
                  pl.BlockSpec((1, D_MODEL), lambda b, t: (0, 0))],
        out_specs=pl.BlockSpec((1, ROW_TILE, D_MODEL), row),
        out_shape=jax.ShapeDtypeStruct((bsz, SEQ, D_MODEL), F32),
        scratch_shapes=[pltpu.VMEM((D_MODEL, D_MODEL), BF16)],
        compiler_params=_params(2),
        name="outproj_final" if final else "outproj",
    )(*ys, w_out, x, mod4, final_w)


def _midproj_kernel(ya_ref, yb_ref, yc_ref, yd_ref, wo_ref, x_ref, mod_ref, modn_ref, nw_ref, w_ref, wdt_ref,
                    xo_ref, pa_ref, pb_ref, pc_ref, pd_ref, pdt_ref, wb_ref):
    @pl.when((pl.program_id(0) == 0) & (pl.program_id(1) == 0))
    def _():
        wb_ref[...] = wo_ref[...].astype(BF16)

    acc = None
    for i, ref in enumerate((ya_ref, yb_ref, yc_ref, yd_ref)):
        part = _dot(ref[0], wb_ref[i * GROUP_W:(i + 1) * GROUP_W, :])
        acc = part if acc is None else acc + part
    x = x_ref[0] + mod_ref[:, 2 * D_MODEL:3 * D_MODEL] * acc
    xo_ref[0] = x
    shift = modn_ref[:, 0:D_MODEL]
    scale = modn_ref[:, D_MODEL:2 * D_MODEL]
    y = x * lax.rsqrt(jnp.mean(x * x, axis=-1, keepdims=True) + EPS) * nw_ref[...]
    h = (y * (1.0 + scale) + shift).astype(BF16)
    for i, ref in enumerate((pa_ref, pb_ref, pc_ref, pd_ref)):
        ref[0] = _dot(h, w_ref[:, i * 1024:(i + 1) * 1024]).astype(BF16)
    pdt_ref[0] = _dot(h, wdt_ref[...])


def _midproj(ys, w_out, x, mod4, norm_w, w_main, w_dt, l):
    bsz = x.shape[0]
    row = lambda b, t: (b, t, 0)
    yspec = pl.BlockSpec((1, ROW_TILE, GROUP_W), row)
    big = pl.BlockSpec((1, ROW_TILE, 1024), row)
    xspec = pl.BlockSpec((1, ROW_TILE, D_MODEL), row)
    once = pl.Buffered(1)
    return pl.pallas_call(
        _midproj_kernel,
        grid=(bsz, SEQ // ROW_TILE),
        in_specs=[yspec, yspec, yspec, yspec,
                  pl.BlockSpec((None, D_MODEL, D_MODEL), lambda b, t: (l, 0, 0), pipeline_mode=once),
                  xspec,
                  pl.BlockSpec((None, None, 1, 3 * D_MODEL), lambda b, t: (l, b, 0, 0)),
                  pl.BlockSpec((None, None, 1, 3 * D_MODEL), lambda b, t: (l + 1, b, 0, 0)),
                  pl.BlockSpec((None, 1, D_MODEL), lambda b, t: (l + 1, 0, 0)),
                  pl.BlockSpec((None, D_MODEL, W_MAIN), lambda b, t: (l + 1, 0, 0), pipeline_mode=once),
                  pl.BlockSpec((None, D_MODEL, DT_PAD), lambda b, t: (l + 1, 0, 0))],
        out_specs=[xspec, big, big, big, big, pl.BlockSpec((1, ROW_TILE, DT_PAD), row)],
        out_shape=[jax.ShapeDtypeStruct((bsz, SEQ, D_MODEL), F32)]
        + [jax.ShapeDtypeStruct((bsz, SEQ, 1024), BF16)] * 4
        + [jax.ShapeDtypeStruct((bsz, SEQ, DT_PAD), F32)],
        scratch_shapes=[pltpu.VMEM((D_MODEL, D_MODEL), BF16)],
        compiler_params=_params(2),
        name="midproj",
    )(*ys, w_out, x, mod4, mod4, norm_w, w_main, w_dt)


KEY_CHUNK = 256
WIDTH = 1024
PAIR_LANES = WIDTH // N_HEADS
VT_ROWS = HEAD_DIM + 16


def _build_toeplitz_t(tab_ref, slopes, patterns, off, unit=1):
    _, rows, tq = tab_ref.shape
    step = min(rows, 512)
    i_io = lax.broadcasted_iota(jnp.int32, (step, tq), 0)
    r_io = lax.broadcasted_iota(jnp.int32, (step, tq), 1)
    base = (i_io - r_io - off) * unit

    def write(i0, n):
        d = base[:n] + i0 * unit
        ad = jnp.abs(d)
        adf = ad.astype(F32)
        logm, valid = 0.0, None
        if patterns is not None:
            mult = jnp.zeros(d.shape, F32)
            for w, r in patterns:
                reach = r * (w // (2 * r))
                mult = mult + jnp.where(ad <= reach, jnp.where((d & (r - 1)) == 0, 1.0, 0.0), 0.0)
            valid = mult > 0.5
            logm = jnp.log(jnp.maximum(mult, 1.0))
        for h, slope in enumerate(slopes):
            val = (logm - slope * adf) * LOG2E
            if valid is not None:
                val = jnp.where(valid, val, NEG)
            tab_ref[h, pl.ds(i0, n), :] = val

    def body(j, _):
        write(pl.multiple_of(j * step, step), step)
        return 0

    lax.fori_loop(0, rows // step, body, 0)
    if rows % step:
        write(rows - rows % step, rows % step)


def _sublane_groups(x):
    return x.reshape(x.shape[0] // SUBLANES, SUBLANES, x.shape[1])


def _attention_t(q_ref, k_ref, v_ref, vt_ref, e_refs, qt_ref, acc_ref, *, n_pairs, key_chunk, win_chunks,
                 key_start, bias, finish):
    tq = WIDTH // n_pairs
    fw = GROUP_W // n_pairs
    n_blocks = SEQ // tq
    ones_row = jnp.where(lax.broadcasted_iota(jnp.int32, (VT_ROWS - HEAD_DIM, key_chunk), 0) == 0, 1.0, 0.0)
    for c in range(SEQ // key_chunk):
        rows = slice(c * key_chunk, (c + 1) * key_chunk)
        vt = v_ref[0, rows, :].astype(F32).T.astype(BF16)
        for h in range(N_HEADS):
            vt_ref[c, h * VT_ROWS:h * VT_ROWS + HEAD_DIM, :] = vt[h * HEAD_DIM:(h + 1) * HEAD_DIM]
            vt_ref[c, h * VT_ROWS + HEAD_DIM:(h + 1) * VT_ROWS, :] = ones_row.astype(BF16)

    feat = lax.broadcasted_iota(jnp.int32, (GROUP_W, tq), 0)
    mx0 = jnp.full((SUBLANES, WIDTH), NEG, F32)

    def load_queries(i):
        qt = q_ref[0, pl.ds(pl.multiple_of(i * tq, tq), tq), :].astype(F32).T
        for p in range(n_pairs):
            keep = (feat >= p * fw) & (feat < (p + 1) * fw)
            qt_ref[:, p * tq:(p + 1) * tq] = jnp.where(keep, qt, 0.0).astype(BF16)

    pph = n_pairs // N_HEADS

    def scores_head(c, i, h, e_ref, mx):
        lanes = slice(h * PAIR_LANES, (h + 1) * PAIR_LANES)
        k0 = pl.multiple_of((key_start(i) + c) * key_chunk, key_chunk)
        s = _dot(k_ref[0, pl.ds(k0, key_chunk), :], qt_ref[:, lanes])
        b = bias(h, i, c)
        e = jnp.concatenate([s[:, p * tq:(p + 1) * tq] + b for p in range(pph)], axis=1)
        e_ref[pl.ds(pl.multiple_of(c * key_chunk, key_chunk), key_chunk), lanes] = e
        return jnp.maximum(mx, jnp.max(_sublane_groups(e), axis=0))

    def scores(c, i, e_ref, mx):
        parts = [scores_head(c, i, h, e_ref, mx[:, h * PAIR_LANES:(h + 1) * PAIR_LANES]) for h in range(N_HEADS)]
        return jnp.concatenate(parts, axis=1)

    def probs_head(c, i, h, e_ref, m):
        lanes = slice(h * PAIR_LANES, (h + 1) * PAIR_LANES)
        r0 = pl.multiple_of(c * key_chunk, key_chunk)
        pb = jnp.exp2(e_ref[pl.ds(r0, key_chunk), lanes] - m[:, lanes]).astype(BF16)
        acc_ref[h] += _dot(vt_ref[key_start(i) + c, h * VT_ROWS:(h + 1) * VT_ROWS, :], pb)

    def probs(c, i, e_ref, m):
        for h in range(N_HEADS):
            probs_head(c, i, h, e_ref, m)

    def finish_block(i, m):
        sums = jnp.concatenate([acc_ref[h, HEAD_DIM:HEAD_DIM + 1, :] for h in range(N_HEADS)], axis=1)
        finish(i, m, sums, acc_ref)

    load_queries(0)
    mx = lax.fori_loop(0, win_chunks, lambda c, mx: scores(c, 0, e_refs[0], mx), mx0, unroll=True)

    def step(i, slot, m):
        load_queries(i + 1)
        acc_ref[...] = jnp.zeros_like(acc_ref)

        def both(c, mx):
            parts = []
            for h in range(N_HEADS):
                parts.append(scores_head(c, i + 1, h, e_refs[1 - slot], mx[:, h * PAIR_LANES:(h + 1) * PAIR_LANES]))
                probs_head(c, i, h, e_refs[slot], m)
            return jnp.concatenate(parts, axis=1)

        mx = lax.fori_loop(0, win_chunks, both, mx0, unroll=True)
        finish_block(i, m)
        return jnp.max(mx, axis=0, keepdims=True)

    def two_steps(j, m):
        return step(2 * j + 1, 1, step(2 * j, 0, m))

    m = lax.fori_loop(0, n_blocks // 2 - 1, two_steps, jnp.max(mx, axis=0, keepdims=True))
    m = step(n_blocks - 2, 0, m)
    acc_ref[...] = jnp.zeros_like(acc_ref)

    def last(c, _):
        probs(c, n_blocks - 1, e_refs[1], m)
        return 0

    lax.fori_loop(0, win_chunks, last, 0, unroll=True)
    finish_block(n_blocks - 1, m)


def _attn_specs():
    def col(j):
        return pl.BlockSpec((1, SEQ, GROUP_W), lambda b: (b, 0, j))
    return [col(0), col(1), col(2), col(3)]


def _attn_scratch(n_pairs, key_chunk, win_chunks, tab_rows):
    tq = WIDTH // n_pairs
    return [pltpu.VMEM((N_HEADS, tab_rows, tq), F32),
            pltpu.VMEM((SEQ // key_chunk, N_HEADS * VT_ROWS, key_chunk), BF16),
            pltpu.VMEM((win_chunks * key_chunk, WIDTH), F32),
            pltpu.VMEM((win_chunks * key_chunk, WIDTH), F32),
            pltpu.VMEM((GROUP_W, WIDTH), BF16),
            pltpu.VMEM((N_HEADS, VT_ROWS, PAIR_LANES), F32)]


def _mask_values(v_ref, vm_ref):
    v = v_ref[0]
    for h in range(N_HEADS):
        vm_ref[h * SEQ:(h + 1) * SEQ, :] = jnp.where(_head_mask(h), v, jnp.zeros_like(v))


A_PAIRS = N_HEADS
A_TQ = WIDTH // A_PAIRS
A_CHUNK = 256
A_WIN = 3
A_FAR = DILATED_PATTERNS[-1][1]
A_NEAR_PATTERNS = DILATED_PATTERNS[:-1]
A_NEAR_REACH = max(r * (w // (2 * r)) for w, r in A_NEAR_PATTERNS)
assert A_NEAR_REACH <= A_CHUNK and A_TQ == A_CHUNK
A_TAB_OFF = 2 * A_CHUNK
A_TAB_ROWS = A_TAB_OFF + A_WIN * A_CHUNK
A_CLASS = SEQ // A_FAR
A_PART = GROUP_W + LANES


def _a_key_start(i):
    return jnp.clip(i - 1, 0, SEQ // A_CHUNK - A_WIN)


def _a_far_partials(q_ref, k_ref, v_ref, tab_ref, x_ref, part_ref):
    tiles = GROUP_W // LANES

    n = A_CLASS
    for a, ref in enumerate((q_ref, k_ref, v_ref)):
        for j in range(tiles):
            x = ref[0, :, j * LANES:(j + 1) * LANES].astype(F32)
            x_ref[a * tiles + j] = jnp.swapaxes(x.reshape(n, A_FAR, LANES), 0, 1)

    feat = lax.broadcasted_iota(jnp.int32, (GROUP_W, n), 0)
    ones_rows = jnp.where(lax.broadcasted_iota(jnp.int32, (VT_ROWS - HEAD_DIM, n), 0) == 0, 1.0, 0.0).astype(BF16)
    pad = jnp.zeros((LANES - 2 * N_HEADS, n), F32)
    for rho in range(A_FAR):
        cls = pl.ds(rho, n, stride=A_FAR)
        cols = lambda a: jnp.concatenate([x_ref[a * tiles + j, rho] for j in range(tiles)], axis=1)
        qt = cols(0).T
        qt_all = jnp.concatenate(
            [jnp.where((feat >= h * HEAD_DIM) & (feat < (h + 1) * HEAD_DIM), qt, 0.0) for h in range(N_HEADS)],
            axis=1).astype(BF16)
        s = _dot(cols(1).astype(BF16), qt_all)
        e = jnp.concatenate([s[:, h * n:(h + 1) * n] + tab_ref[h] for h in range(N_HEADS)], axis=1)
        m = jnp.max(e, axis=0, keepdims=True)
        pb = jnp.exp2(e - m).astype(BF16)
        vt = cols(2).T.astype(BF16)
        accs = [_dot(jnp.concatenate([vt[h * HEAD_DIM:(h + 1) * HEAD_DIM], ones_rows], axis=0),
                     pb[:, h * n:(h + 1) * n]) for h in range(N_HEADS)]
        stats = jnp.concatenate([m[:, h * n:(h + 1) * n] for h in range(N_HEADS)]
                                + [a[HEAD_DIM:HEAD_DIM + 1] for a in accs] + [pad], axis=0)
        rec = jnp.concatenate([jnp.concatenate([a[0:HEAD_DIM] for a in accs], axis=0).T, stats.T], axis=1)
        for j in range(A_PART // LANES):
            part_ref[j, cls, :] = rec[:, j * LANES:(j + 1) * LANES]


def _mixer_a_kernel(q_ref, k_ref, v_ref, g_ref, o_ref, tab_ref, vt_ref, e0_ref, e1_ref, qt_ref, acc_ref,
                    far_tab_ref, x_ref, part_ref, *, slopes):
    @pl.when(pl.program_id(0) == 0)
    def _():
        _build_toeplitz_t(tab_ref, slopes, A_NEAR_PATTERNS, A_TAB_OFF)
        _build_toeplitz_t(far_tab_ref, slopes, DILATED_PATTERNS[-1:], 0, unit=A_FAR)

    _a_far_partials(q_ref, k_ref, v_ref, far_tab_ref, x_ref, part_ref)

    def bias(h, i, c):
        t0 = pl.multiple_of((_a_key_start(i) + c - i) * A_CHUNK + A_TAB_OFF, A_CHUNK)
        return tab_ref[h, pl.ds(t0, A_CHUNK), :]

    def finish(i, m, sums, acc_ref):
        q0 = pl.multiple_of(i * A_TQ, A_TQ)
        far = jnp.concatenate([part_ref[j, pl.ds(q0, A_TQ), :] for j in range(GROUP_W // LANES)], axis=1).T
        stats = part_ref[GROUP_W // LANES, pl.ds(q0, A_TQ), :].T
        outs = []
        for h in range(N_HEADS):
            lanes = slice(h * A_TQ, (h + 1) * A_TQ)
            m_far = stats[h:h + 1]
            top = jnp.maximum(m[:, lanes], m_far)
            w_near = jnp.exp2(m[:, lanes] - top)
            w_far = jnp.exp2(m_far - top)
            denom = sums[:, lanes] * w_near + stats[N_HEADS + h:N_HEADS + h + 1] * w_far
            outs.append((acc_ref[h, 0:HEAD_DIM, :] * w_near + far[h * HEAD_DIM:(h + 1) * HEAD_DIM] * w_far) / denom)
        g = g_ref[0, pl.ds(q0, A_TQ), :].astype(F32)
        o_ref[0, pl.ds(q0, A_TQ), :] = (jnp.concatenate(outs, axis=0).T * _silu(g)).astype(BF16)

    _attention_t(q_ref, k_ref, v_ref, vt_ref, (e0_ref, e1_ref), qt_ref, acc_ref, n_pairs=A_PAIRS,
                 key_chunk=A_CHUNK, win_chunks=A_WIN, key_start=_a_key_start, bias=bias, finish=finish)


def _mixer_a(pa, slopes):
    bsz = pa.shape[0]
    return pl.pallas_call(
        functools.partial(_mixer_a_kernel, slopes=slopes),
        grid=(bsz,),
        in_specs=_attn_specs(),
        out_specs=pl.BlockSpec((1, SEQ, GROUP_W), lambda b: (b, 0, 0)),
        out_shape=jax.ShapeDtypeStruct((bsz, SEQ, GROUP_W), BF16),
        scratch_shapes=_attn_scratch(A_PAIRS, A_CHUNK, A_WIN, A_TAB_ROWS) + [
            pltpu.VMEM((N_HEADS, A_CLASS, A_CLASS), F32),
            pltpu.VMEM((3 * GROUP_W // LANES, A_FAR, A_CLASS, LANES), F32),
            pltpu.VMEM((A_PART // LANES, SEQ, LANES), F32)],
        compiler_params=_params(1),
        name="mixer_a_dilated",
    )(pa, pa, pa, pa)


B_PAIRS = 2 * N_HEADS
B_TQ = WIDTH // B_PAIRS


def _mixer_b_kernel(q_ref, k_ref, v_ref, g_ref, lam_ref, nw_ref, o_ref, tab_ref, vt_ref, e0_ref, e1_ref, qt_ref,
                    acc_ref, *, slopes, lam_init):
    tq = B_TQ

    @pl.when(pl.program_id(0) == 0)
    def _():
        _build_toeplitz_t(tab_ref, slopes, None, SEQ - tq)

    def bias(h, i, c):
        return tab_ref[h, pl.ds(pl.multiple_of(SEQ - tq - i * tq + c * KEY_CHUNK, LANES), KEY_CHUNK), :]

    lv = lam_ref[...]
    lam = (jnp.exp(jnp.sum(lv[0:1] * lv[1:2], axis=-1, keepdims=True))
           - jnp.exp(jnp.sum(lv[2:3] * lv[3:4], axis=-1, keepdims=True)) + lam_init)
    nw = nw_ref[...]

    def finish(i, m, sums, acc_ref):
        q0 = pl.multiple_of(i * tq, tq)
        inv = 1.0 / sums
        outs = []
        for h in range(N_HEADS):
            acc = acc_ref[h, 0:HEAD_DIM, :]
            o = (acc[:, 0:tq] * inv[:, 2 * h * tq:(2 * h + 1) * tq]
                 - lam * (acc[:, tq:2 * tq] * inv[:, (2 * h + 1) * tq:(2 * h + 2) * tq]))
            ms = jnp.mean(o * o, axis=0, keepdims=True)
            outs.append(o * lax.rsqrt(ms + EPS))
        on = jnp.concatenate(outs, axis=0).T * nw * (1.0 - lam_init)
        g = g_ref[0, pl.ds(q0, tq), :].astype(F32)
        o_ref[0, pl.ds(q0, tq), :] = (on * _silu(g)).astype(BF16)

    _attention_t(q_ref, k_ref, v_ref, vt_ref, (e0_ref, e1_ref), qt_ref, acc_ref, n_pairs=B_PAIRS,
                 key_chunk=KEY_CHUNK, win_chunks=SEQ // KEY_CHUNK, key_start=lambda i: 0, bias=bias, finish=finish)


def _mixer_b(pb, lam_p, nw256, slopes, lam_init):
    bsz = pb.shape[0]
    return pl.pallas_call(
        functools.partial(_mixer_b_kernel, slopes=slopes, lam_init=lam_init),
        grid=(bsz,),
        in_specs=_attn_specs() + [pl.BlockSpec((4, DIFF_HEAD_DIM), lambda b: (0, 0)),
                                  pl.BlockSpec((1, GROUP_W), lambda b: (0, 0))],
        out_specs=pl.BlockSpec((1, SEQ, GROUP_W), lambda b: (b, 0, 0)),
        out_shape=jax.ShapeDtypeStruct((bsz, SEQ, GROUP_W), BF16),
        scratch_shapes=_attn_scratch(B_PAIRS, KEY_CHUNK, SEQ // KEY_CHUNK, 2 * SEQ - B_TQ),
        compiler_params=_params(1),
        name="mixer_b_diff",
    )(pb, pb, pb, pb, lam_p, nw256)


CONV_TILE = 256
CONV_HALO = SUBLANES
N_CHUNK = SEQ // SSM_CHUNK


def _expand_heads(x, expand3):
    hi = x.astype(BF16)
    r1 = x - hi.astype(F32)
    mid = r1.astype(BF16)
    lo = (r1 - mid.astype(F32)).astype(BF16)
    return _dot(jnp.concatenate([hi, mid, lo], axis=1), expand3)


def _cumsum_rows(a):
    row = lax.broadcasted_iota(jnp.int32, a.shape, 0)
    s = 1
    while s < a.shape[0]:
        a = a + jnp.where(row >= s, pltpu.roll(a, s, 0), 0.0)
        s *= 2
    return a


def _mixer_c_kernel(p_ref, dt_ref, cw_ref, cb_ref, alog_ref, alogx_ref, dtb_ref, dskip_ref, nw_ref, exp_ref,
                    o_ref, xpad, xc, y_s, sb_s, cs_s, db_s):
    L = SSM_CHUNK
    zero_rows = jnp.zeros((CONV_HALO, SSM_XBC), F32)
    xpad[0:CONV_HALO, :] = zero_rows
    xpad[CONV_HALO + SEQ:CONV_HALO + SEQ + CONV_HALO, :] = zero_rows

    def fill(i, _):
        r0 = pl.multiple_of(i * CONV_TILE, CONV_TILE)
        xpad[pl.ds(CONV_HALO + r0, CONV_TILE), :] = p_ref[0, pl.ds(r0, CONV_TILE), GROUP_W:].astype(F32)
        return 0

    lax.fori_loop(0, SEQ // CONV_TILE, fill, 0)

    def conv(i, _):
        r0 = pl.multiple_of(i * CONV_TILE, CONV_TILE)
        rows = CONV_TILE + 2 * CONV_HALO
        win = xpad[pl.ds(r0, rows), :]
        acc = jnp.zeros((CONV_TILE, SSM_XBC), F32) + cb_ref[...]
        for j in range(SSM_CONV):
            back = (SSM_CONV // 2 - j) % rows
            tap = win if back == 0 else pltpu.roll(win, back, 0)
            acc = acc + cw_ref[j:j + 1, :] * tap[CONV_HALO:CONV_HALO + CONV_TILE, :]
        xc[pl.ds(r0, CONV_TILE), :] = _silu(acc)
        return 0

    lax.fori_loop(0, SEQ // CONV_TILE, conv, 0)

    a_neg = -jnp.exp(alog_ref[...])
    a_neg_x = -jnp.exp(alogx_ref[...])
    expand = exp_ref[...]
    li = lax.broadcasted_iota(jnp.int32, (L, L), 0)
    si = lax.broadcasted_iota(jnp.int32, (L, L), 1)
    lower = si <= li
    upper = si >= li
    hmasks = [_head_mask(h) for h in range(N_HEADS)]

    def chunk_terms(t0):
        dtr = dt_ref[0, pl.ds(t0, L), :] + dtb_ref[...]
        dt = jnp.maximum(dtr, 0.0) + jnp.log(1.0 + jnp.exp(-jnp.abs(dtr)))
        a = dt * a_neg
        ainc = _cumsum_rows(a)
        aexc = ainc - a
        return dt, ainc, aexc

    def fwd(c, hf):
        t0 = pl.multiple_of(c * L, L)
        dt, ainc, aexc = chunk_terms(t0)
        both_x = _expand_heads(jnp.concatenate([dt, ainc], axis=0), expand)
        dt_x = both_x[0:L]
        ainc_x = both_x[L:2 * L]
        aexc_x = ainc_x - dt_x * a_neg_x
        ainc_t = ainc.T
        aexc_t = aexc.T
        xs = xc[pl.ds(t0, L), 0:GROUP_W]
        bm = xc[pl.ds(t0, L), GROUP_W:2 * GROUP_W]
        cm = xc[pl.ds(t0, L), 2 * GROUP_W:3 * GROUP_W]
        xf = xs * dt_x[:, 0:GROUP_W]
        xb = xs * dt_x[:, GROUP_W:2 * GROUP_W]
        xcat = jnp.concatenate([xf, xb], axis=0).astype(BF16)
        tot_f = ainc_x[L - 1:L, 0:GROUP_W]
        tot_b = ainc_x[L - 1:L, GROUP_W:2 * GROUP_W]
        y = xs * dskip_ref[...]
        cbs = []
        for g in range(SSM_GROUPS):
            gs = slice(g * SSM_STATE, (g + 1) * SSM_STATE)
            cbs.append(_dot_nt(cm[:, gs].astype(BF16), bm[:, gs].astype(BF16)))
        for h in range(N_HEADS):
            cb = cbs[h // (N_HEADS // SSM_GROUPS)]
            col_f = ainc[:, h:h + 1]
            row_f = ainc_t[h:h + 1, :]
            col_b = aexc[:, N_HEADS + h:N_HEADS + h + 1]
            row_b = aexc_t[N_HEADS + h:N_HEADS + h + 1, :]
            lf = jnp.exp(jnp.where(lower, col_f - row_f, NEG))
            ub = jnp.exp(jnp.where(upper, row_b - col_b, NEG))
            mcat = jnp.concatenate([cb * lf, cb * ub], axis=1).astype(BF16)
            y = y + jnp.where(hmasks[h], _dot(mcat, xcat), 0.0)
        wf = (jnp.exp(tot_f - ainc_x[:, 0:GROUP_W]) * xf).astype(BF16)
        wb = (jnp.exp(aexc_x[:, GROUP_W:2 * GROUP_W]) * xb).astype(BF16)
        ef = jnp.exp(ainc_x[:, 0:GROUP_W])
        cs_s[pl.ds(t0, L), :] = jnp.exp(tot_b - aexc_x[:, GROUP_W:2 * GROUP_W])
        db_s[pl.ds(pl.multiple_of(c * SUBLANES, SUBLANES), SUBLANES), :] = jnp.broadcast_to(
            jnp.exp(tot_b), (SUBLANES, GROUP_W))
        dec_f = jnp.exp(tot_f)
        hf_new = []
        yoff = []
        for g in range(SSM_GROUPS):
            gs = slice(g * SSM_STATE, (g + 1) * SSM_STATE)
            bt = bm[:, gs].T.astype(BF16)
            yoff.append(_dot(cm[:, gs].astype(BF16), hf[g].astype(BF16)))
            hf_new.append(dec_f[:, gs] * hf[g] + _dot(bt, wf[:, gs]))
            sb_s[pl.ds(t0, L), gs] = _dot(bt, wb[:, gs])
        y = y + jnp.concatenate(yoff, axis=1) * ef
        y_s[pl.ds(t0, L), :] = y
        return tuple(hf_new)

    h0 = tuple(jnp.zeros((SSM_STATE, SSM_STATE), F32) for _ in range(SSM_GROUPS))
    lax.fori_loop(0, N_CHUNK, fwd, h0, unroll=2)

    def bwd(i, hb):
        c = N_CHUNK - 1 - i
        t0 = pl.multiple_of(c * L, L)
        cm = xc[pl.ds(t0, L), 2 * GROUP_W:3 * GROUP_W]
        dec_b = db_s[pl.ds(pl.multiple_of(c * SUBLANES, SUBLANES), 1), :]
        yoff = []
        hb_new = []
        for g in range(SSM_GROUPS):
            gs = slice(g * SSM_STATE, (g + 1) * SSM_STATE)
            yoff.append(_dot(cm[:, gs].astype(BF16), hb[g].astype(BF16)))
            hb_new.append(dec_b[:, gs] * hb[g] + sb_s[pl.ds(t0, L), gs])
        y_s[pl.ds(t0, L), :] = y_s[pl.ds(t0, L), :] + jnp.concatenate(yoff, axis=1) * cs_s[pl.ds(t0, L), :]
        return tuple(hb_new)

    lax.fori_loop(0, N_CHUNK, bwd, h0, unroll=4)

    def fin(i, _):
        r0 = pl.multiple_of(i * CONV_TILE, CONV_TILE)
        z = p_ref[0, pl.ds(r0, CONV_TILE), 0:GROUP_W].astype(F32)
        y = y_s[pl.ds(r0, CONV_TILE), :] * _silu(z)
        parts = []
        for g in range(SSM_GROUPS):
            yg = y[:, g * SSM_STATE:(g + 1) * SSM_STATE]
            parts.append(yg * lax.rsqrt(jnp.mean(yg * yg, axis=-1, keepdims=True) + EPS))
        o_ref[0, pl.ds(r0, CONV_TILE), :] = (jnp.concatenate(parts, axis=1) * nw_ref[...]).astype(BF16)
        return 0

    lax.fori_loop(0, SEQ // CONV_TILE, fin, 0)


def _head_expand_matrix():
    j = lax.broadcasted_iota(jnp.int32, (LANES, 2 * GROUP_W), 0)
    c = lax.broadcasted_iota(jnp.int32, (LANES, 2 * GROUP_W), 1)
    return jnp.tile((j == c // HEAD_DIM).astype(BF16), (3, 1))


def _mixer_c(pc, pdt, conv_w, conv_b, a_log, dt_bias, d_skip, norm_w):
    bsz = pc.shape[0]
    pad8 = lambda v: jnp.pad(v.reshape(1, 2 * N_HEADS), ((0, 0), (0, LANES - 2 * N_HEADS)))
    small = lambda shape: pl.BlockSpec(shape, lambda b: (0,) * len(shape))
    return pl.pallas_call(
        _mixer_c_kernel,
        grid=(bsz,),
        in_specs=[pl.BlockSpec((1, SEQ, 1024), lambda b: (b, 0, 0)),
                  pl.BlockSpec((1, SEQ, DT_PAD), lambda b: (b, 0, 0)),
                  small((SSM_CONV, SSM_XBC)), small((1, SSM_XBC)),
                  small((1, LANES)), small((1, 2 * GROUP_W)), small((1, LANES)),
                  small((1, GROUP_W)), small((1, GROUP_W)),
                  small((3 * LANES, 2 * GROUP_W))],
        out_specs=pl.BlockSpec((1, SEQ, GROUP_W), lambda b: (b, 0, 0)),
        out_shape=jax.ShapeDtypeStruct((bsz, SEQ, GROUP_W), BF16),
        scratch_shapes=[pltpu.VMEM((SEQ + 2 * CONV_HALO, SSM_XBC), F32),
                        pltpu.VMEM((SEQ, SSM_XBC), F32),
                        pltpu.VMEM((SEQ, GROUP_W), F32),
                        pltpu.VMEM((SEQ, GROUP_W), F32),
                        pltpu.VMEM((SEQ, GROUP_W), F32),
                        pltpu.VMEM((N_CHUNK * SUBLANES, GROUP_W), F32)],
        compiler_params=_params(1),
        name="mixer_c_ssd",
    )(pc, pdt, conv_w, conv_b.reshape(1, SSM_XBC), pad8(a_log),
      jnp.repeat(a_log.reshape(-1), HEAD_DIM).reshape(1, 2 * GROUP_W), pad8(dt_bias),
      jnp.repeat(d_skip, HEAD_DIM).reshape(1, GROUP_W), norm_w.reshape(1, GROUP_W),
      _head_expand_matrix())


NA_ROWS = SEQ // GRID_W
NA_DR = 2 * NA_WIN_H - 1
NA_DC = 2 * NA_WIN_W - 1
NA_PAIRS = NA_DR - 1


def _na_bias_kernel(rpb_ref, o_ref):
    lh = pl.program_id(0)
    cq = lax.broadcasted_iota(jnp.int32, (GRID_W, LANES), 0)
    lane = lax.broadcasted_iota(jnp.int32, (GRID_W, LANES), 1)
    second = lane >= GRID_W
    ck = jnp.where(second, lane - GRID_W, lane)
    cs = jnp.clip(cq - NA_WIN_W // 2, 0, GRID_W - NA_WIN_W)
    inside = (ck >= cs) & (ck < cs + NA_WIN_W)
    dc = ck - cq + NA_WIN_W - 1
    base = lh * (NA_DR * NA_DC)
    rows = []
    for dr in range(NA_DR):
        acc = jnp.zeros((GRID_W, LANES), F32)
        for j in range(NA_DC):
            acc = jnp.where(dc == j, rpb_ref[base + dr * NA_DC + j], acc)
        rows.append(acc)
    for p in range(NA_PAIRS):
        o_ref[0, p] = jnp.where(inside, jnp.where(second, rows[p + 1], rows[p]), NEG)


def _na_bias(na_rpb):
    n = DEPTH * N_HEADS
    return pl.pallas_call(
        _na_bias_kernel,
        grid=(n,),
        in_specs=[pl.BlockSpec(memory_space=pltpu.SMEM)],
        out_specs=pl.BlockSpec((1, NA_PAIRS, GRID_W, LANES), lambda i: (i, 0, 0, 0)),
        out_shape=jax.ShapeDtypeStruct((n, NA_PAIRS, GRID_W, LANES), F32),
        compiler_params=_params(1),
        name="na_bias_table",
    )(na_rpb.reshape(-1))


NA_UNROLL = 8


def _mixer_d_kernel(q_ref, k_ref, v_ref, g_ref, bias_ref, o_ref, vm_ref):
    nkeys = NA_WIN_H * GRID_W
    _mask_values(v_ref, vm_ref)

    def score_row(r):
        rs = jnp.clip(r - NA_WIN_H // 2, 0, NA_ROWS - NA_WIN_H)
        k0 = pl.multiple_of(rs * GRID_W, GRID_W)
        q = q_ref[0, pl.ds(pl.multiple_of(r * GRID_W, GRID_W), GRID_W), :]
        qs = jnp.concatenate([jnp.where(_head_mask(h), q, jnp.zeros_like(q)) for h in range(N_HEADS)], axis=0)
        return _dot_nt(qs, k_ref[0, pl.ds(k0, nkeys), :])

    def prob_row(r, s_all):
        delta = r - jnp.clip(r - NA_WIN_H // 2, 0, NA_ROWS - NA_WIN_H)
        ps, invs = [], []
        for h in range(N_HEADS):
            bias = jnp.concatenate(
                [bias_ref[h, 2 * i - delta + NA_WIN_H - 1] for i in range(NA_WIN_H // 2)], axis=-1)
            e = s_all[h * GRID_W:(h + 1) * GRID_W] + bias
            m = jnp.max(e, axis=-1, keepdims=True)
            p = jnp.exp(e - m)
            invs.append(1.0 / jnp.sum(p, axis=-1, keepdims=True))
            ps.append(p.astype(BF16))
        inv = jnp.broadcast_to(invs[N_HEADS - 1], (GRID_W, GROUP_W))
        for h in range(N_HEADS - 2, -1, -1):
            inv = jnp.where(_head_mask(h), invs[h], inv)
        return jnp.concatenate(ps, axis=-1), inv

    def value_row(r, pcat, inv):
        rs = jnp.clip(r - NA_WIN_H // 2, 0, NA_ROWS - NA_WIN_H)
        k0 = pl.multiple_of(rs * GRID_W, GRID_W)
        q0 = pl.multiple_of(r * GRID_W, GRID_W)
        vcat = jnp.concatenate([vm_ref[pl.ds(pl.multiple_of(h * SEQ + k0, GRID_W), nkeys), :]
                                for h in range(N_HEADS)], axis=0)
        acc = _dot(pcat, vcat)
        g = g_ref[0, pl.ds(q0, GRID_W), :].astype(F32)
        o_ref[0, pl.ds(q0, GRID_W), :] = (acc * inv * _silu(g)).astype(BF16)

    def rstep(i, _):
        rows = [i * NA_UNROLL + u for u in range(NA_UNROLL)]
        s_next = score_row(rows[0])
        for u, r in enumerate(rows):
            s_cur = s_next
            if u + 1 < NA_UNROLL:
                s_next = score_row(rows[u + 1])
            value_row(r, *prob_row(r, s_cur))
        return 0

    lax.fori_loop(0, NA_ROWS // NA_UNROLL, rstep, 0)


def _mixer_d(pd, bias, l):
    bsz = pd.shape[0]
    return pl.pallas_call(
        _mixer_d_kernel,
        grid=(bsz,),
        in_specs=_attn_specs() + [pl.BlockSpec((N_HEADS, NA_PAIRS, GRID_W, LANES), lambda b: (l, 0, 0, 0))],
        out_specs=pl.BlockSpec((1, SEQ, GROUP_W), lambda b: (b, 0, 0)),
        out_shape=jax.ShapeDtypeStruct((bsz, SEQ, GROUP_W), BF16),
        scratch_shapes=[pltpu.VMEM((N_HEADS * SEQ, GROUP_W), BF16)],
        compiler_params=_params(1),
        name="mixer_d_neighbourhood",
    )(pd, pd, pd, pd, bias)


def kernel(x, c, norm_w, ada_w, ada_b, w_in, diff_lambda, diff_norm_w, conv_w, conv_b, ssm_a_log,
           ssm_dt_bias, ssm_d, ssm_norm_w, na_rpb, w_out, final_norm_w):
    bsz = x.shape[0]
    n = 2 * N_HEADS
    slopes = [2.0 ** (-8.0 * i / n) for i in range(1, n + 1)]
    slopes_a, slopes_b = tuple(slopes[0::2]), tuple(slopes[1::2])
    mod4 = _adaln(c, ada_w, ada_b).reshape(DEPTH, bsz, 1, 3 * D_MODEL)
    na_bias = _na_bias(na_rpb)
    w_main, w_dt = _prep_w_in(w_in)
    norm_w3 = norm_w.reshape(DEPTH, 1, D_MODEL)
    final_w = final_norm_w.reshape(1, D_MODEL)
    pa, pb, pc, pd, pdt = _inproj(x, mod4, norm_w3, w_main, w_dt, 0)
    for l in range(DEPTH):
        lam_init = 0.8 - 0.6 * math.exp(-0.3 * l)
        ya = _mixer_a(pa, slopes_a)
        yb = _mixer_b(pb, diff_lambda[l], jnp.tile(diff_norm_w[l], N_HEADS).reshape(1, GROUP_W),
                      slopes_b, lam_init)
        yc = _mixer_c(pc, pdt, conv_w[l], conv_b[l], ssm_a_log[l], ssm_dt_bias[l], ssm_d[l], ssm_norm_w[l])
        yd = _mixer_d(pd, na_bias, l)
        if l + 1 < DEPTH:
            x, pa, pb, pc, pd, pdt = _midproj((ya, yb, yc, yd), w_out, x, mod4, norm_w3, w_main, w_dt, l)
        else:
            x = _outproj((ya, yb, yc, yd), w_out, x, mod4, final_w, l)
    return x
```

```python
import functools
import math

import jax
import jax.numpy as jnp
from jax import lax
from jax.experimental import pallas as pl
from jax.experimental.pallas import tpu as pltpu

D_MODEL = 1024
SEQ = 2048
DEPTH = 2
HEAD_DIM = 64
GROUP_W = 256
N_HEADS = 4
EPS = 1e-6
DILATED_PATTERNS = ((128, 1), (512, 4), (2048, 16))
DIFF_HEAD_DIM = 32
SSM_GROUPS = 2
SSM_STATE = 128
SSM_CONV = 5
SSM_CHUNK = 128
SSM_XBC = 768
GRID_W = 64
NA_WIN_H = 8
NA_WIN_W = 16
D_IN = 13 * GROUP_W + SSM_XBC + 2 * N_HEADS

LANES = 128
SUBLANES = 8
VMEM_LIMIT = 56 * 1024 * 1024

NEG = -1e30
LOG2E = math.log2(math.e)
F32 = jnp.float32
BF16 = jnp.bfloat16
HIGHEST = lax.Precision.HIGHEST

ROW_TILE = 512
DT_PAD = LANES


def _silu(x):
    return x / (1.0 + jnp.exp(-x))


def _dot_nt(a, b):
    return lax.dot_general(a, b, (((1,), (1,)), ((), ())), preferred_element_type=F32)


def _dot(a, b):
    return jnp.dot(a, b, preferred_element_type=F32)


def _params(n_grid):
    return pltpu.CompilerParams(dimension_semantics=("arbitrary",) * n_grid,
                                vmem_limit_bytes=VMEM_LIMIT)


def _head_mask(h, width=HEAD_DIM, total=GROUP_W):
    lane = lax.broadcasted_iota(jnp.int32, (1, total), 1)
    return (lane >= h * width) & (lane < (h + 1) * width)


def _mod_kernel(c_ref, w_ref, b_ref, o_ref):
    c = c_ref[...]
    o_ref[0] = jnp.dot(_silu(c), w_ref[0], precision=HIGHEST,
                       preferred_element_type=F32) + b_ref[0]


def _adaln(c, ada_w, ada_b):
    bsz = c.shape[0]
    tn = 768
    return pl.pallas_call(
        _mod_kernel,
        grid=(DEPTH, 3 * D_MODEL // tn),
        in_specs=[pl.BlockSpec((bsz, D_MODEL), lambda l, j: (0, 0)),
                  pl.BlockSpec((1, D_MODEL, tn), lambda l, j: (l, 0, j)),
                  pl.BlockSpec((1, 1, tn), lambda l, j: (l, 0, j))],
        out_specs=pl.BlockSpec((1, bsz, tn), lambda l, j: (l, 0, j)),
        out_shape=jax.ShapeDtypeStruct((DEPTH, bsz, 3 * D_MODEL), F32),
        compiler_params=_params(2),
        name="adaln_mod",
    )(c, ada_w, ada_b.reshape(DEPTH, 1, 3 * D_MODEL))


W_MAIN = 16 * GROUP_W
DT_COL0 = 9 * GROUP_W + SSM_XBC


def _prep_w_in(w_in):
    scale = [1.0] * W_MAIN
    for c0, s in ((0, HEAD_DIM ** -0.5 * LOG2E), (4 * GROUP_W, DIFF_HEAD_DIM ** -0.5 * LOG2E),
                  (12 * GROUP_W, HEAD_DIM ** -0.5)):
        scale[c0:c0 + GROUP_W] = [s] * GROUP_W
    sc = jnp.asarray(scale, F32)
    d0 = DT_COL0 + 2 * N_HEADS
    w_main = jnp.concatenate([(w_in[:, :, :DT_COL0] * sc[:DT_COL0]).astype(BF16),
                              (w_in[:, :, d0:] * sc[DT_COL0:]).astype(BF16)], axis=2)
    w_dt = jnp.pad(w_in[:, :, DT_COL0:d0].astype(BF16), ((0, 0), (0, 0), (0, DT_PAD - 2 * N_HEADS)))
    return w_main, w_dt


def _inproj_kernel(x_ref, mod_ref, nw_ref, w_ref, wdt_ref, pa_ref, pb_ref, pc_ref, pd_ref, pdt_ref):
    x = x_ref[0]
    shift = mod_ref[:, 0:D_MODEL]
    scale = mod_ref[:, D_MODEL:2 * D_MODEL]
    y = x * lax.rsqrt(jnp.mean(x * x, axis=-1, keepdims=True) + EPS) * nw_ref[...]
    h = (y * (1.0 + scale) + shift).astype(BF16)
    for i, ref in enumerate((pa_ref, pb_ref, pc_ref, pd_ref)):
        ref[0] = _dot(h, w_ref[:, i * 1024:(i + 1) * 1024]).astype(BF16)
    pdt_ref[0] = _dot(h, wdt_ref[...])


def _inproj(x, mod4, norm_w, w_main, w_dt, l):
    bsz = x.shape[0]
    row = lambda b, t: (b, t, 0)
    big = pl.BlockSpec((1, ROW_TILE, 1024), row)
    return pl.pallas_call(
        _inproj_kernel,
        grid=(bsz, SEQ // ROW_TILE),
        in_specs=[pl.BlockSpec((1, ROW_TILE, D_MODEL), row),
                  pl.BlockSpec((None, None, 1, 3 * D_MODEL), lambda b, t: (l, b, 0, 0)),
                  pl.BlockSpec((None, 1, D_MODEL), lambda b, t: (l, 0, 0)),
                  pl.BlockSpec((None, D_MODEL, W_MAIN), lambda b, t: (l, 0, 0)),
                  pl.BlockSpec((None, D_MODEL, DT_PAD), lambda b, t: (l, 0, 0))],
        out_specs=[big, big, big, big, pl.BlockSpec((1, ROW_TILE, DT_PAD), row)],
        out_shape=[jax.ShapeDtypeStruct((bsz, SEQ, 1024), BF16)] * 4
        + [jax.ShapeDtypeStruct((bsz, SEQ, DT_PAD), F32)],
        compiler_params=_params(2),
        name="inproj",
    )(x, mod4, norm_w, w_main, w_dt)


def _outproj_kernel(ya_ref, yb_ref, yc_ref, yd_ref, w_ref, x_ref, mod_ref, fw_ref, o_ref, wb_ref, *, final):
    @pl.when((pl.program_id(0) == 0) & (pl.program_id(1) == 0))
    def _():
        wb_ref[...] = w_ref[...].astype(BF16)

    acc = None
    for i, ref in enumerate((ya_ref, yb_ref, yc_ref, yd_ref)):
        part = _dot(ref[0], wb_ref[i * GROUP_W:(i + 1) * GROUP_W, :])
        acc = part if acc is None else acc + part
    gate = mod_ref[:, 2 * D_MODEL:3 * D_MODEL]
    xn = x_ref[0] + gate * acc
    if final:
        xn = xn * lax.rsqrt(jnp.mean(xn * xn, axis=-1, keepdims=True) + EPS) * fw_ref[...]
    o_ref[0] = xn


def _outproj(ys, w_out, x, mod4, final_w, l):
    bsz = x.shape[0]
    final = l == DEPTH - 1
    row = lambda b, t: (b, t, 0)
    yspec = pl.BlockSpec((1, ROW_TILE, GROUP_W), row)
    return pl.pallas_call(
        functools.partial(_outproj_kernel, final=final),
        grid=(bsz, SEQ // ROW_TILE),
        in_specs=[yspec, yspec, yspec, yspec,
                  pl.BlockSpec((None, D_MODEL, D_MODEL), lambda b, t: (l, 0, 0)),
                  pl.BlockSpec((1, ROW_TILE, D_MODEL), row),
                  pl.BlockSpec((None, None, 1, 3 * D_MODEL), lambda b, t: (l, b, 0, 0)),
                  pl.BlockSpec((1, D_MODEL), lambda b, t: (0, 0))],
        out_specs=pl.BlockSpec((1, ROW_TILE, D_MODEL), row),
        out_shape=jax.ShapeDtypeStruct((bsz, SEQ, D_MODEL), F32),
        scratch_shapes=[pltpu.VMEM((D_MODEL, D_MODEL), BF16)],
        compiler_params=_params(2),
        name="outproj_final" if final else "outproj",
    )(*ys, w_out, x, mod4, final_w)


def _midproj_kernel(ya_ref, yb_ref, yc_ref, yd_ref, wo_ref, x_ref, mod_ref, modn_ref, nw_ref, w_ref, wdt_ref,
                    xo_ref, pa_ref, pb_ref, pc_ref, pd_ref, pdt_ref, wb_ref):
    @pl.when((pl.program_id(0) == 0) & (pl.program_id(1) == 0))
    def _():
        wb_ref[...] = wo_ref[...].astype(BF16)

    acc = None
    for i, ref in enumerate((ya_ref, yb_ref, yc_ref, yd_ref)):
        part = _dot(ref[0], wb_ref[i * GROUP_W:(i + 1) * GROUP_W, :])
        acc = part if acc is None else acc + part
    x = x_ref[0] + mod_ref[:, 2 * D_MODEL:3 * D_MODEL] * acc
    xo_ref[0] = x
    shift = modn_ref[:, 0:D_MODEL]
    scale = modn_ref[:, D_MODEL:2 * D_MODEL]
    y = x * lax.rsqrt(jnp.mean(x * x, axis=-1, keepdims=True) + EPS) * nw_ref[...]
    h = (y * (1.0 + scale) + shift).astype(BF16)
    for i, ref in enumerate((pa_ref, pb_ref, pc_ref, pd_ref)):
        ref[0] = _dot(h, w_ref[:, i * 1024:(i + 1) * 1024]).astype(BF16)
    pdt_ref[0] = _dot(h, wdt_ref[...])


def _midproj(ys, w_out, x, mod4, norm_w, w_main, w_dt, l):
    bsz = x.shape[0]
    row = lambda b, t: (b, t, 0)
    yspec = pl.BlockSpec((1, ROW_TILE, GROUP_W), row)
    big = pl.BlockSpec((1, ROW_TILE, 1024), row)
    xspec = pl.BlockSpec((1, ROW_TILE, D_MODEL), row)
    once = pl.Buffered(1)
    return pl.pallas_call(
        _midproj_kernel,
        grid=(bsz, SEQ // ROW_TILE),
        in_specs=[yspec, yspec, yspec, yspec,
                  pl.BlockSpec((None, D_MODEL, D_MODEL), lambda b, t: (l, 0, 0), pipeline_mode=once),
                  xspec,
                  pl.BlockSpec((None, None, 1, 3 * D_MODEL), lambda b, t: (l, b, 0, 0)),
                  pl.BlockSpec((None, None, 1, 3 * D_MODEL), lambda b, t: (l + 1, b, 0, 0)),
                  pl.BlockSpec((None, 1, D_MODEL), lambda b, t: (l + 1, 0, 0)),
                  pl.BlockSpec((None, D_MODEL, W_MAIN), lambda b, t: (l + 1, 0, 0), pipeline_mode=once),
                  pl.BlockSpec((None, D_MODEL, DT_PAD), lambda b, t: (l + 1, 0, 0))],
        out_specs=[xspec, big, big, big, big, pl.BlockSpec((1, ROW_TILE, DT_PAD), row)],
        out_shape=[jax.ShapeDtypeStruct((bsz, SEQ, D_MODEL), F32)]
        + [jax.ShapeDtypeStruct((bsz, SEQ, 1024), BF16)] * 4
        + [jax.ShapeDtypeStruct((bsz, SEQ, DT_PAD), F32)],
        scratch_shapes=[pltpu.VMEM((D_MODEL, D_MODEL), BF16)],
        compiler_params=_params(2),
        name="midproj",
    )(*ys, w_out, x, mod4, mod4, norm_w, w_main, w_dt)


KEY_CHUNK = 256
WIDTH = 1024
PAIR_LANES = WIDTH // N_HEADS
VT_ROWS = HEAD_DIM + 16


def _build_toeplitz_t(tab_ref, slopes, patterns, off, unit=1):
    _, rows, tq = tab_ref.shape
    step = min(rows, 512)
    i_io = lax.broadcasted_iota(jnp.int32, (step, tq), 0)
    r_io = lax.broadcasted_iota(jnp.int32, (step, tq), 1)
    base = (i_io - r_io - off) * unit

    def write(i0, n):
        d = base[:n] + i0 * unit
        ad = jnp.abs(d)
        adf = ad.astype(F32)
        logm, valid = 0.0, None
        if patterns is not None:
            mult = jnp.zeros(d.shape, F32)
            for w, r in patterns:
                reach = r * (w // (2 * r))
                mult = mult + jnp.where(ad <= reach, jnp.where((d & (r - 1)) == 0, 1.0, 0.0), 0.0)
            valid = mult > 0.5
            logm = jnp.log(jnp.maximum(mult, 1.0))
        for h, slope in enumerate(slopes):
            val = (logm - slope * adf) * LOG2E
            if valid is not None:
                val = jnp.where(valid, val, NEG)
            tab_ref[h, pl.ds(i0, n), :] = val

    def body(j, _):
        write(pl.multiple_of(j * step, step), step)
        return 0

    lax.fori_loop(0, rows // step, body, 0)
    if rows % step:
        write(rows - rows % step, rows % step)


def _sublane_groups(x):
    return x.reshape(x.shape[0] // SUBLANES, SUBLANES, x.shape[1])


def _attention_t(q_ref, k_ref, v_ref, vt_ref, e_refs, qt_ref, acc_ref, *, n_pairs, key_chunk, win_chunks,
                 key_start, bias, finish):
    tq = WIDTH // n_pairs
    fw = GROUP_W // n_pairs
    n_blocks = SEQ // tq
    ones_row = jnp.where(lax.broadcasted_iota(jnp.int32, (VT_ROWS - HEAD_DIM, key_chunk), 0) == 0, 1.0, 0.0)
    for c in range(SEQ // key_chunk):
        rows = slice(c * key_chunk, (c + 1) * key_chunk)
        vt = v_ref[0, rows, :].astype(F32).T.astype(BF16)
        for h in range(N_HEADS):
            vt_ref[c, h * VT_ROWS:h * VT_ROWS + HEAD_DIM, :] = vt[h * HEAD_DIM:(h + 1) * HEAD_DIM]
            vt_ref[c, h * VT_ROWS + HEAD_DIM:(h + 1) * VT_ROWS, :] = ones_row.astype(BF16)

    feat = lax.broadcasted_iota(jnp.int32, (GROUP_W, tq), 0)
    mx0 = jnp.full((SUBLANES, WIDTH), NEG, F32)

    def load_queries(i):
        qt = q_ref[0, pl.ds(pl.multiple_of(i * tq, tq), tq), :].astype(F32).T
        for p in range(n_pairs):
            keep = (feat >= p * fw) & (feat < (p + 1) * fw)
            qt_ref[:, p * tq:(p + 1) * tq] = jnp.where(keep, qt, 0.0).astype(BF16)

    pph = n_pairs // N_HEADS

    def scores_head(c, i, h, e_ref, mx):
        lanes = slice(h * PAIR_LANES, (h + 1) * PAIR_LANES)
        k0 = pl.multiple_of((key_start(i) + c) * key_chunk, key_chunk)
        s = _dot(k_ref[0, pl.ds(k0, key_chunk), :], qt_ref[:, lanes])
        b = bias(h, i, c)
        e = jnp.concatenate([s[:, p * tq:(p + 1) * tq] + b for p in range(pph)], axis=1)
        e_ref[pl.ds(pl.multiple_of(c * key_chunk, key_chunk), key_chunk), lanes] = e
        return jnp.maximum(mx, jnp.max(_sublane_groups(e), axis=0))

    def scores(c, i, e_ref, mx):
        parts = [scores_head(c, i, h, e_ref, mx[:, h * PAIR_LANES:(h + 1) * PAIR_LANES]) for h in range(N_HEADS)]
        return jnp.concatenate(parts, axis=1)

    def probs_head(c, i, h, e_ref, m):
        lanes = slice(h * PAIR_LANES, (h + 1) * PAIR_LANES)
        r0 = pl.multiple_of(c * key_chunk, key_chunk)
        pb = jnp.exp2(e_ref[pl.ds(r0, key_chunk), lanes] - m[:, lanes]).astype(BF16)
        acc_ref[h] += _dot(vt_ref[key_start(i) + c, h * VT_ROWS:(h + 1) * VT_ROWS, :], pb)

    def probs(c, i, e_ref, m):
        for h in range(N_HEADS):
            probs_head(c, i, h, e_ref, m)

    def finish_block(i, m):
        sums = jnp.concatenate([acc_ref[h, HEAD_DIM:HEAD_DIM + 1, :] for h in range(N_HEADS)], axis=1)
        finish(i, m, sums, acc_ref)

    load_queries(0)
    mx = lax.fori_loop(0, win_chunks, lambda c, mx: scores(c, 0, e_refs[0], mx), mx0, unroll=True)

    def step(i, slot, m):
        load_queries(i + 1)
        acc_ref[...] = jnp.zeros_like(acc_ref)

        def both(c, mx):
            parts = []
            for h in range(N_HEADS):
                parts.append(scores_head(c, i + 1, h, e_refs[1 - slot], mx[:, h * PAIR_LANES:(h + 1) * PAIR_LANES]))
                probs_head(c, i, h, e_refs[slot], m)
            return jnp.concatenate(parts, axis=1)

        mx = lax.fori_loop(0, win_chunks, both, mx0, unroll=True)
        finish_block(i, m)
        return jnp.max(mx, axis=0, keepdims=True)

    def two_steps(j, m):
        return step(2 * j + 1, 1, step(2 * j, 0, m))

    m = lax.fori_loop(0, n_blocks // 2 - 1, two_steps, jnp.max(mx, axis=0, keepdims=True))
    m = step(n_blocks - 2, 0, m)
    acc_ref[...] = jnp.zeros_like(acc_ref)

    def last(c, _):
        probs(c, n_blocks - 1, e_refs[1], m)
        return 0

    lax.fori_loop(0, win_chunks, last, 0, unroll=True)
    finish_block(n_blocks - 1, m)


def _attn_specs():
    def col(j):
        return pl.BlockSpec((1, SEQ, GROUP_W), lambda b: (b, 0, j))
    return [col(0), col(1), col(2), col(3)]


def _attn_scratch(n_pairs, key_chunk, win_chunks, tab_rows):
    tq = WIDTH // n_pairs
    return [pltpu.VMEM((N_HEADS, tab_rows, tq), F32),
            pltpu.VMEM((SEQ // key_chunk, N_HEADS * VT_ROWS, key_chunk), BF16),
            pltpu.VMEM((win_chunks * key_chunk, WIDTH), F32),
            pltpu.VMEM((win_chunks * key_chunk, WIDTH), F32),
            pltpu.VMEM((GROUP_W, WIDTH), BF16),
            pltpu.VMEM((N_HEADS, VT_ROWS, PAIR_LANES), F32)]


def _mask_values(v_ref, vm_ref):
    v = v_ref[0]
    for h in range(N_HEADS):
        vm_ref[h * SEQ:(h + 1) * SEQ, :] = jnp.where(_head_mask(h), v, jnp.zeros_like(v))


A_PAIRS = N_HEADS
A_TQ = WIDTH // A_PAIRS
A_CHUNK = 256
A_WIN = 3
A_FAR = DILATED_PATTERNS[-1][1]
A_NEAR_PATTERNS = DILATED_PATTERNS[:-1]
A_NEAR_REACH = max(r * (w // (2 * r)) for w, r in A_NEAR_PATTERNS)
assert A_NEAR_REACH <= A_CHUNK and A_TQ == A_CHUNK
A_TAB_OFF = 2 * A_CHUNK
A_TAB_ROWS = A_TAB_OFF + A_WIN * A_CHUNK
A_CLASS = SEQ // A_FAR
A_PART = GROUP_W + LANES


def _a_key_start(i):
    return jnp.clip(i - 1, 0, SEQ // A_CHUNK - A_WIN)


def _a_far_partials(q_ref, k_ref, v_ref, tab_ref, x_ref, part_ref):
    tiles = GROUP_W // LANES

    n = A_CLASS
    for a, ref in enumerate((q_ref, k_ref, v_ref)):
        for j in range(tiles):
            x = ref[0, :, j * LANES:(j + 1) * LANES].astype(F32)
            x_ref[a * tiles + j] = jnp.swapaxes(x.reshape(n, A_FAR, LANES), 0, 1)

    feat = lax.broadcasted_iota(jnp.int32, (GROUP_W, n), 0)
    ones_rows = jnp.where(lax.broadcasted_iota(jnp.int32, (VT_ROWS - HEAD_DIM, n), 0) == 0, 1.0, 0.0).astype(BF16)
    pad = jnp.zeros((LANES - 2 * N_HEADS, n), F32)
    def class_scores(rho):
        cols = lambda a: jnp.concatenate([x_ref[a * tiles + j, rho] for j in range(tiles)], axis=1)
        qt = cols(0).T
        qt_all = jnp.concatenate(
            [jnp.where((feat >= h * HEAD_DIM) & (feat < (h + 1) * HEAD_DIM), qt, 0.0) for h in range(N_HEADS)],
            axis=1).astype(BF16)
        return _dot(cols(1).astype(BF16), qt_all)

    s_next = class_scores(0)
    for rho in range(A_FAR):
        s = s_next
        if rho + 1 < A_FAR:
            s_next = class_scores(rho + 1)
        cls = pl.ds(rho, n, stride=A_FAR)
        cols = lambda a: jnp.concatenate([x_ref[a * tiles + j, rho] for j in range(tiles)], axis=1)
        e = jnp.concatenate([s[:, h * n:(h + 1) * n] + tab_ref[h] for h in range(N_HEADS)], axis=1)
        m = jnp.max(e, axis=0, keepdims=True)
        pb = jnp.exp2(e - m).astype(BF16)
        vt = cols(2).T.astype(BF16)
        accs = [_dot(jnp.concatenate([vt[h * HEAD_DIM:(h + 1) * HEAD_DIM], ones_rows], axis=0),
                     pb[:, h * n:(h + 1) * n]) for h in range(N_HEADS)]
        stats = jnp.concatenate([m[:, h * n:(h + 1) * n] for h in range(N_HEADS)]
                                + [a[HEAD_DIM:HEAD_DIM + 1] for a in accs] + [pad], axis=0)
        rec = jnp.concatenate([jnp.concatenate([a[0:HEAD_DIM] for a in accs], axis=0).T, stats.T], axis=1)
        for j in range(A_PART // LANES):
            part_ref[j, cls, :] = rec[:, j * LANES:(j + 1) * LANES]


def _mixer_a_kernel(q_ref, k_ref, v_ref, g_ref, o_ref, tab_ref, vt_ref, e0_ref, e1_ref, qt_ref, acc_ref,
                    far_tab_ref, x_ref, part_ref, *, slopes):
    @pl.when(pl.program_id(0) == 0)
    def _():
        _build_toeplitz_t(tab_ref, slopes, A_NEAR_PATTERNS, A_TAB_OFF)
        _build_toeplitz_t(far_tab_ref, slopes, DILATED_PATTERNS[-1:], 0, unit=A_FAR)

    _a_far_partials(q_ref, k_ref, v_ref, far_tab_ref, x_ref, part_ref)

    def bias(h, i, c):
        t0 = pl.multiple_of((_a_key_start(i) + c - i) * A_CHUNK + A_TAB_OFF, A_CHUNK)
        return tab_ref[h, pl.ds(t0, A_CHUNK), :]

    def finish(i, m, sums, acc_ref):
        q0 = pl.multiple_of(i * A_TQ, A_TQ)
        far = jnp.concatenate([part_ref[j, pl.ds(q0, A_TQ), :] for j in range(GROUP_W // LANES)], axis=1).T
        stats = part_ref[GROUP_W // LANES, pl.ds(q0, A_TQ), :].T
        outs = []
        for h in range(N_HEADS):
            lanes = slice(h * A_TQ, (h + 1) * A_TQ)
            m_far = stats[h:h + 1]
            top = jnp.maximum(m[:, lanes], m_far)
            w_near = jnp.exp2(m[:, lanes] - top)
            w_far = jnp.exp2(m_far - top)
            denom = sums[:, lanes] * w_near + stats[N_HEADS + h:N_HEADS + h + 1] * w_far
            outs.append((acc_ref[h, 0:HEAD_DIM, :] * w_near + far[h * HEAD_DIM:(h + 1) * HEAD_DIM] * w_far) / denom)
        g = g_ref[0, pl.ds(q0, A_TQ), :].astype(F32)
        o_ref[0, pl.ds(q0, A_TQ), :] = (jnp.concatenate(outs, axis=0).T * _silu(g)).astype(BF16)

    _attention_t(q_ref, k_ref, v_ref, vt_ref, (e0_ref, e1_ref), qt_ref, acc_ref, n_pairs=A_PAIRS,
                 key_chunk=A_CHUNK, win_chunks=A_WIN, key_start=_a_key_start, bias=bias, finish=finish)


def _mixer_a(pa, slopes):
    bsz = pa.shape[0]
    return pl.pallas_call(
        functools.partial(_mixer_a_kernel, slopes=slopes),
        grid=(bsz,),
        in_specs=_attn_specs(),
        out_specs=pl.BlockSpec((1, SEQ, GROUP_W), lambda b: (b, 0, 0)),
        out_shape=jax.ShapeDtypeStruct((bsz, SEQ, GROUP_W), BF16),
        scratch_shapes=_attn_scratch(A_PAIRS, A_CHUNK, A_WIN, A_TAB_ROWS) + [
            pltpu.VMEM((N_HEADS, A_CLASS, A_CLASS), F32),
            pltpu.VMEM((3 * GROUP_W // LANES, A_FAR, A_CLASS, LANES), F32),
            pltpu.VMEM((A_PART // LANES, SEQ, LANES), F32)],
        compiler_params=_params(1),
        name="mixer_a_dilated",
    )(pa, pa, pa, pa)


B_PAIRS = 2 * N_HEADS
B_TQ = WIDTH // B_PAIRS


def _mixer_b_kernel(q_ref, k_ref, v_ref, g_ref, lam_ref, nw_ref, o_ref, tab_ref, vt_ref, e0_ref, e1_ref, qt_ref,
                    acc_ref, *, slopes, lam_init):
    tq = B_TQ

    @pl.when(pl.program_id(0) == 0)
    def _():
        _build_toeplitz_t(tab_ref, slopes, None, SEQ - tq)

    def bias(h, i, c):
        return tab_ref[h, pl.ds(pl.multiple_of(SEQ - tq - i * tq + c * KEY_CHUNK, LANES), KEY_CHUNK), :]

    lv = lam_ref[...]
    lam = (jnp.exp(jnp.sum(lv[0:1] * lv[1:2], axis=-1, keepdims=True))
           - jnp.exp(jnp.sum(lv[2:3] * lv[3:4], axis=-1, keepdims=True)) + lam_init)
    nw = nw_ref[...]

    def finish(i, m, sums, acc_ref):
        q0 = pl.multiple_of(i * tq, tq)
        inv = 1.0 / sums
        outs = []
        for h in range(N_HEADS):
            acc = acc_ref[h, 0:HEAD_DIM, :]
            o = (acc[:, 0:tq] * inv[:, 2 * h * tq:(2 * h + 1) * tq]
                 - lam * (acc[:, tq:2 * tq] * inv[:, (2 * h + 1) * tq:(2 * h + 2) * tq]))
            ms = jnp.mean(o * o, axis=0, keepdims=True)
            outs.append(o * lax.rsqrt(ms + EPS))
        on = jnp.concatenate(outs, axis=0).T * nw * (1.0 - lam_init)
        g = g_ref[0, pl.ds(q0, tq), :].astype(F32)
        o_ref[0, pl.ds(q0, tq), :] = (on * _silu(g)).astype(BF16)

    _attention_t(q_ref, k_ref, v_ref, vt_ref, (e0_ref, e1_ref), qt_ref, acc_ref, n_pairs=B_PAIRS,
                 key_chunk=KEY_CHUNK, win_chunks=SEQ // KEY_CHUNK, key_start=lambda i: 0, bias=bias, finish=finish)


def _mixer_b(pb, lam_p, nw256, slopes, lam_init):
    bsz = pb.shape[0]
    return pl.pallas_call(
        functools.partial(_mixer_b_kernel, slopes=slopes, lam_init=lam_init),
        grid=(bsz,),
        in_specs=_attn_specs() + [pl.BlockSpec((4, DIFF_HEAD_DIM), lambda b: (0, 0)),
                                  pl.BlockSpec((1, GROUP_W), lambda b: (0, 0))],
        out_specs=pl.BlockSpec((1, SEQ, GROUP_W), lambda b: (b, 0, 0)),
        out_shape=jax.ShapeDtypeStruct((bsz, SEQ, GROUP_W), BF16),
        scratch_shapes=_attn_scratch(B_PAIRS, KEY_CHUNK, SEQ // KEY_CHUNK, 2 * SEQ - B_TQ),
        compiler_params=_params(1),
        name="mixer_b_diff",
    )(pb, pb, pb, pb, lam_p, nw256)


CONV_TILE = 256
CONV_HALO = SUBLANES
N_CHUNK = SEQ // SSM_CHUNK


def _split3_dot(x, w3):
    hi = x.astype(BF16)
    r1 = x - hi.astype(F32)
    mid = r1.astype(BF16)
    lo = (r1 - mid.astype(F32)).astype(BF16)
    return _dot(jnp.concatenate([hi, mid, lo], axis=1), w3)


def _cumsum_rows(a):
    row = lax.broadcasted_iota(jnp.int32, a.shape, 0)
    s = 1
    while s < a.shape[0]:
        a = a + jnp.where(row >= s, pltpu.roll(a, s, 0), 0.0)
        s *= 2
    return a


def _mixer_c_kernel(p_ref, dt_ref, cw_ref, cb_ref, alog_ref, alogx_ref, dtb_ref, dskip_ref, nw_ref, exp_ref,
                    o_ref, xpad, xc, y_s, sb_s, cs_s, db_s):
    L = SSM_CHUNK
    zero_rows = jnp.zeros((CONV_HALO, SSM_XBC), F32)
    xpad[0:CONV_HALO, :] = zero_rows
    xpad[CONV_HALO + SEQ:CONV_HALO + SEQ + CONV_HALO, :] = zero_rows

    def fill(i, _):
        r0 = pl.multiple_of(i * CONV_TILE, CONV_TILE)
        xpad[pl.ds(CONV_HALO + r0, CONV_TILE), :] = p_ref[0, pl.ds(r0, CONV_TILE), GROUP_W:].astype(F32)
        return 0

    lax.fori_loop(0, SEQ // CONV_TILE, fill, 0)

    def conv(i, _):
        r0 = pl.multiple_of(i * CONV_TILE, CONV_TILE)
        rows = CONV_TILE + 2 * CONV_HALO
        win = xpad[pl.ds(r0, rows), :]
        acc = jnp.zeros((CONV_TILE, SSM_XBC), F32) + cb_ref[...]
        for j in range(SSM_CONV):
            back = (SSM_CONV // 2 - j) % rows
            tap = win if back == 0 else pltpu.roll(win, back, 0)
            acc = acc + cw_ref[j:j + 1, :] * tap[CONV_HALO:CONV_HALO + CONV_TILE, :]
        xc[pl.ds(r0, CONV_TILE), :] = _silu(acc)
        return 0

    lax.fori_loop(0, SEQ // CONV_TILE, conv, 0)

    a_neg_x = -jnp.exp(alogx_ref[...])
    a_neg = -jnp.exp(alog_ref[...])
    expand = exp_ref[...]
    li = lax.broadcasted_iota(jnp.int32, (L, L), 0)
    si = lax.broadcasted_iota(jnp.int32, (L, L), 1)
    lower = si <= li
    upper = si >= li
    hmasks = [_head_mask(h) for h in range(N_HEADS)]

    def chunk_terms(t0):
        dtr = dt_ref[0, pl.ds(t0, L), :] + dtb_ref[...]
        dt = jnp.maximum(dtr, 0.0) + jnp.log(1.0 + jnp.exp(-jnp.abs(dtr)))
        a = dt * a_neg
        ainc = _cumsum_rows(a)
        aexc = ainc - a
        return dt, ainc, aexc

    def fwd(c, hf):
        t0 = pl.multiple_of(c * L, L)
        dt, ainc, aexc = chunk_terms(t0)
        both_x = _split3_dot(jnp.concatenate([dt, ainc], axis=0), expand)
        dt_x = both_x[0:L]
        ainc_x = both_x[L:2 * L]
        aexc_x = ainc_x - dt_x * a_neg_x
        ainc_t = ainc.T
        aexc_t = aexc.T
        xs = xc[pl.ds(t0, L), 0:GROUP_W]
        bm = xc[pl.ds(t0, L), GROUP_W:2 * GROUP_W]
        cm = xc[pl.ds(t0, L), 2 * GROUP_W:3 * GROUP_W]
        xf = xs * dt_x[:, 0:GROUP_W]
        xb = xs * dt_x[:, GROUP_W:2 * GROUP_W]
        xcat = jnp.concatenate([xf, xb], axis=0).astype(BF16)
        tot_f = ainc_x[L - 1:L, 0:GROUP_W]
        tot_b = ainc_x[L - 1:L, GROUP_W:2 * GROUP_W]
        y = xs * dskip_ref[...]
        cbs = []
        for g in range(SSM_GROUPS):
            gs = slice(g * SSM_STATE, (g + 1) * SSM_STATE)
            cbs.append(_dot_nt(cm[:, gs].astype(BF16), bm[:, gs].astype(BF16)))
        for h in range(N_HEADS):
            cb = cbs[h // (N_HEADS // SSM_GROUPS)]
            col_f = ainc[:, h:h + 1]
            row_f = ainc_t[h:h + 1, :]
            col_b = aexc[:, N_HEADS + h:N_HEADS + h + 1]
            row_b = aexc_t[N_HEADS + h:N_HEADS + h + 1, :]
            lf = jnp.exp(jnp.where(lower, col_f - row_f, NEG))
            ub = jnp.exp(jnp.where(upper, row_b - col_b, NEG))
            mcat = jnp.concatenate([cb * lf, cb * ub], axis=1).astype(BF16)
            y = y + jnp.where(hmasks[h], _dot(mcat, xcat), 0.0)
        wf = (jnp.exp(tot_f - ainc_x[:, 0:GROUP_W]) * xf).astype(BF16)
        wb = (jnp.exp(aexc_x[:, GROUP_W:2 * GROUP_W]) * xb).astype(BF16)
        ef = jnp.exp(ainc_x[:, 0:GROUP_W])
        cs_s[pl.ds(t0, L), :] = jnp.exp(tot_b - aexc_x[:, GROUP_W:2 * GROUP_W])
        db_s[pl.ds(pl.multiple_of(c * SUBLANES, SUBLANES), SUBLANES), :] = jnp.broadcast_to(
            jnp.exp(tot_b), (SUBLANES, GROUP_W))
        dec_f = jnp.exp(tot_f)
        hf_new = []
        yoff = []
        for g in range(SSM_GROUPS):
            gs = slice(g * SSM_STATE, (g + 1) * SSM_STATE)
            bt = bm[:, gs].T.astype(BF16)
            yoff.append(_dot(cm[:, gs].astype(BF16), hf[g].astype(BF16)))
            hf_new.append(dec_f[:, gs] * hf[g] + _dot(bt, wf[:, gs]))
            sb_s[pl.ds(t0, L), gs] = _dot(bt, wb[:, gs])
        y = y + jnp.concatenate(yoff, axis=1) * ef
        y_s[pl.ds(t0, L), :] = y
        return tuple(hf_new)

    h0 = tuple(jnp.zeros((SSM_STATE, SSM_STATE), F32) for _ in range(SSM_GROUPS))
    lax.fori_loop(0, N_CHUNK, fwd, h0, unroll=2)

    def bwd(i, hb):
        c = N_CHUNK - 1 - i
        t0 = pl.multiple_of(c * L, L)
        cm = xc[pl.ds(t0, L), 2 * GROUP_W:3 * GROUP_W]
        dec_b = db_s[pl.ds(pl.multiple_of(c * SUBLANES, SUBLANES), 1), :]
        yoff = []
        hb_new = []
        for g in range(SSM_GROUPS):
            gs = slice(g * SSM_STATE, (g + 1) * SSM_STATE)
            yoff.append(_dot(cm[:, gs].astype(BF16), hb[g].astype(BF16)))
            hb_new.append(dec_b[:, gs] * hb[g] + sb_s[pl.ds(t0, L), gs])
        y_s[pl.ds(t0, L), :] = y_s[pl.ds(t0, L), :] + jnp.concatenate(yoff, axis=1) * cs_s[pl.ds(t0, L), :]
        return tuple(hb_new)

    lax.fori_loop(0, N_CHUNK, bwd, h0, unroll=4)

    def fin(i, _):
        r0 = pl.multiple_of(i * CONV_TILE, CONV_TILE)
        z = p_ref[0, pl.ds(r0, CONV_TILE), 0:GROUP_W].astype(F32)
        y = y_s[pl.ds(r0, CONV_TILE), :] * _silu(z)
        parts = []
        for g in range(SSM_GROUPS):
            yg = y[:, g * SSM_STATE:(g + 1) * SSM_STATE]
            parts.append(yg * lax.rsqrt(jnp.mean(yg * yg, axis=-1, keepdims=True) + EPS))
        o_ref[0, pl.ds(r0, CONV_TILE), :] = (jnp.concatenate(parts, axis=1) * nw_ref[...]).astype(BF16)
        return 0

    lax.fori_loop(0, SEQ // CONV_TILE, fin, 0)


def _head_expand_matrix():
    j = lax.broadcasted_iota(jnp.int32, (LANES, 2 * GROUP_W), 0)
    c = lax.broadcasted_iota(jnp.int32, (LANES, 2 * GROUP_W), 1)
    return jnp.tile((j == c // HEAD_DIM).astype(BF16), (3, 1))


def _mixer_c(pc, pdt, conv_w, conv_b, a_log, dt_bias, d_skip, norm_w):
    bsz = pc.shape[0]
    pad8 = lambda v: jnp.pad(v.reshape(1, 2 * N_HEADS), ((0, 0), (0, LANES - 2 * N_HEADS)))
    small = lambda shape: pl.BlockSpec(shape, lambda b: (0,) * len(shape))
    return pl.pallas_call(
        _mixer_c_kernel,
        grid=(bsz,),
        in_specs=[pl.BlockSpec((1, SEQ, 1024), lambda b: (b, 0, 0)),
                  pl.BlockSpec((1, SEQ, DT_PAD), lambda b: (b, 0, 0)),
                  small((SSM_CONV, SSM_XBC)), small((1, SSM_XBC)),
                  small((1, LANES)), small((1, 2 * GROUP_W)), small((1, LANES)),
                  small((1, GROUP_W)), small((1, GROUP_W)),
                  small((3 * LANES, 2 * GROUP_W))],
        out_specs=pl.BlockSpec((1, SEQ, GROUP_W), lambda b: (b, 0, 0)),
        out_shape=jax.ShapeDtypeStruct((bsz, SEQ, GROUP_W), BF16),
        scratch_shapes=[pltpu.VMEM((SEQ + 2 * CONV_HALO, SSM_XBC), F32),
                        pltpu.VMEM((SEQ, SSM_XBC), F32),
                        pltpu.VMEM((SEQ, GROUP_W), F32),
                        pltpu.VMEM((SEQ, GROUP_W), F32),
                        pltpu.VMEM((SEQ, GROUP_W), F32),
                        pltpu.VMEM((N_CHUNK * SUBLANES, GROUP_W), F32)],
        compiler_params=_params(1),
        name="mixer_c_ssd",
    )(pc, pdt, conv_w, conv_b.reshape(1, SSM_XBC), pad8(a_log),
      jnp.repeat(a_log.reshape(-1), HEAD_DIM).reshape(1, 2 * GROUP_W), pad8(dt_bias),
      jnp.repeat(d_skip, HEAD_DIM).reshape(1, GROUP_W), norm_w.reshape(1, GROUP_W),
      _head_expand_matrix())


NA_ROWS = SEQ // GRID_W
NA_DR = 2 * NA_WIN_H - 1
NA_DC = 2 * NA_WIN_W - 1
NA_PAIRS = NA_DR - 1


def _na_bias_kernel(rpb_ref, o_ref):
    lh = pl.program_id(0)
    cq = lax.broadcasted_iota(jnp.int32, (GRID_W, LANES), 0)
    lane = lax.broadcasted_iota(jnp.int32, (GRID_W, LANES), 1)
    second = lane >= GRID_W
    ck = jnp.where(second, lane - GRID_W, lane)
    cs = jnp.clip(cq - NA_WIN_W // 2, 0, GRID_W - NA_WIN_W)
    inside = (ck >= cs) & (ck < cs + NA_WIN_W)
    dc = ck - cq + NA_WIN_W - 1
    base = lh * (NA_DR * NA_DC)
    rows = []
    for dr in range(NA_DR):
        acc = jnp.zeros((GRID_W, LANES), F32)
        for j in range(NA_DC):
            acc = jnp.where(dc == j, rpb_ref[base + dr * NA_DC + j], acc)
        rows.append(acc)
    for p in range(NA_PAIRS):
        o_ref[0, p] = jnp.where(inside, jnp.where(second, rows[p + 1], rows[p]), NEG)


def _na_bias(na_rpb):
    n = DEPTH * N_HEADS
    return pl.pallas_call(
        _na_bias_kernel,
        grid=(n,),
        in_specs=[pl.BlockSpec(memory_space=pltpu.SMEM)],
        out_specs=pl.BlockSpec((1, NA_PAIRS, GRID_W, LANES), lambda i: (i, 0, 0, 0)),
        out_shape=jax.ShapeDtypeStruct((n, NA_PAIRS, GRID_W, LANES), F32),
        compiler_params=_params(1),
        name="na_bias_table",
    )(na_rpb.reshape(-1))


NA_UNROLL = 8


def _mixer_d_kernel(q_ref, k_ref, v_ref, g_ref, bias_ref, o_ref, vm_ref):
    nkeys = NA_WIN_H * GRID_W
    _mask_values(v_ref, vm_ref)

    def score_row(r):
        rs = jnp.clip(r - NA_WIN_H // 2, 0, NA_ROWS - NA_WIN_H)
        k0 = pl.multiple_of(rs * GRID_W, GRID_W)
        q = q_ref[0, pl.ds(pl.multiple_of(r * GRID_W, GRID_W), GRID_W), :]
        qs = jnp.concatenate([jnp.where(_head_mask(h), q, jnp.zeros_like(q)) for h in range(N_HEADS)], axis=0)
        return _dot_nt(qs, k_ref[0, pl.ds(k0, nkeys), :])

    def prob_row(r, s_all):
        delta = r - jnp.clip(r - NA_WIN_H // 2, 0, NA_ROWS - NA_WIN_H)
        ps, invs = [], []
        for h in range(N_HEADS):
            bias = jnp.concatenate(
                [bias_ref[h, 2 * i - delta + NA_WIN_H - 1] for i in range(NA_WIN_H // 2)], axis=-1)
            e = s_all[h * GRID_W:(h + 1) * GRID_W] + bias
            m = jnp.max(e, axis=-1, keepdims=True)
            p = jnp.exp(e - m)
            invs.append(1.0 / jnp.sum(p, axis=-1, keepdims=True))
            ps.append(p.astype(BF16))
        inv = jnp.broadcast_to(invs[N_HEADS - 1], (GRID_W, GROUP_W))
        for h in range(N_HEADS - 2, -1, -1):
            inv = jnp.where(_head_mask(h), invs[h], inv)
        return jnp.concatenate(ps, axis=-1), inv

    def value_row(r, pcat, inv):
        rs = jnp.clip(r - NA_WIN_H // 2, 0, NA_ROWS - NA_WIN_H)
        k0 = pl.multiple_of(rs * GRID_W, GRID_W)
        q0 = pl.multiple_of(r * GRID_W, GRID_W)
        vcat = jnp.concatenate([vm_ref[pl.ds(pl.multiple_of(h * SEQ + k0, GRID_W), nkeys), :]
                                for h in range(N_HEADS)], axis=0)
        acc = _dot(pcat, vcat)
        g = g_ref[0, pl.ds(q0, GRID_W), :].astype(F32)
        o_ref[0, pl.ds(q0, GRID_W), :] = (acc * inv * _silu(g)).astype(BF16)

    def rstep(i, _):
        rows = [i * NA_UNROLL + u for u in range(NA_UNROLL)]
        s_next = score_row(rows[0])
        for u, r in enumerate(rows):
            s_cur = s_next
            if u + 1 < NA_UNROLL:
                s_next = score_row(rows[u + 1])
            value_row(r, *prob_row(r, s_cur))
        return 0

    lax.fori_loop(0, NA_ROWS // NA_UNROLL, rstep, 0)


def _mixer_d(pd, bias, l):
    bsz = pd.shape[0]
    return pl.pallas_call(
        _mixer_d_kernel,
        grid=(bsz,),
        in_specs=_attn_specs() + [pl.BlockSpec((N_HEADS, NA_PAIRS, GRID_W, LANES), lambda b: (l, 0, 0, 0))],
        out_specs=pl.BlockSpec((1, SEQ, GROUP_W), lambda b: (b, 0, 0)),
        out_shape=jax.ShapeDtypeStruct((bsz, SEQ, GROUP_W), BF16),
        scratch_shapes=[pltpu.VMEM((N_HEADS * SEQ, GROUP_W), BF16)],
        compiler_params=_params(1),
        name="mixer_d_neighbourhood",
    )(pd, pd, pd, pd, bias)


def kernel(x, c, norm_w, ada_w, ada_b, w_in, diff_lambda, diff_norm_w, conv_w, conv_b, ssm_a_log,
           ssm_dt_bias, ssm_d, ssm_norm_w, na_rpb, w_out, final_norm_w):
    bsz = x.shape[0]
    n = 2 * N_HEADS
    slopes = [2.0 ** (-8.0 * i / n) for i in range(1, n + 1)]
    slopes_a, slopes_b = tuple(slopes[0::2]), tuple(slopes[1::2])
    mod4 = _adaln(c, ada_w, ada_b).reshape(DEPTH, bsz, 1, 3 * D_MODEL)
    na_bias = _na_bias(na_rpb)
    w_main, w_dt = _prep_w_in(w_in)
    norm_w3 = norm_w.reshape(DEPTH, 1, D_MODEL)
    final_w = final_norm_w.reshape(1, D_MODEL)
    pa, pb, pc, pd, pdt = _inproj(x, mod4, norm_w3, w_main, w_dt, 0)
    for l in range(DEPTH):
        lam_init = 0.8 - 0.6 * math.exp(-0.3 * l)
        ya = _mixer_a(pa, slopes_a)
        yb = _mixer_b(pb, diff_lambda[l], jnp.tile(diff_norm_w[l], N_HEADS).reshape(1, GROUP_W),
                      slopes_b, lam_init)
        yc = _mixer_c(pc, pdt, conv_w[l], conv_b[l], ssm_a_log[l], ssm_dt_bias[l], ssm_d[l], ssm_norm_w[l])
        yd = _mixer_d(pd, na_bias, l)
        if l + 1 < DEPTH:
            x, pa, pb, pc, pd, pdt = _midproj((ya, yb, yc, yd), w_out, x, mod4, norm_w3, w_main, w_dt, l)
        else:
            x = _outproj((ya, yb, yc, yd), w_out, x, mod4, final_w, l)
    return x
```

```python
import functools
import math

import jax
import jax.numpy as jnp
from jax import lax
from jax.experimental import pallas as pl
from jax.experimental.pallas import tpu as pltpu

D_MODEL = 1024
SEQ = 2048
DEPTH = 2
HEAD_DIM = 64
GROUP_W = 256
N_HEADS = 4
EPS = 1e-6
DILATED_PATTERNS = ((128, 1), (512, 4), (2048, 16))
DIFF_HEAD_DIM = 32
SSM_GROUPS = 2
SSM_STATE = 128
SSM_CONV = 5
SSM_CHUNK = 128
SSM_XBC = 768
GRID_W = 64
NA_WIN_H = 8
NA_WIN_W = 16
D_IN = 13 * GROUP_W + SSM_XBC + 2 * N_HEADS

LANES = 128
SUBLANES = 8
VMEM_LIMIT = 56 * 1024 * 1024

NEG = -1e30
LOG2E = math.log2(math.e)
F32 = jnp.float32
BF16 = jnp.bfloat16
HIGHEST = lax.Precision.HIGHEST

ROW_TILE = 512
DT_PAD = LANES


def _silu(x):
    return x / (1.0 + jnp.exp(-x))


def _dot_nt(a, b):
    return lax.dot_general(a, b, (((1,), (1,)), ((), ())), preferred_element_type=F32)


def _dot(a, b):
    return jnp.dot(a, b, preferred_element_type=F32)


def _params(n_grid):
    return pltpu.CompilerParams(dimension_semantics=("arbitrary",) * n_grid,
                                vmem_limit_bytes=VMEM_LIMIT)


def _head_mask(h, width=HEAD_DIM, total=GROUP_W):
    lane = lax.broadcasted_iota(jnp.int32, (1, total), 1)
    return (lane >= h * width) & (lane < (h + 1) * width)


def _mod_kernel(c_ref, w_ref, b_ref, o_ref):
    c = c_ref[...]
    o_ref[0] = jnp.dot(_silu(c), w_ref[0], precision=HIGHEST,
                       preferred_element_type=F32) + b_ref[0]


def _adaln(c, ada_w, ada_b):
    bsz = c.shape[0]
    tn = 768
    return pl.pallas_call(
        _mod_kernel,
        grid=(DEPTH, 3 * D_MODEL // tn),
        in_specs=[pl.BlockSpec((bsz, D_MODEL), lambda l, j: (0, 0)),
                  pl.BlockSpec((1, D_MODEL, tn), lambda l, j: (l, 0, j)),
                  pl.BlockSpec((1, 1, tn), lambda l, j: (l, 0, j))],
        out_specs=pl.BlockSpec((1, bsz, tn), lambda l, j: (l, 0, j)),
        out_shape=jax.ShapeDtypeStruct((DEPTH, bsz, 3 * D_MODEL), F32),
        compiler_params=_params(2),
        name="adaln_mod",
    )(c, ada_w, ada_b.reshape(DEPTH, 1, 3 * D_MODEL))


W_MAIN = 16 * GROUP_W
DT_COL0 = 9 * GROUP_W + SSM_XBC


def _prep_w_in(w_in):
    scale = [1.0] * W_MAIN
    for c0, s in ((0, HEAD_DIM ** -0.5 * LOG2E), (4 * GROUP_W, DIFF_HEAD_DIM ** -0.5 * LOG2E),
                  (12 * GROUP_W, HEAD_DIM ** -0.5)):
        scale[c0:c0 + GROUP_W] = [s] * GROUP_W
    sc = jnp.asarray(scale, F32)
    d0 = DT_COL0 + 2 * N_HEADS
    w_main = jnp.concatenate([(w_in[:, :, :DT_COL0] * sc[:DT_COL0]).astype(BF16),
                              (w_in[:, :, d0:] * sc[DT_COL0:]).astype(BF16)], axis=2)
    w_dt = jnp.pad(w_in[:, :, DT_COL0:d0].astype(BF16), ((0, 0), (0, 0), (0, DT_PAD - 2 * N_HEADS)))
    return w_main, w_dt


def _inproj_kernel(x_ref, mod_ref, nw_ref, w_ref, wdt_ref, pa_ref, pb_ref, pc_ref, pd_ref, pdt_ref):
    x = x_ref[0]
    shift = mod_ref[:, 0:D_MODEL]
    scale = mod_ref[:, D_MODEL:2 * D_MODEL]
    y = x * lax.rsqrt(jnp.mean(x * x, axis=-1, keepdims=True) + EPS) * nw_ref[...]
    h = (y * (1.0 + scale) + shift).astype(BF16)
    for i, ref in enumerate((pa_ref, pb_ref, pc_ref, pd_ref)):
        ref[0] = _dot(h, w_ref[:, i * 1024:(i + 1) * 1024]).astype(BF16)
    pdt_ref[0] = _dot(h, wdt_ref[...])


def _inproj(x, mod4, norm_w, w_main, w_dt, l):
    bsz = x.shape[0]
    row = lambda b, t: (b, t, 0)
    big = pl.BlockSpec((1, ROW_TILE, 1024), row)
    return pl.pallas_call(
        _inproj_kernel,
        grid=(bsz, SEQ // ROW_TILE),
        in_specs=[pl.BlockSpec((1, ROW_TILE, D_MODEL), row),
                  pl.BlockSpec((None, None, 1, 3 * D_MODEL), lambda b, t: (l, b, 0, 0)),
                  pl.BlockSpec((None, 1, D_MODEL), lambda b, t: (l, 0, 0)),
                  pl.BlockSpec((None, D_MODEL, W_MAIN), lambda b, t: (l, 0, 0)),
                  pl.BlockSpec((None, D_MODEL, DT_PAD), lambda b, t: (l, 0, 0))],
        out_specs=[big, big, big, big, pl.BlockSpec((1, ROW_TILE, DT_PAD), row)],
        out_shape=[jax.ShapeDtypeStruct((bsz, SEQ, 1024), BF16)] * 4
        + [jax.ShapeDtypeStruct((bsz, SEQ, DT_PAD), F32)],
        compiler_params=_params(2),
        name="inproj",
    )(x, mod4, norm_w, w_main, w_dt)


def _outproj_kernel(ya_ref, yb_ref, yc_ref, yd_ref, w_ref, x_ref, mod_ref, fw_ref, o_ref, wb_ref, *, final):
    @pl.when((pl.program_id(0) == 0) & (pl.program_id(1) == 0))
    def _():
        wb_ref[...] = w_ref[...].astype(BF16)

    acc = None
    for i, ref in enumerate((ya_ref, yb_ref, yc_ref, yd_ref)):
        part = _dot(ref[0], wb_ref[i * GROUP_W:(i + 1) * GROUP_W, :])
        acc = part if acc is None else acc + part
    gate = mod_ref[:, 2 * D_MODEL:3 * D_MODEL]
    xn = x_ref[0] + gate * acc
    if final:
        xn = xn * lax.rsqrt(jnp.mean(xn * xn, axis=-1, keepdims=True) + EPS) * fw_ref[...]
    o_ref[0] = xn


def _outproj(ys, w_out, x, mod4, final_w, l):
    bsz = x.shape[0]
    final = l == DEPTH - 1
    row = lambda b, t: (b, t, 0)
    yspec = pl.BlockSpec((1, ROW_TILE, GROUP_W), row)
    return pl.pallas_call(
        functools.partial(_outproj_kernel, final=final),
        grid=(bsz, SEQ // ROW_TILE),
        in_specs=[yspec, yspec, yspec, yspec,
                  pl.BlockSpec((None, D_MODEL, D_MODEL), lambda b, t: (l, 0, 0)),
                  pl.BlockSpec((1, ROW_TILE, D_MODEL), row),
                  pl.BlockSpec((None, None, 1, 3 * D_MODEL), lambda b, t: (l, b, 0, 0)),
                  pl.BlockSpec((1, D_MODEL), lambda b, t: (0, 0))],
        out_specs=pl.BlockSpec((1, ROW_TILE, D_MODEL), row),
        out_shape=jax.ShapeDtypeStruct((bsz, SEQ, D_MODEL), F32),
        scratch_shapes=[pltpu.VMEM((D_MODEL, D_MODEL), BF16)],
        compiler_params=_params(2),
        name="outproj_final" if final else "outproj",
    )(*ys, w_out, x, mod4, final_w)


def _midproj_kernel(ya_ref, yb_ref, yc_ref, yd_ref, wo_ref, x_ref, mod_ref, modn_ref, nw_ref, w_ref, wdt_ref,
                    xo_ref, pa_ref, pb_ref, pc_ref, pd_ref, pdt_ref, wb_ref):
    @pl.when((pl.program_id(0) == 0) & (pl.program_id(1) == 0))
    def _():
        wb_ref[...] = wo_ref[...].astype(BF16)

    acc = None
    for i, ref in enumerate((ya_ref, yb_ref, yc_ref, yd_ref)):
        part = _dot(ref[0], wb_ref[i * GROUP_W:(i + 1) * GROUP_W, :])
        acc = part if acc is None else acc + part
    x = x_ref[0] + mod_ref[:, 2 * D_MODEL:3 * D_MODEL] * acc
    xo_ref[0] = x
    shift = modn_ref[:, 0:D_MODEL]
    scale = modn_ref[:, D_MODEL:2 * D_MODEL]
    y = x * lax.rsqrt(jnp.mean(x * x, axis=-1, keepdims=True) + EPS) * nw_ref[...]
    h = (y * (1.0 + scale) + shift).astype(BF16)
    for i, ref in enumerate((pa_ref, pb_ref, pc_ref, pd_ref)):
        ref[0] = _dot(h, w_ref[:, i * 1024:(i + 1) * 1024]).astype(BF16)
    pdt_ref[0] = _dot(h, wdt_ref[...])


def _midproj(ys, w_out, x, mod4, norm_w, w_main, w_dt, l):
    bsz = x.shape[0]
    row = lambda b, t: (b, t, 0)
    yspec = pl.BlockSpec((1, ROW_TILE, GROUP_W), row)
    big = pl.BlockSpec((1, ROW_TILE, 1024), row)
    xspec = pl.BlockSpec((1, ROW_TILE, D_MODEL), row)
    once = pl.Buffered(1)
    return pl.pallas_call(
        _midproj_kernel,
        grid=(bsz, SEQ // ROW_TILE),
        in_specs=[yspec, yspec, yspec, yspec,
                  pl.BlockSpec((None, D_MODEL, D_MODEL), lambda b, t: (l, 0, 0), pipeline_mode=once),
                  xspec,
                  pl.BlockSpec((None, None, 1, 3 * D_MODEL), lambda b, t: (l, b, 0, 0)),
                  pl.BlockSpec((None, None, 1, 3 * D_MODEL), lambda b, t: (l + 1, b, 0, 0)),
                  pl.BlockSpec((None, 1, D_MODEL), lambda b, t: (l + 1, 0, 0)),
                  pl.BlockSpec((None, D_MODEL, W_MAIN), lambda b, t: (l + 1, 0, 0), pipeline_mode=once),
                  pl.BlockSpec((None, D_MODEL, DT_PAD), lambda b, t: (l + 1, 0, 0))],
        out_specs=[xspec, big, big, big, big, pl.BlockSpec((1, ROW_TILE, DT_PAD), row)],
        out_shape=[jax.ShapeDtypeStruct((bsz, SEQ, D_MODEL), F32)]
        + [jax.ShapeDtypeStruct((bsz, SEQ, 1024), BF16)] * 4
        + [jax.ShapeDtypeStruct((bsz, SEQ, DT_PAD), F32)],
        scratch_shapes=[pltpu.VMEM((D_MODEL, D_MODEL), BF16)],
        compiler_params=_params(2),
        name="midproj",
    )(*ys, w_out, x, mod4, mod4, norm_w, w_main, w_dt)


KEY_CHUNK = 256
WIDTH = 1024
PAIR_LANES = WIDTH // N_HEADS
VT_ROWS = HEAD_DIM + 16
ATTN_BATCH = 2


def _build_toeplitz_t(tab_ref, slopes, patterns, off, unit=1):
    _, rows, tq = tab_ref.shape
    step = min(rows, 512)
    i_io = lax.broadcasted_iota(jnp.int32, (step, tq), 0)
    r_io = lax.broadcasted_iota(jnp.int32, (step, tq), 1)
    base = (i_io - r_io - off) * unit

    def write(i0, n):
        d = base[:n] + i0 * unit
        ad = jnp.abs(d)
        adf = ad.astype(F32)
        logm, valid = 0.0, None
        if patterns is not None:
            mult = jnp.zeros(d.shape, F32)
            for w, r in patterns:
                reach = r * (w // (2 * r))
                mult = mult + jnp.where(ad <= reach, jnp.where((d & (r - 1)) == 0, 1.0, 0.0), 0.0)
            valid = mult > 0.5
            logm = jnp.log(jnp.maximum(mult, 1.0))
        for h, slope in enumerate(slopes):
            val = (logm - slope * adf) * LOG2E
            if valid is not None:
                val = jnp.where(valid, val, NEG)
            tab_ref[h, pl.ds(i0, n), :] = val

    def body(j, _):
        write(pl.multiple_of(j * step, step), step)
        return 0

    lax.fori_loop(0, rows // step, body, 0)
    if rows % step:
        write(rows - rows % step, rows % step)


def _sublane_groups(x):
    return x.reshape(x.shape[0] // SUBLANES, SUBLANES, x.shape[1])


def _attention_t(q_ref, k_ref, v_ref, vt_ref, e_refs, qt_ref, acc_ref, *, n_pairs, key_chunk, win_chunks,
                 key_start, bias, finish):
    n_batch = ATTN_BATCH
    tq = WIDTH // n_pairs
    fw = GROUP_W // n_pairs
    per_seq = SEQ // tq
    n_blocks = n_batch * per_seq
    key_chunks = SEQ // key_chunk
    ones_row = jnp.where(lax.broadcasted_iota(jnp.int32, (VT_ROWS - HEAD_DIM, key_chunk), 0) == 0, 1.0, 0.0)
    for bi in range(n_batch):
        for c in range(key_chunks):
            rows = slice(c * key_chunk, (c + 1) * key_chunk)
            vt = v_ref[bi, rows, :].astype(F32).T.astype(BF16)
            for h in range(N_HEADS):
                r0 = h * VT_ROWS
                vt_ref[bi * key_chunks + c, r0:r0 + HEAD_DIM, :] = vt[h * HEAD_DIM:(h + 1) * HEAD_DIM]
                vt_ref[bi * key_chunks + c, r0 + HEAD_DIM:r0 + VT_ROWS, :] = ones_row.astype(BF16)

    feat = lax.broadcasted_iota(jnp.int32, (GROUP_W, tq), 0)
    mx0 = jnp.full((SUBLANES, WIDTH), NEG, F32)

    def split(i):
        return i // per_seq, i % per_seq

    def load_queries(i):
        bi, li = split(i)
        qt = q_ref[bi, pl.ds(pl.multiple_of(li * tq, tq), tq), :].astype(F32).T
        for p in range(n_pairs):
            keep = (feat >= p * fw) & (feat < (p + 1) * fw)
            qt_ref[:, p * tq:(p + 1) * tq] = jnp.where(keep, qt, 0.0).astype(BF16)

    pph = n_pairs // N_HEADS

    def scores_head(c, i, h, e_ref, mx):
        lanes = slice(h * PAIR_LANES, (h + 1) * PAIR_LANES)
        bi, li = split(i)
        k0 = pl.multiple_of((key_start(li) + c) * key_chunk, key_chunk)
        s = _dot(k_ref[bi, pl.ds(k0, key_chunk), :], qt_ref[:, lanes])
        b = bias(h, li, c)
        e = jnp.concatenate([s[:, p * tq:(p + 1) * tq] + b for p in range(pph)], axis=1)
        e_ref[pl.ds(pl.multiple_of(c * key_chunk, key_chunk), key_chunk), lanes] = e
        return jnp.maximum(mx, jnp.max(_sublane_groups(e), axis=0))

    def scores(c, i, e_ref, mx):
        parts = [scores_head(c, i, h, e_ref, mx[:, h * PAIR_LANES:(h + 1) * PAIR_LANES]) for h in range(N_HEADS)]
        return jnp.concatenate(parts, axis=1)

    def probs_head(c, i, h, e_ref, m):
        lanes = slice(h * PAIR_LANES, (h + 1) * PAIR_LANES)
        r0 = pl.multiple_of(c * key_chunk, key_chunk)
        pb = jnp.exp2(e_ref[pl.ds(r0, key_chunk), lanes] - m[:, lanes]).astype(BF16)
        bi, li = split(i)
        acc_ref[h] += _dot(vt_ref[bi * key_chunks + key_start(li) + c, h * VT_ROWS:(h + 1) * VT_ROWS, :], pb)

    def probs(c, i, e_ref, m):
        for h in range(N_HEADS):
            probs_head(c, i, h, e_ref, m)

    def finish_block(i, m):
        sums = jnp.concatenate([acc_ref[h, HEAD_DIM:HEAD_DIM + 1, :] for h in range(N_HEADS)], axis=1)
        finish(*split(i), m, sums, acc_ref)

    load_queries(0)
    mx = lax.fori_loop(0, win_chunks, lambda c, mx: scores(c, 0, e_refs[0], mx), mx0, unroll=True)

    def step(i, slot, m):
        load_queries(i + 1)
        acc_ref[...] = jnp.zeros_like(acc_ref)

        def both(c, mx):
            parts = []
            for h in range(N_HEADS):
                parts.append(scores_head(c, i + 1, h, e_refs[1 - slot], mx[:, h * PAIR_LANES:(h + 1) * PAIR_LANES]))
                probs_head(c, i, h, e_refs[slot], m)
            return jnp.concatenate(parts, axis=1)

        mx = lax.fori_loop(0, win_chunks, both, mx0, unroll=True)
        finish_block(i, m)
        return jnp.max(mx, axis=0, keepdims=True)

    def two_steps(j, m):
        return step(2 * j + 1, 1, step(2 * j, 0, m))

    m = lax.fori_loop(0, n_blocks // 2 - 1, two_steps, jnp.max(mx, axis=0, keepdims=True))
    m = step(n_blocks - 2, 0, m)
    acc_ref[...] = jnp.zeros_like(acc_ref)

    def last(c, _):
        probs(c, n_blocks - 1, e_refs[1], m)
        return 0

    lax.fori_loop(0, win_chunks, last, 0, unroll=True)
    finish_block(n_blocks - 1, m)


def _attn_specs(n_batch=1):
    def col(j):
        return pl.BlockSpec((n_batch, SEQ, GROUP_W), lambda b: (b, 0, j))
    return [col(0), col(1), col(2), col(3)]


def _attn_scratch(n_pairs, key_chunk, win_chunks, tab_rows):
    tq = WIDTH // n_pairs
    return [pltpu.VMEM((N_HEADS, tab_rows, tq), F32),
            pltpu.VMEM((ATTN_BATCH * SEQ // key_chunk, N_HEADS * VT_ROWS, key_chunk), BF16),
            pltpu.VMEM((win_chunks * key_chunk, WIDTH), F32),
            pltpu.VMEM((win_chunks * key_chunk, WIDTH), F32),
            pltpu.VMEM((GROUP_W, WIDTH), BF16),
            pltpu.VMEM((N_HEADS, VT_ROWS, PAIR_LANES), F32)]


def _mask_values(v_ref, vm_ref):
    v = v_ref[0]
    for h in range(N_HEADS):
        vm_ref[h * SEQ:(h + 1) * SEQ, :] = jnp.where(_head_mask(h), v, jnp.zeros_like(v))


A_PAIRS = N_HEADS
A_TQ = WIDTH // A_PAIRS
A_CHUNK = 256
A_WIN = 3
A_FAR = DILATED_PATTERNS[-1][1]
A_NEAR_PATTERNS = DILATED_PATTERNS[:-1]
A_NEAR_REACH = max(r * (w // (2 * r)) for w, r in A_NEAR_PATTERNS)
assert A_NEAR_REACH <= A_CHUNK and A_TQ == A_CHUNK
A_TAB_OFF = 2 * A_CHUNK
A_TAB_ROWS = A_TAB_OFF + A_WIN * A_CHUNK
A_CLASS = SEQ // A_FAR
A_PART = GROUP_W + LANES


def _a_key_start(i):
    return jnp.clip(i - 1, 0, SEQ // A_CHUNK - A_WIN)


def _a_far_partials(q_ref, k_ref, v_ref, tab_ref, x_ref, part_ref, seq):
    tiles = GROUP_W // LANES

    n = A_CLASS
    for a, ref in enumerate((q_ref, k_ref, v_ref)):
        for j in range(tiles):
            x = ref[seq, :, j * LANES:(j + 1) * LANES].astype(F32)
            x_ref[a * tiles + j] = jnp.swapaxes(x.reshape(n, A_FAR, LANES), 0, 1)

    feat = lax.broadcasted_iota(jnp.int32, (GROUP_W, n), 0)
    ones_rows = jnp.where(lax.broadcasted_iota(jnp.int32, (VT_ROWS - HEAD_DIM, n), 0) == 0, 1.0, 0.0).astype(BF16)
    pad = jnp.zeros((LANES - 2 * N_HEADS, n), F32)
    def class_scores(rho):
        cols = lambda a: jnp.concatenate([x_ref[a * tiles + j, rho] for j in range(tiles)], axis=1)
        qt = cols(0).T
        qt_all = jnp.concatenate(
            [jnp.where((feat >= h * HEAD_DIM) & (feat < (h + 1) * HEAD_DIM), qt, 0.0) for h in range(N_HEADS)],
            axis=1).astype(BF16)
        return _dot(cols(1).astype(BF16), qt_all)

    s_next = class_scores(0)
    for rho in range(A_FAR):
        s = s_next
        if rho + 1 < A_FAR:
            s_next = class_scores(rho + 1)
        cls = pl.ds(rho, n, stride=A_FAR)
        cols = lambda a: jnp.concatenate([x_ref[a * tiles + j, rho] for j in range(tiles)], axis=1)
        e = jnp.concatenate([s[:, h * n:(h + 1) * n] + tab_ref[h] for h in range(N_HEADS)], axis=1)
        m = jnp.max(e, axis=0, keepdims=True)
        pb = jnp.exp2(e - m).astype(BF16)
        vt = cols(2).T.astype(BF16)
        accs = [_dot(jnp.concatenate([vt[h * HEAD_DIM:(h + 1) * HEAD_DIM], ones_rows], axis=0),
                     pb[:, h * n:(h + 1) * n]) for h in range(N_HEADS)]
        stats = jnp.concatenate([m[:, h * n:(h + 1) * n] for h in range(N_HEADS)]
                                + [a[HEAD_DIM:HEAD_DIM + 1] for a in accs] + [pad], axis=0)
        rec = jnp.concatenate([jnp.concatenate([a[0:HEAD_DIM] for a in accs], axis=0).T, stats.T], axis=1)
        for j in range(A_PART // LANES):
            part_ref[seq * (A_PART // LANES) + j, cls, :] = rec[:, j * LANES:(j + 1) * LANES]


def _mixer_a_kernel(q_ref, k_ref, v_ref, g_ref, o_ref, tab_ref, vt_ref, e0_ref, e1_ref, qt_ref, acc_ref,
                    far_tab_ref, x_ref, part_ref, *, slopes):
    @pl.when(pl.program_id(0) == 0)
    def _():
        _build_toeplitz_t(tab_ref, slopes, A_NEAR_PATTERNS, A_TAB_OFF)
        _build_toeplitz_t(far_tab_ref, slopes, DILATED_PATTERNS[-1:], 0, unit=A_FAR)

    for seq in range(ATTN_BATCH):
        _a_far_partials(q_ref, k_ref, v_ref, far_tab_ref, x_ref, part_ref, seq)

    def bias(h, i, c):
        t0 = pl.multiple_of((_a_key_start(i) + c - i) * A_CHUNK + A_TAB_OFF, A_CHUNK)
        return tab_ref[h, pl.ds(t0, A_CHUNK), :]

    def finish(seq, i, m, sums, acc_ref):
        q0 = pl.multiple_of(i * A_TQ, A_TQ)
        p0 = seq * (A_PART // LANES)
        far = jnp.concatenate([part_ref[p0 + j, pl.ds(q0, A_TQ), :] for j in range(GROUP_W // LANES)], axis=1).T
        stats = part_ref[p0 + GROUP_W // LANES, pl.ds(q0, A_TQ), :].T
        outs = []
        for h in range(N_HEADS):
            lanes = slice(h * A_TQ, (h + 1) * A_TQ)
            m_far = stats[h:h + 1]
            top = jnp.maximum(m[:, lanes], m_far)
            w_near = jnp.exp2(m[:, lanes] - top)
            w_far = jnp.exp2(m_far - top)
            denom = sums[:, lanes] * w_near + stats[N_HEADS + h:N_HEADS + h + 1] * w_far
            outs.append((acc_ref[h, 0:HEAD_DIM, :] * w_near + far[h * HEAD_DIM:(h + 1) * HEAD_DIM] * w_far) / denom)
        g = g_ref[seq, pl.ds(q0, A_TQ), :].astype(F32)
        o_ref[seq, pl.ds(q0, A_TQ), :] = (jnp.concatenate(outs, axis=0).T * _silu(g)).astype(BF16)

    _attention_t(q_ref, k_ref, v_ref, vt_ref, (e0_ref, e1_ref), qt_ref, acc_ref, n_pairs=A_PAIRS,
                 key_chunk=A_CHUNK, win_chunks=A_WIN, key_start=_a_key_start, bias=bias, finish=finish)


def _mixer_a(pa, slopes):
    bsz = pa.shape[0]
    return pl.pallas_call(
        functools.partial(_mixer_a_kernel, slopes=slopes),
        grid=(bsz // ATTN_BATCH,),
        in_specs=_attn_specs(ATTN_BATCH),
        out_specs=pl.BlockSpec((ATTN_BATCH, SEQ, GROUP_W), lambda b: (b, 0, 0)),
        out_shape=jax.ShapeDtypeStruct((bsz, SEQ, GROUP_W), BF16),
        scratch_shapes=_attn_scratch(A_PAIRS, A_CHUNK, A_WIN, A_TAB_ROWS) + [
            pltpu.VMEM((N_HEADS, A_CLASS, A_CLASS), F32),
            pltpu.VMEM((3 * GROUP_W // LANES, A_FAR, A_CLASS, LANES), F32),
            pltpu.VMEM((ATTN_BATCH * A_PART // LANES, SEQ, LANES), F32)],
        compiler_params=_params(1),
        name="mixer_a_dilated",
    )(pa, pa, pa, pa)


B_PAIRS = 2 * N_HEADS
B_TQ = WIDTH // B_PAIRS


def _mixer_b_kernel(q_ref, k_ref, v_ref, g_ref, lam_ref, nw_ref, o_ref, tab_ref, vt_ref, e0_ref, e1_ref, qt_ref,
                    acc_ref, *, slopes, lam_init):
    tq = B_TQ

    @pl.when(pl.program_id(0) == 0)
    def _():
        _build_toeplitz_t(tab_ref, slopes, None, SEQ - tq)

    def bias(h, i, c):
        return tab_ref[h, pl.ds(pl.multiple_of(SEQ - tq - i * tq + c * KEY_CHUNK, LANES), KEY_CHUNK), :]

    lv = lam_ref[...]
    lam = (jnp.exp(jnp.sum(lv[0:1] * lv[1:2], axis=-1, keepdims=True))
           - jnp.exp(jnp.sum(lv[2:3] * lv[3:4], axis=-1, keepdims=True)) + lam_init)
    nw = nw_ref[...]

    def finish(seq, i, m, sums, acc_ref):
        q0 = pl.multiple_of(i * tq, tq)
        inv = 1.0 / sums
        outs = []
        for h in range(N_HEADS):
            acc = acc_ref[h, 0:HEAD_DIM, :]
            o = (acc[:, 0:tq] * inv[:, 2 * h * tq:(2 * h + 1) * tq]
                 - lam * (acc[:, tq:2 * tq] * inv[:, (2 * h + 1) * tq:(2 * h + 2) * tq]))
            ms = jnp.mean(o * o, axis=0, keepdims=True)
            outs.append(o * lax.rsqrt(ms + EPS))
        on = jnp.concatenate(outs, axis=0).T * nw * (1.0 - lam_init)
        g = g_ref[seq, pl.ds(q0, tq), :].astype(F32)
        o_ref[seq, pl.ds(q0, tq), :] = (on * _silu(g)).astype(BF16)

    _attention_t(q_ref, k_ref, v_ref, vt_ref, (e0_ref, e1_ref), qt_ref, acc_ref, n_pairs=B_PAIRS,
                 key_chunk=KEY_CHUNK, win_chunks=SEQ // KEY_CHUNK, key_start=lambda i: 0, bias=bias, finish=finish)


def _mixer_b(pb, lam_p, nw256, slopes, lam_init):
    bsz = pb.shape[0]
    return pl.pallas_call(
        functools.partial(_mixer_b_kernel, slopes=slopes, lam_init=lam_init),
        grid=(bsz // ATTN_BATCH,),
        in_specs=_attn_specs(ATTN_BATCH) + [pl.BlockSpec((4, DIFF_HEAD_DIM), lambda b: (0, 0)),
                                            pl.BlockSpec((1, GROUP_W), lambda b: (0, 0))],
        out_specs=pl.BlockSpec((ATTN_BATCH, SEQ, GROUP_W), lambda b: (b, 0, 0)),
        out_shape=jax.ShapeDtypeStruct((bsz, SEQ, GROUP_W), BF16),
        scratch_shapes=_attn_scratch(B_PAIRS, KEY_CHUNK, SEQ // KEY_CHUNK, 2 * SEQ - B_TQ),
        compiler_params=_params(1),
        name="mixer_b_diff",
    )(pb, pb, pb, pb, lam_p, nw256)


CONV_TILE = 256
CONV_HALO = SUBLANES
N_CHUNK = SEQ // SSM_CHUNK


def _split3_dot(x, w3):
    hi = x.astype(BF16)
    r1 = x - hi.astype(F32)
    mid = r1.astype(BF16)
    lo = (r1 - mid.astype(F32)).astype(BF16)
    return _dot(jnp.concatenate([hi, mid, lo], axis=1), w3)


def _cumsum_rows(a):
    row = lax.broadcasted_iota(jnp.int32, a.shape, 0)
    s = 1
    while s < a.shape[0]:
        a = a + jnp.where(row >= s, pltpu.roll(a, s, 0), 0.0)
        s *= 2
    return a


def _mixer_c_kernel(p_ref, dt_ref, cw_ref, cb_ref, alog_ref, alogx_ref, dtb_ref, dskip_ref, nw_ref, exp_ref,
                    o_ref, xpad, xc, y_s, sb_s, cs_s, db_s):
    L = SSM_CHUNK
    zero_rows = jnp.zeros((CONV_HALO, SSM_XBC), F32)
    xpad[0:CONV_HALO, :] = zero_rows
    xpad[CONV_HALO + SEQ:CONV_HALO + SEQ + CONV_HALO, :] = zero_rows

    def fill(i, _):
        r0 = pl.multiple_of(i * CONV_TILE, CONV_TILE)
        xpad[pl.ds(CONV_HALO + r0, CONV_TILE), :] = p_ref[0, pl.ds(r0, CONV_TILE), GROUP_W:].astype(F32)
        return 0

    lax.fori_loop(0, SEQ // CONV_TILE, fill, 0)

    def conv(i, _):
        r0 = pl.multiple_of(i * CONV_TILE, CONV_TILE)
        rows = CONV_TILE + 2 * CONV_HALO
        win = xpad[pl.ds(r0, rows), :]
        acc = jnp.zeros((CONV_TILE, SSM_XBC), F32) + cb_ref[...]
        for j in range(SSM_CONV):
            back = (SSM_CONV // 2 - j) % rows
            tap = win if back == 0 else pltpu.roll(win, back, 0)
            acc = acc + cw_ref[j:j + 1, :] * tap[CONV_HALO:CONV_HALO + CONV_TILE, :]
        xc[pl.ds(r0, CONV_TILE), :] = _silu(acc)
        return 0

    lax.fori_loop(0, SEQ // CONV_TILE, conv, 0)

    a_neg_x = -jnp.exp(alogx_ref[...])
    a_neg = -jnp.exp(alog_ref[...])
    expand = exp_ref[...]
    li = lax.broadcasted_iota(jnp.int32, (L, L), 0)
    si = lax.broadcasted_iota(jnp.int32, (L, L), 1)
    lower = si <= li
    upper = si >= li
    hmasks = [_head_mask(h) for h in range(N_HEADS)]

    def chunk_terms(t0):
        dtr = dt_ref[0, pl.ds(t0, L), :] + dtb_ref[...]
        dt = jnp.maximum(dtr, 0.0) + jnp.log(1.0 + jnp.exp(-jnp.abs(dtr)))
        a = dt * a_neg
        ainc = _cumsum_rows(a)
        aexc = ainc - a
        return dt, ainc, aexc

    def fwd(c, hf):
        t0 = pl.multiple_of(c * L, L)
        dt, ainc, aexc = chunk_terms(t0)
        both_x = _split3_dot(jnp.concatenate([dt, ainc], axis=0), expand)
        dt_x = both_x[0:L]
        ainc_x = both_x[L:2 * L]
        aexc_x = ainc_x - dt_x * a_neg_x
        ainc_t = ainc.T
        aexc_t = aexc.T
        xs = xc[pl.ds(t0, L), 0:GROUP_W]
        bm = xc[pl.ds(t0, L), GROUP_W:2 * GROUP_W]
        cm = xc[pl.ds(t0, L), 2 * GROUP_W:3 * GROUP_W]
        xf = xs * dt_x[:, 0:GROUP_W]
        xb = xs * dt_x[:, GROUP_W:2 * GROUP_W]
        xcat = jnp.concatenate([xf, xb], axis=0).astype(BF16)
        tot_f = ainc_x[L - 1:L, 0:GROUP_W]
        tot_b = ainc_x[L - 1:L, GROUP_W:2 * GROUP_W]
        y = xs * dskip_ref[...]
        cbs = []
        for g in range(SSM_GROUPS):
            gs = slice(g * SSM_STATE, (g + 1) * SSM_STATE)
            cbs.append(_dot_nt(cm[:, gs].astype(BF16), bm[:, gs].astype(BF16)))
        for h in range(N_HEADS):
            cb = cbs[h // (N_HEADS // SSM_GROUPS)]
            col_f = ainc[:, h:h + 1]
            row_f = ainc_t[h:h + 1, :]
            col_b = aexc[:, N_HEADS + h:N_HEADS + h + 1]
            row_b = aexc_t[N_HEADS + h:N_HEADS + h + 1, :]
            lf = jnp.exp(jnp.where(lower, col_f - row_f, NEG))
            ub = jnp.exp(jnp.where(upper, row_b - col_b, NEG))
            mcat = jnp.concatenate([cb * lf, cb * ub], axis=1).astype(BF16)
            y = y + jnp.where(hmasks[h], _dot(mcat, xcat), 0.0)
        wf = (jnp.exp(tot_f - ainc_x[:, 0:GROUP_W]) * xf).astype(BF16)
        wb = (jnp.exp(aexc_x[:, GROUP_W:2 * GROUP_W]) * xb).astype(BF16)
        ef = jnp.exp(ainc_x[:, 0:GROUP_W])
        cs_s[pl.ds(t0, L), :] = jnp.exp(tot_b - aexc_x[:, GROUP_W:2 * GROUP_W])
        db_s[pl.ds(pl.multiple_of(c * SUBLANES, SUBLANES), SUBLANES), :] = jnp.broadcast_to(
            jnp.exp(tot_b), (SUBLANES, GROUP_W))
        dec_f = jnp.exp(tot_f)
        hf_new = []
        yoff = []
        for g in range(SSM_GROUPS):
            gs = slice(g * SSM_STATE, (g + 1) * SSM_STATE)
            bt = bm[:, gs].T.astype(BF16)
            yoff.append(_dot(cm[:, gs].astype(BF16), hf[g].astype(BF16)))
            hf_new.append(dec_f[:, gs] * hf[g] + _dot(bt, wf[:, gs]))
            sb_s[pl.ds(t0, L), gs] = _dot(bt, wb[:, gs])
        y = y + jnp.concatenate(yoff, axis=1) * ef
        y_s[pl.ds(t0, L), :] = y
        return tuple(hf_new)

    h0 = tuple(jnp.zeros((SSM_STATE, SSM_STATE), F32) for _ in range(SSM_GROUPS))
    lax.fori_loop(0, N_CHUNK, fwd, h0, unroll=2)

    def bwd(i, hb):
        c = N_CHUNK - 1 - i
        t0 = pl.multiple_of(c * L, L)
        cm = xc[pl.ds(t0, L), 2 * GROUP_W:3 * GROUP_W]
        dec_b = db_s[pl.ds(pl.multiple_of(c * SUBLANES, SUBLANES), 1), :]
        yoff = []
        hb_new = []
        for g in range(SSM_GROUPS):
            gs = slice(g * SSM_STATE, (g + 1) * SSM_STATE)
            yoff.append(_dot(cm[:, gs].astype(BF16), hb[g].astype(BF16)))
            hb_new.append(dec_b[:, gs] * hb[g] + sb_s[pl.ds(t0, L), gs])
        y_s[pl.ds(t0, L), :] = y_s[pl.ds(t0, L), :] + jnp.concatenate(yoff, axis=1) * cs_s[pl.ds(t0, L), :]
        return tuple(hb_new)

    lax.fori_loop(0, N_CHUNK, bwd, h0, unroll=4)

    def fin(i, _):
        r0 = pl.multiple_of(i * CONV_TILE, CONV_TILE)
        z = p_ref[0, pl.ds(r0, CONV_TILE), 0:GROUP_W].astype(F32)
        y = y_s[pl.ds(r0, CONV_TILE), :] * _silu(z)
        parts = []
        for g in range(SSM_GROUPS):
            yg = y[:, g * SSM_STATE:(g + 1) * SSM_STATE]
            parts.append(yg * lax.rsqrt(jnp.mean(yg * yg, axis=-1, keepdims=True) + EPS))
        o_ref[0, pl.ds(r0, CONV_TILE), :] = (jnp.concatenate(parts, axis=1) * nw_ref[...]).astype(BF16)
        return 0

    lax.fori_loop(0, SEQ // CONV_TILE, fin, 0)


def _head_expand_matrix():
    j = lax.broadcasted_iota(jnp.int32, (LANES, 2 * GROUP_W), 0)
    c = lax.broadcasted_iota(jnp.int32, (LANES, 2 * GROUP_W), 1)
    return jnp.tile((j == c // HEAD_DIM).astype(BF16), (3, 1))


def _mixer_c(pc, pdt, conv_w, conv_b, a_log, dt_bias, d_skip, norm_w):
    bsz = pc.shape[0]
    pad8 = lambda v: jnp.pad(v.reshape(1, 2 * N_HEADS), ((0, 0), (0, LANES - 2 * N_HEADS)))
    small = lambda shape: pl.BlockSpec(shape, lambda b: (0,) * len(shape))
    return pl.pallas_call(
        _mixer_c_kernel,
        grid=(bsz,),
        in_specs=[pl.BlockSpec((1, SEQ, 1024), lambda b: (b, 0, 0)),
                  pl.BlockSpec((1, SEQ, DT_PAD), lambda b: (b, 0, 0)),
                  small((SSM_CONV, SSM_XBC)), small((1, SSM_XBC)),
                  small((1, LANES)), small((1, 2 * GROUP_W)), small((1, LANES)),
                  small((1, GROUP_W)), small((1, GROUP_W)),
                  small((3 * LANES, 2 * GROUP_W))],
        out_specs=pl.BlockSpec((1, SEQ, GROUP_W), lambda b: (b, 0, 0)),
        out_shape=jax.ShapeDtypeStruct((bsz, SEQ, GROUP_W), BF16),
        scratch_shapes=[pltpu.VMEM((SEQ + 2 * CONV_HALO, SSM_XBC), F32),
                        pltpu.VMEM((SEQ, SSM_XBC), F32),
                        pltpu.VMEM((SEQ, GROUP_W), F32),
                        pltpu.VMEM((SEQ, GROUP_W), F32),
                        pltpu.VMEM((SEQ, GROUP_W), F32),
                        pltpu.VMEM((N_CHUNK * SUBLANES, GROUP_W), F32)],
        compiler_params=_params(1),
        name="mixer_c_ssd",
    )(pc, pdt, conv_w, conv_b.reshape(1, SSM_XBC), pad8(a_log),
      jnp.repeat(a_log.reshape(-1), HEAD_DIM).reshape(1, 2 * GROUP_W), pad8(dt_bias),
      jnp.repeat(d_skip, HEAD_DIM).reshape(1, GROUP_W), norm_w.reshape(1, GROUP_W),
      _head_expand_matrix())


NA_ROWS = SEQ // GRID_W
NA_DR = 2 * NA_WIN_H - 1
NA_DC = 2 * NA_WIN_W - 1
NA_PAIRS = NA_DR - 1


def _na_bias_kernel(rpb_ref, o_ref):
    lh = pl.program_id(0)
    cq = lax.broadcasted_iota(jnp.int32, (GRID_W, LANES), 0)
    lane = lax.broadcasted_iota(jnp.int32, (GRID_W, LANES), 1)
    second = lane >= GRID_W
    ck = jnp.where(second, lane - GRID_W, lane)
    cs = jnp.clip(cq - NA_WIN_W // 2, 0, GRID_W - NA_WIN_W)
    inside = (ck >= cs) & (ck < cs + NA_WIN_W)
    dc = ck - cq + NA_WIN_W - 1
    base = lh * (NA_DR * NA_DC)
    rows = []
    for dr in range(NA_DR):
        acc = jnp.zeros((GRID_W, LANES), F32)
        for j in range(NA_DC):
            acc = jnp.where(dc == j, rpb_ref[base + dr * NA_DC + j], acc)
        rows.append(acc)
    for p in range(NA_PAIRS):
        o_ref[0, p] = jnp.where(inside, jnp.where(second, rows[p + 1], rows[p]), NEG)


def _na_bias(na_rpb):
    n = DEPTH * N_HEADS
    return pl.pallas_call(
        _na_bias_kernel,
        grid=(n,),
        in_specs=[pl.BlockSpec(memory_space=pltpu.SMEM)],
        out_specs=pl.BlockSpec((1, NA_PAIRS, GRID_W, LANES), lambda i: (i, 0, 0, 0)),
        out_shape=jax.ShapeDtypeStruct((n, NA_PAIRS, GRID_W, LANES), F32),
        compiler_params=_params(1),
        name="na_bias_table",
    )(na_rpb.reshape(-1))


NA_UNROLL = 8


def _mixer_d_kernel(q_ref, k_ref, v_ref, g_ref, bias_ref, o_ref, vm_ref):
    nkeys = NA_WIN_H * GRID_W
    _mask_values(v_ref, vm_ref)

    def score_row(r):
        rs = jnp.clip(r - NA_WIN_H // 2, 0, NA_ROWS - NA_WIN_H)
        k0 = pl.multiple_of(rs * GRID_W, GRID_W)
        q = q_ref[0, pl.ds(pl.multiple_of(r * GRID_W, GRID_W), GRID_W), :]
        qs = jnp.concatenate([jnp.where(_head_mask(h), q, jnp.zeros_like(q)) for h in range(N_HEADS)], axis=0)
        return _dot_nt(qs, k_ref[0, pl.ds(k0, nkeys), :])

    def prob_row(r, s_all):
        delta = r - jnp.clip(r - NA_WIN_H // 2, 0, NA_ROWS - NA_WIN_H)
        ps, invs = [], []
        for h in range(N_HEADS):
            bias = jnp.concatenate(
                [bias_ref[h, 2 * i - delta + NA_WIN_H - 1] for i in range(NA_WIN_H // 2)], axis=-1)
            e = s_all[h * GRID_W:(h + 1) * GRID_W] + bias
            m = jnp.max(e, axis=-1, keepdims=True)
            p = jnp.exp(e - m)
            invs.append(1.0 / jnp.sum(p, axis=-1, keepdims=True))
            ps.append(p.astype(BF16))
        inv = jnp.broadcast_to(invs[N_HEADS - 1], (GRID_W, GROUP_W))
        for h in range(N_HEADS - 2, -1, -1):
            inv = jnp.where(_head_mask(h), invs[h], inv)
        return jnp.concatenate(ps, axis=-1), inv

    def value_row(r, pcat, inv):
        rs = jnp.clip(r - NA_WIN_H // 2, 0, NA_ROWS - NA_WIN_H)
        k0 = pl.multiple_of(rs * GRID_W, GRID_W)
        q0 = pl.multiple_of(r * GRID_W, GRID_W)
        vcat = jnp.concatenate([vm_ref[pl.ds(pl.multiple_of(h * SEQ + k0, GRID_W), nkeys), :]
                                for h in range(N_HEADS)], axis=0)
        acc = _dot(pcat, vcat)
        g = g_ref[0, pl.ds(q0, GRID_W), :].astype(F32)
        o_ref[0, pl.ds(q0, GRID_W), :] = (acc * inv * _silu(g)).astype(BF16)

    def rstep(i, _):
        rows = [i * NA_UNROLL + u for u in range(NA_UNROLL)]
        s_next = score_row(rows[0])
        for u, r in enumerate(rows):
            s_cur = s_next
            if u + 1 < NA_UNROLL:
                s_next = score_row(rows[u + 1])
            value_row(r, *prob_row(r, s_cur))
        return 0

    lax.fori_loop(0, NA_ROWS // NA_UNROLL, rstep, 0)


def _mixer_d(pd, bias, l):
    bsz = pd.shape[0]
    return pl.pallas_call(
        _mixer_d_kernel,
        grid=(bsz,),
        in_specs=_attn_specs() + [pl.BlockSpec((N_HEADS, NA_PAIRS, GRID_W, LANES), lambda b: (l, 0, 0, 0))],
        out_specs=pl.BlockSpec((1, SEQ, GROUP_W), lambda b: (b, 0, 0)),
        out_shape=jax.ShapeDtypeStruct((bsz, SEQ, GROUP_W), BF16),
        scratch_shapes=[pltpu.VMEM((N_HEADS * SEQ, GROUP_W), BF16)],
        compiler_params=_params(1),
        name="mixer_d_neighbourhood",
    )(pd, pd, pd, pd, bias)


def kernel(x, c, norm_w, ada_w, ada_b, w_in, diff_lambda, diff_norm_w, conv_w, conv_b, ssm_a_log,
           ssm_dt_bias, ssm_d, ssm_norm_w, na_rpb, w_out, final_norm_w):
    bsz = x.shape[0]
    n = 2 * N_HEADS
    slopes = [2.0 ** (-8.0 * i / n) for i in range(1, n + 1)]
    slopes_a, slopes_b = tuple(slopes[0::2]), tuple(slopes[1::2])
    mod4 = _adaln(c, ada_w, ada_b).reshape(DEPTH, bsz, 1, 3 * D_MODEL)
    na_bias = _na_bias(na_rpb)
    w_main, w_dt = _prep_w_in(w_in)
    norm_w3 = norm_w.reshape(DEPTH, 1, D_MODEL)
    final_w = final_norm_w.reshape(1, D_MODEL)
    pa, pb, pc, pd, pdt = _inproj(x, mod4, norm_w3, w_main, w_dt, 0)
    for l in range(DEPTH):
        lam_init = 0.8 - 0.6 * math.exp(-0.3 * l)
        ya = _mixer_a(pa, slopes_a)
        yb = _mixer_b(pb, diff_lambda[l], jnp.tile(diff_norm_w[l], N_HEADS).reshape(1, GROUP_W),
                      slopes_b, lam_init)
        yc = _mixer_c(pc, pdt, conv_w[l], conv_b[l], ssm_a_log[l], ssm_dt_bias[l], ssm_d[l], ssm_norm_w[l])
        yd = _mixer_d(pd, na_bias, l)
        if l + 1 < DEPTH:
            x, pa, pb, pc, pd, pdt = _midproj((ya, yb, yc, yd), w_out, x, mod4, norm_w3, w_main, w_dt, l)
        else:
            x = _outproj((ya, yb, yc, yd), w_out, x, mod4, final_w, l)
    return x
```

```python
import functools
import math

import jax
import jax.numpy as jnp
from jax import lax
from jax.experimental import pallas as pl
from jax.experimental.pallas import tpu as pltpu

D_MODEL = 1024
SEQ = 2048
DEPTH = 2
HEAD_DIM = 64
GROUP_W = 256
N_HEADS = 4
EPS = 1e-6
DILATED_PATTERNS = ((128, 1), (512, 4), (2048, 16))
DIFF_HEAD_DIM = 32
SSM_GROUPS = 2
SSM_STATE = 128
SSM_CONV = 5
SSM_CHUNK = 128
SSM_XBC = 768
GRID_W = 64
NA_WIN_H = 8
NA_WIN_W = 16
D_IN = 13 * GROUP_W + SSM_XBC + 2 * N_HEADS

LANES = 128
SUBLANES = 8
VMEM_LIMIT = 56 * 1024 * 1024

NEG = -1e30
LOG2E = math.log2(math.e)
F32 = jnp.float32
BF16 = jnp.bfloat16
HIGHEST = lax.Precision.HIGHEST

ROW_TILE = 512
DT_PAD = LANES


def _silu(x):
    return x / (1.0 + jnp.exp(-x))


def _dot_nt(a, b):
    return lax.dot_general(a, b, (((1,), (1,)), ((), ())), preferred_element_type=F32)


def _dot(a, b):
    return jnp.dot(a, b, preferred_element_type=F32)


def _params(n_grid):
    return pltpu.CompilerParams(dimension_semantics=("arbitrary",) * n_grid,
                                vmem_limit_bytes=VMEM_LIMIT)


def _head_mask(h, width=HEAD_DIM, total=GROUP_W):
    lane = lax.broadcasted_iota(jnp.int32, (1, total), 1)
    return (lane >= h * width) & (lane < (h + 1) * width)


def _mod_kernel(c_ref, w_ref, b_ref, o_ref):
    c = c_ref[...]
    o_ref[0] = jnp.dot(_silu(c), w_ref[0], precision=HIGHEST,
                       preferred_element_type=F32) + b_ref[0]


def _adaln(c, ada_w, ada_b):
    bsz = c.shape[0]
    tn = 768
    return pl.pallas_call(
        _mod_kernel,
        grid=(DEPTH, 3 * D_MODEL // tn),
        in_specs=[pl.BlockSpec((bsz, D_MODEL), lambda l, j: (0, 0)),
                  pl.BlockSpec((1, D_MODEL, tn), lambda l, j: (l, 0, j)),
                  pl.BlockSpec((1, 1, tn), lambda l, j: (l, 0, j))],
        out_specs=pl.BlockSpec((1, bsz, tn), lambda l, j: (l, 0, j)),
        out_shape=jax.ShapeDtypeStruct((DEPTH, bsz, 3 * D_MODEL), F32),
        compiler_params=_params(2),
        name="adaln_mod",
    )(c, ada_w, ada_b.reshape(DEPTH, 1, 3 * D_MODEL))


W_MAIN = 16 * GROUP_W
DT_COL0 = 9 * GROUP_W + SSM_XBC


def _prep_w_in(w_in):
    scale = [1.0] * W_MAIN
    for c0, s in ((0, HEAD_DIM ** -0.5 * LOG2E), (4 * GROUP_W, DIFF_HEAD_DIM ** -0.5 * LOG2E),
                  (12 * GROUP_W, HEAD_DIM ** -0.5)):
        scale[c0:c0 + GROUP_W] = [s] * GROUP_W
    sc = jnp.asarray(scale, F32)
    d0 = DT_COL0 + 2 * N_HEADS
    w_main = jnp.concatenate([(w_in[:, :, :DT_COL0] * sc[:DT_COL0]).astype(BF16),
                              (w_in[:, :, d0:] * sc[DT_COL0:]).astype(BF16)], axis=2)
    w_dt = jnp.pad(w_in[:, :, DT_COL0:d0].astype(BF16), ((0, 0), (0, 0), (0, DT_PAD - 2 * N_HEADS)))
    return w_main, w_dt


def _inproj_kernel(x_ref, mod_ref, nw_ref, w_ref, wdt_ref, pa_ref, pb_ref, pc_ref, pd_ref, pdt_ref):
    x = x_ref[0]
    shift = mod_ref[:, 0:D_MODEL]
    scale = mod_ref[:, D_MODEL:2 * D_MODEL]
    y = x * lax.rsqrt(jnp.mean(x * x, axis=-1, keepdims=True) + EPS) * nw_ref[...]
    h = (y * (1.0 + scale) + shift).astype(BF16)
    for i, ref in enumerate((pa_ref, pb_ref, pc_ref, pd_ref)):
        ref[0] = _dot(h, w_ref[:, i * 1024:(i + 1) * 1024]).astype(BF16)
    pdt_ref[0] = _dot(h, wdt_ref[...])


def _inproj(x, mod4, norm_w, w_main, w_dt, l):
    bsz = x.shape[0]
    row = lambda b, t: (b, t, 0)
    big = pl.BlockSpec((1, ROW_TILE, 1024), row)
    return pl.pallas_call(
        _inproj_kernel,
        grid=(bsz, SEQ // ROW_TILE),
        in_specs=[pl.BlockSpec((1, ROW_TILE, D_MODEL), row),
                  pl.BlockSpec((None, None, 1, 3 * D_MODEL), lambda b, t: (l, b, 0, 0)),
                  pl.BlockSpec((None, 1, D_MODEL), lambda b, t: (l, 0, 0)),
                  pl.BlockSpec((None, D_MODEL, W_MAIN), lambda b, t: (l, 0, 0)),
                  pl.BlockSpec((None, D_MODEL, DT_PAD), lambda b, t: (l, 0, 0))],
        out_specs=[big, big, big, big, pl.BlockSpec((1, ROW_TILE, DT_PAD), row)],
        out_shape=[jax.ShapeDtypeStruct((bsz, SEQ, 1024), BF16)] * 4
        + [jax.ShapeDtypeStruct((bsz, SEQ, DT_PAD), F32)],
        compiler_params=_params(2),
        name="inproj",
    )(x, mod4, norm_w, w_main, w_dt)


def _outproj_kernel(ya_ref, yb_ref, yc_ref, yd_ref, w_ref, x_ref, mod_ref, fw_ref, o_ref, wb_ref, *, final):
    @pl.when((pl.program_id(0) == 0) & (pl.program_id(1) == 0))
    def _():
        wb_ref[...] = w_ref[...].astype(BF16)

    acc = None
    for i, ref in enumerate((ya_ref, yb_ref, yc_ref, yd_ref)):
        part = _dot(ref[0], wb_ref[i * GROUP_W:(i + 1) * GROUP_W, :])
        acc = part if acc is None else acc + part
    gate = mod_ref[:, 2 * D_MODEL:3 * D_MODEL]
    xn = x_ref[0] + gate * acc
    if final:
        xn = xn * lax.rsqrt(jnp.mean(xn * xn, axis=-1, keepdims=True) + EPS) * fw_ref[...]
    o_ref[0] = xn


def _outproj(ys, w_out, x, mod4, final_w, l):
    bsz = x.shape[0]
    final = l == DEPTH - 1
    row = lambda b, t: (b, t, 0)
    yspec = pl.BlockSpec((1, ROW_TILE, GROUP_W), row)
    return pl.pallas_call(
        functools.partial(_outproj_kernel, final=final),
        grid=(bsz, SEQ // ROW_TILE),
        in_specs=[yspec, yspec, yspec, yspec,
                  pl.BlockSpec((None, D_MODEL, D_MODEL), lambda b, t: (l, 0, 0)),
                  pl.BlockSpec((1, ROW_TILE, D_MODEL), row),
                  pl.BlockSpec((None, None, 1, 3 * D_MODEL), lambda b, t: (l, b, 0, 0)),
                  pl.BlockSpec((1, D_MODEL), lambda b, t: (0, 0))],
        out_specs=pl.BlockSpec((1, ROW_TILE, D_MODEL), row),
        out_shape=jax.ShapeDtypeStruct((bsz, SEQ, D_MODEL), F32),
        scratch_shapes=[pltpu.VMEM((D_MODEL, D_MODEL), BF16)],
        compiler_params=_params(2),
        name="outproj_final" if final else "outproj",
    )(*ys, w_out, x, mod4, final_w)


def _midproj_kernel(ya_ref, yb_ref, yc_ref, yd_ref, wo_ref, x_ref, mod_ref, modn_ref, nw_ref, w_ref, wdt_ref,
                    xo_ref, pa_ref, pb_ref, pc_ref, pd_ref, pdt_ref, wb_ref):
    @pl.when((pl.program_id(0) == 0) & (pl.program_id(1) == 0))
    def _():
        wb_ref[...] = wo_ref[...].astype(BF16)

    acc = None
    for i, ref in enumerate((ya_ref, yb_ref, yc_ref, yd_ref)):
        part = _dot(ref[0], wb_ref[i * GROUP_W:(i + 1) * GROUP_W, :])
        acc = part if acc is None else acc + part
    x = x_ref[0] + mod_ref[:, 2 * D_MODEL:3 * D_MODEL] * acc
    xo_ref[0] = x
    shift = modn_ref[:, 0:D_MODEL]
    scale = modn_ref[:, D_MODEL:2 * D_MODEL]
    y = x * lax.rsqrt(jnp.mean(x * x, axis=-1, keepdims=True) + EPS) * nw_ref[...]
    h = (y * (1.0 + scale) + shift).astype(BF16)
    for i, ref in enumerate((pa_ref, pb_ref, pc_ref, pd_ref)):
        ref[0] = _dot(h, w_ref[:, i * 1024:(i + 1) * 1024]).astype(BF16)
    pdt_ref[0] = _dot(h, wdt_ref[...])


def _midproj(ys, w_out, x, mod4, norm_w, w_main, w_dt, l):
    bsz = x.shape[0]
    row = lambda b, t: (b, t, 0)
    yspec = pl.BlockSpec((1, ROW_TILE, GROUP_W), row)
    big = pl.BlockSpec((1, ROW_TILE, 1024), row)
    xspec = pl.BlockSpec((1, ROW_TILE, D_MODEL), row)
    once = pl.Buffered(1)
    return pl.pallas_call(
        _midproj_kernel,
        grid=(bsz, SEQ // ROW_TILE),
        in_specs=[yspec, yspec, yspec, yspec,
                  pl.BlockSpec((None, D_MODEL, D_MODEL), lambda b, t: (l, 0, 0), pipeline_mode=once),
                  xspec,
                  pl.BlockSpec((None, None, 1, 3 * D_MODEL), lambda b, t: (l, b, 0, 0)),
                  pl.BlockSpec((None, None, 1, 3 * D_MODEL), lambda b, t: (l + 1, b, 0, 0)),
                  pl.BlockSpec((None, 1, D_MODEL), lambda b, t: (l + 1, 0, 0)),
                  pl.BlockSpec((None, D_MODEL, W_MAIN), lambda b, t: (l + 1, 0, 0), pipeline_mode=once),
                  pl.BlockSpec((None, D_MODEL, DT_PAD), lambda b, t: (l + 1, 0, 0))],
        out_specs=[xspec, big, big, big, big, pl.BlockSpec((1, ROW_TILE, DT_PAD), row)],
        out_shape=[jax.ShapeDtypeStruct((bsz, SEQ, D_MODEL), F32)]
        + [jax.ShapeDtypeStruct((bsz, SEQ, 1024), BF16)] * 4
        + [jax.ShapeDtypeStruct((bsz, SEQ, DT_PAD), F32)],
        scratch_shapes=[pltpu.VMEM((D_MODEL, D_MODEL), BF16)],
        compiler_params=_params(2),
        name="midproj",
    )(*ys, w_out, x, mod4, mod4, norm_w, w_main, w_dt)


KEY_CHUNK = 256
WIDTH = 1024
PAIR_LANES = WIDTH // N_HEADS
VT_ROWS = HEAD_DIM + 16
ATTN_BATCH = 2


def _build_toeplitz_t(tab_ref, slopes, patterns, off, unit=1):
    _, rows, tq = tab_ref.shape
    step = min(rows, 512)
    i_io = lax.broadcasted_iota(jnp.int32, (step, tq), 0)
    r_io = lax.broadcasted_iota(jnp.int32, (step, tq), 1)
    base = (i_io - r_io - off) * unit

    def write(i0, n):
        d = base[:n] + i0 * unit
        ad = jnp.abs(d)
        adf = ad.astype(F32)
        logm, valid = 0.0, None
        if patterns is not None:
            mult = jnp.zeros(d.shape, F32)
            for w, r in patterns:
                reach = r * (w // (2 * r))
                mult = mult + jnp.where(ad <= reach, jnp.where((d & (r - 1)) == 0, 1.0, 0.0), 0.0)
            valid = mult > 0.5
            logm = jnp.log(jnp.maximum(mult, 1.0))
        for h, slope in enumerate(slopes):
            val = (logm - slope * adf) * LOG2E
            if valid is not None:
                val = jnp.where(valid, val, NEG)
            tab_ref[h, pl.ds(i0, n), :] = val

    def body(j, _):
        write(pl.multiple_of(j * step, step), step)
        return 0

    lax.fori_loop(0, rows // step, body, 0)
    if rows % step:
        write(rows - rows % step, rows % step)


def _sublane_groups(x):
    return x.reshape(x.shape[0] // SUBLANES, SUBLANES, x.shape[1])


def _attention_t(q_ref, k_ref, v_ref, vt_ref, e_refs, qt_ref, acc_ref, *, n_pairs, key_chunk, win_chunks,
                 key_start, bias, finish):
    n_batch = ATTN_BATCH
    tq = WIDTH // n_pairs
    fw = GROUP_W // n_pairs
    per_seq = SEQ // tq
    n_blocks = n_batch * per_seq
    key_chunks = SEQ // key_chunk
    ones_row = jnp.where(lax.broadcasted_iota(jnp.int32, (VT_ROWS - HEAD_DIM, key_chunk), 0) == 0, 1.0, 0.0)
    for bi in range(n_batch):
        for c in range(key_chunks):
            rows = slice(c * key_chunk, (c + 1) * key_chunk)
            vt = v_ref[bi, rows, :].astype(F32).T.astype(BF16)
            for h in range(N_HEADS):
                r0 = h * VT_ROWS
                vt_ref[bi * key_chunks + c, r0:r0 + HEAD_DIM, :] = vt[h * HEAD_DIM:(h + 1) * HEAD_DIM]
                vt_ref[bi * key_chunks + c, r0 + HEAD_DIM:r0 + VT_ROWS, :] = ones_row.astype(BF16)

    feat = lax.broadcasted_iota(jnp.int32, (GROUP_W, tq), 0)
    mx0 = jnp.full((SUBLANES, WIDTH), NEG, F32)

    def split(i):
        return i // per_seq, i % per_seq

    def load_queries(i):
        bi, li = split(i)
        qt = q_ref[bi, pl.ds(pl.multiple_of(li * tq, tq), tq), :].astype(F32).T
        for p in range(n_pairs):
            keep = (feat >= p * fw) & (feat < (p + 1) * fw)
            qt_ref[:, p * tq:(p + 1) * tq] = jnp.where(keep, qt, 0.0).astype(BF16)

    pph = n_pairs // N_HEADS

    def scores_head(c, i, h, e_ref, mx):
        lanes = slice(h * PAIR_LANES, (h + 1) * PAIR_LANES)
        bi, li = split(i)
        k0 = pl.multiple_of((key_start(li) + c) * key_chunk, key_chunk)
        s = _dot(k_ref[bi, pl.ds(k0, key_chunk), :], qt_ref[:, lanes])
        b = bias(h, li, c)
        e = jnp.concatenate([s[:, p * tq:(p + 1) * tq] + b for p in range(pph)], axis=1)
        e_ref[pl.ds(pl.multiple_of(c * key_chunk, key_chunk), key_chunk), lanes] = e
        return jnp.maximum(mx, jnp.max(_sublane_groups(e), axis=0))

    def scores(c, i, e_ref, mx):
        parts = [scores_head(c, i, h, e_ref, mx[:, h * PAIR_LANES:(h + 1) * PAIR_LANES]) for h in range(N_HEADS)]
        return jnp.concatenate(parts, axis=1)

    def probs_head(c, i, h, e_ref, m):
        lanes = slice(h * PAIR_LANES, (h + 1) * PAIR_LANES)
        r0 = pl.multiple_of(c * key_chunk, key_chunk)
        pb = jnp.exp2(e_ref[pl.ds(r0, key_chunk), lanes] - m[:, lanes]).astype(BF16)
        bi, li = split(i)
        acc_ref[h] += _dot(vt_ref[bi * key_chunks + key_start(li) + c, h * VT_ROWS:(h + 1) * VT_ROWS, :], pb)

    def probs(c, i, e_ref, m):
        for h in range(N_HEADS):
            probs_head(c, i, h, e_ref, m)

    def finish_block(i, m):
        sums = jnp.concatenate([acc_ref[h, HEAD_DIM:HEAD_DIM + 1, :] for h in range(N_HEADS)], axis=1)
        finish(*split(i), m, sums, acc_ref)

    load_queries(0)
    mx = lax.fori_loop(0, win_chunks, lambda c, mx: scores(c, 0, e_refs[0], mx), mx0, unroll=True)

    def step(i, slot, m):
        load_queries(i + 1)
        acc_ref[...] = jnp.zeros_like(acc_ref)

        def both(c, mx):
            parts = []
            for h in range(N_HEADS):
                parts.append(scores_head(c, i + 1, h, e_refs[1 - slot], mx[:, h * PAIR_LANES:(h + 1) * PAIR_LANES]))
                probs_head(c, i, h, e_refs[slot], m)
            return jnp.concatenate(parts, axis=1)

        mx = lax.fori_loop(0, win_chunks, both, mx0, unroll=True)
        finish_block(i, m)
        return jnp.max(mx, axis=0, keepdims=True)

    def two_steps(j, m):
        return step(2 * j + 1, 1, step(2 * j, 0, m))

    m = lax.fori_loop(0, n_blocks // 2 - 1, two_steps, jnp.max(mx, axis=0, keepdims=True))
    m = step(n_blocks - 2, 0, m)
    acc_ref[...] = jnp.zeros_like(acc_ref)

    def last(c, _):
        probs(c, n_blocks - 1, e_refs[1], m)
        return 0

    lax.fori_loop(0, win_chunks, last, 0, unroll=True)
    finish_block(n_blocks - 1, m)


def _attn_specs(n_batch=1):
    def col(j):
        return pl.BlockSpec((n_batch, SEQ, GROUP_W), lambda b: (b, 0, j))
    return [col(0), col(1), col(2), col(3)]


def _attn_scratch(n_pairs, key_chunk, win_chunks, tab_rows):
    tq = WIDTH // n_pairs
    return [pltpu.VMEM((N_HEADS, tab_rows, tq), F32),
            pltpu.VMEM((ATTN_BATCH * SEQ // key_chunk, N_HEADS * VT_ROWS, key_chunk), BF16),
            pltpu.VMEM((win_chunks * key_chunk, WIDTH), F32),
            pltpu.VMEM((win_chunks * key_chunk, WIDTH), F32),
            pltpu.VMEM((GROUP_W, WIDTH), BF16),
            pltpu.VMEM((N_HEADS, VT_ROWS, PAIR_LANES), F32)]


def _mask_values(v_ref, vm_ref):
    v = v_ref[0]
    for h in range(N_HEADS):
        vm_ref[h * SEQ:(h + 1) * SEQ, :] = jnp.where(_head_mask(h), v, jnp.zeros_like(v))


A_PAIRS = N_HEADS
A_TQ = WIDTH // A_PAIRS
A_CHUNK = 256
A_WIN = 3
A_FAR = DILATED_PATTERNS[-1][1]
A_NEAR_PATTERNS = DILATED_PATTERNS[:-1]
A_NEAR_REACH = max(r * (w // (2 * r)) for w, r in A_NEAR_PATTERNS)
assert A_NEAR_REACH <= A_CHUNK and A_TQ == A_CHUNK
A_TAB_OFF = 2 * A_CHUNK
A_TAB_ROWS = A_TAB_OFF + A_WIN * A_CHUNK
A_CLASS = SEQ // A_FAR
A_PART = GROUP_W + LANES


def _a_key_start(i):
    return jnp.clip(i - 1, 0, SEQ // A_CHUNK - A_WIN)


def _a_far_partials(q_ref, k_ref, v_ref, tab_ref, x_ref, part_ref, seq):
    tiles = GROUP_W // LANES

    n = A_CLASS
    for a, ref in enumerate((q_ref, k_ref, v_ref)):
        for j in range(tiles):
            x = ref[seq, :, j * LANES:(j + 1) * LANES].astype(F32)
            x_ref[a * tiles + j] = jnp.swapaxes(x.reshape(n, A_FAR, LANES), 0, 1)

    feat = lax.broadcasted_iota(jnp.int32, (GROUP_W, n), 0)
    ones_rows = jnp.where(lax.broadcasted_iota(jnp.int32, (VT_ROWS - HEAD_DIM, n), 0) == 0, 1.0, 0.0).astype(BF16)
    pad = jnp.zeros((LANES - 2 * N_HEADS, n), F32)
    def class_scores(rho):
        cols = lambda a: jnp.concatenate([x_ref[a * tiles + j, rho] for j in range(tiles)], axis=1)
        qt = cols(0).T
        qt_all = jnp.concatenate(
            [jnp.where((feat >= h * HEAD_DIM) & (feat < (h + 1) * HEAD_DIM), qt, 0.0) for h in range(N_HEADS)],
            axis=1).astype(BF16)
        return _dot(cols(1).astype(BF16), qt_all)

    s_next = class_scores(0)
    for rho in range(A_FAR):
        s = s_next
        if rho + 1 < A_FAR:
            s_next = class_scores(rho + 1)
        cls = pl.ds(rho, n, stride=A_FAR)
        cols = lambda a: jnp.concatenate([x_ref[a * tiles + j, rho] for j in range(tiles)], axis=1)
        e = jnp.concatenate([s[:, h * n:(h + 1) * n] + tab_ref[h] for h in range(N_HEADS)], axis=1)
        m = jnp.max(e, axis=0, keepdims=True)
        pb = jnp.exp2(e - m).astype(BF16)
        vt = cols(2).T.astype(BF16)
        accs = [_dot(jnp.concatenate([vt[h * HEAD_DIM:(h + 1) * HEAD_DIM], ones_rows], axis=0),
                     pb[:, h * n:(h + 1) * n]) for h in range(N_HEADS)]
        stats = jnp.concatenate([m[:, h * n:(h + 1) * n] for h in range(N_HEADS)]
                                + [a[HEAD_DIM:HEAD_DIM + 1] for a in accs] + [pad], axis=0)
        rec = jnp.concatenate([jnp.concatenate([a[0:HEAD_DIM] for a in accs], axis=0).T, stats.T], axis=1)
        for j in range(A_PART // LANES):
            part_ref[seq * (A_PART // LANES) + j, cls, :] = rec[:, j * LANES:(j + 1) * LANES]


def _mixer_a_kernel(q_ref, k_ref, v_ref, g_ref, o_ref, tab_ref, vt_ref, e0_ref, e1_ref, qt_ref, acc_ref,
                    far_tab_ref, x_ref, part_ref, *, slopes):
    @pl.when(pl.program_id(0) == 0)
    def _():
        _build_toeplitz_t(tab_ref, slopes, A_NEAR_PATTERNS, A_TAB_OFF)
        _build_toeplitz_t(far_tab_ref, slopes, DILATED_PATTERNS[-1:], 0, unit=A_FAR)

    for seq in range(ATTN_BATCH):
        _a_far_partials(q_ref, k_ref, v_ref, far_tab_ref, x_ref, part_ref, seq)

    def bias(h, i, c):
        t0 = pl.multiple_of((_a_key_start(i) + c - i) * A_CHUNK + A_TAB_OFF, A_CHUNK)
        return tab_ref[h, pl.ds(t0, A_CHUNK), :]

    def finish(seq, i, m, sums, acc_ref):
        q0 = pl.multiple_of(i * A_TQ, A_TQ)
        p0 = seq * (A_PART // LANES)
        far = jnp.concatenate([part_ref[p0 + j, pl.ds(q0, A_TQ), :] for j in range(GROUP_W // LANES)], axis=1).T
        stats = part_ref[p0 + GROUP_W // LANES, pl.ds(q0, A_TQ), :].T
        outs = []
        for h in range(N_HEADS):
            lanes = slice(h * A_TQ, (h + 1) * A_TQ)
            m_far = stats[h:h + 1]
            top = jnp.maximum(m[:, lanes], m_far)
            w_near = jnp.exp2(m[:, lanes] - top)
            w_far = jnp.exp2(m_far - top)
            denom = sums[:, lanes] * w_near + stats[N_HEADS + h:N_HEADS + h + 1] * w_far
            outs.append((acc_ref[h, 0:HEAD_DIM, :] * w_near + far[h * HEAD_DIM:(h + 1) * HEAD_DIM] * w_far) / denom)
        g = g_ref[seq, pl.ds(q0, A_TQ), :].astype(F32)
        o_ref[seq, pl.ds(q0, A_TQ), :] = (jnp.concatenate(outs, axis=0).T * _silu(g)).astype(BF16)

    _attention_t(q_ref, k_ref, v_ref, vt_ref, (e0_ref, e1_ref), qt_ref, acc_ref, n_pairs=A_PAIRS,
                 key_chunk=A_CHUNK, win_chunks=A_WIN, key_start=_a_key_start, bias=bias, finish=finish)


def _mixer_a(pa, slopes):
    bsz = pa.shape[0]
    return pl.pallas_call(
        functools.partial(_mixer_a_kernel, slopes=slopes),
        grid=(bsz // ATTN_BATCH,),
        in_specs=_attn_specs(ATTN_BATCH),
        out_specs=pl.BlockSpec((ATTN_BATCH, SEQ, GROUP_W), lambda b: (b, 0, 0)),
        out_shape=jax.ShapeDtypeStruct((bsz, SEQ, GROUP_W), BF16),
        scratch_shapes=_attn_scratch(A_PAIRS, A_CHUNK, A_WIN, A_TAB_ROWS) + [
            pltpu.VMEM((N_HEADS, A_CLASS, A_CLASS), F32),
            pltpu.VMEM((3 * GROUP_W // LANES, A_FAR, A_CLASS, LANES), F32),
            pltpu.VMEM((ATTN_BATCH * A_PART // LANES, SEQ, LANES), F32)],
        compiler_params=_params(1),
        name="mixer_a_dilated",
    )(pa, pa, pa, pa)


B_PAIRS = 2 * N_HEADS
B_TQ = WIDTH // B_PAIRS


def _mixer_b_kernel(q_ref, k_ref, v_ref, g_ref, lam_ref, nw_ref, o_ref, tab_ref, vt_ref, e0_ref, e1_ref, qt_ref,
                    acc_ref, *, slopes, lam_init):
    tq = B_TQ

    @pl.when(pl.program_id(0) == 0)
    def _():
        _build_toeplitz_t(tab_ref, slopes, None, SEQ - tq)

    def bias(h, i, c):
        return tab_ref[h, pl.ds(pl.multiple_of(SEQ - tq - i * tq + c * KEY_CHUNK, LANES), KEY_CHUNK), :]

    lv = lam_ref[...]
    lam = (jnp.exp(jnp.sum(lv[0:1] * lv[1:2], axis=-1, keepdims=True))
           - jnp.exp(jnp.sum(lv[2:3] * lv[3:4], axis=-1, keepdims=True)) + lam_init)
    nw = nw_ref[...]

    def finish(seq, i, m, sums, acc_ref):
        q0 = pl.multiple_of(i * tq, tq)
        inv = 1.0 / sums
        outs = []
        for h in range(N_HEADS):
            acc = acc_ref[h, 0:HEAD_DIM, :]
            o = (acc[:, 0:tq] * inv[:, 2 * h * tq:(2 * h + 1) * tq]
                 - lam * (acc[:, tq:2 * tq] * inv[:, (2 * h + 1) * tq:(2 * h + 2) * tq]))
            ms = jnp.mean(o * o, axis=0, keepdims=True)
            outs.append(o * lax.rsqrt(ms + EPS))
        on = jnp.concatenate(outs, axis=0).T * nw * (1.0 - lam_init)
        g = g_ref[seq, pl.ds(q0, tq), :].astype(F32)
        o_ref[seq, pl.ds(q0, tq), :] = (on * _silu(g)).astype(BF16)

    _attention_t(q_ref, k_ref, v_ref, vt_ref, (e0_ref, e1_ref), qt_ref, acc_ref, n_pairs=B_PAIRS,
                 key_chunk=KEY_CHUNK, win_chunks=SEQ // KEY_CHUNK, key_start=lambda i: 0, bias=bias, finish=finish)


def _mixer_b(pb, lam_p, nw256, slopes, lam_init):
    bsz = pb.shape[0]
    return pl.pallas_call(
        functools.partial(_mixer_b_kernel, slopes=slopes, lam_init=lam_init),
        grid=(bsz // ATTN_BATCH,),
        in_specs=_attn_specs(ATTN_BATCH) + [pl.BlockSpec((4, DIFF_HEAD_DIM), lambda b: (0, 0)),
                                            pl.BlockSpec((1, GROUP_W), lambda b: (0, 0))],
        out_specs=pl.BlockSpec((ATTN_BATCH, SEQ, GROUP_W), lambda b: (b, 0, 0)),
        out_shape=jax.ShapeDtypeStruct((bsz, SEQ, GROUP_W), BF16),
        scratch_shapes=_attn_scratch(B_PAIRS, KEY_CHUNK, SEQ // KEY_CHUNK, 2 * SEQ - B_TQ),
        compiler_params=_params(1),
        name="mixer_b_diff",
    )(pb, pb, pb, pb, lam_p, nw256)


CONV_TILE = 256
CONV_HALO = SUBLANES
N_CHUNK = SEQ // SSM_CHUNK


def _split3_dot(x, w3):
    hi = x.astype(BF16)
    r1 = x - hi.astype(F32)
    mid = r1.astype(BF16)
    lo = (r1 - mid.astype(F32)).astype(BF16)
    return _dot(jnp.concatenate([hi, mid, lo], axis=1), w3)


def _cumsum_rows(a):
    row = lax.broadcasted_iota(jnp.int32, a.shape, 0)
    s = 1
    while s < a.shape[0]:
        a = a + jnp.where(row >= s, pltpu.roll(a, s, 0), 0.0)
        s *= 2
    return a


def _mixer_c_kernel(p_ref, dt_ref, cw_ref, cb_ref, alog_ref, alogx_ref, dtb_ref, dskip_ref, nw_ref, exp_ref,
                    o_ref, xpad, xc, y_s, sb_s, cs_s, db_s):
    L = SSM_CHUNK
    zero_rows = jnp.zeros((CONV_HALO, SSM_XBC), F32)
    xpad[0:CONV_HALO, :] = zero_rows
    xpad[CONV_HALO + SEQ:CONV_HALO + SEQ + CONV_HALO, :] = zero_rows

    def fill(i, _):
        r0 = pl.multiple_of(i * CONV_TILE, CONV_TILE)
        xpad[pl.ds(CONV_HALO + r0, CONV_TILE), :] = p_ref[0, pl.ds(r0, CONV_TILE), GROUP_W:].astype(F32)
        return 0

    lax.fori_loop(0, SEQ // CONV_TILE, fill, 0)

    def conv(i, _):
        r0 = pl.multiple_of(i * CONV_TILE, CONV_TILE)
        rows = CONV_TILE + 2 * CONV_HALO
        win = xpad[pl.ds(r0, rows), :]
        acc = jnp.zeros((CONV_TILE, SSM_XBC), F32) + cb_ref[...]
        for j in range(SSM_CONV):
            back = (SSM_CONV // 2 - j) % rows
            tap = win if back == 0 else pltpu.roll(win, back, 0)
            acc = acc + cw_ref[j:j + 1, :] * tap[CONV_HALO:CONV_HALO + CONV_TILE, :]
        xc[pl.ds(r0, CONV_TILE), :] = _silu(acc)
        return 0

    lax.fori_loop(0, SEQ // CONV_TILE, conv, 0)

    a_neg_x = -jnp.exp(alogx_ref[...])
    a_neg = -jnp.exp(alog_ref[...])
    expand = exp_ref[...]
    li = lax.broadcasted_iota(jnp.int32, (L, L), 0)
    si = lax.broadcasted_iota(jnp.int32, (L, L), 1)
    lower = si <= li
    upper = si >= li
    hmasks = [_head_mask(h) for h in range(N_HEADS)]

    def chunk_terms(t0):
        dtr = dt_ref[0, pl.ds(t0, L), :] + dtb_ref[...]
        dt = jnp.maximum(dtr, 0.0) + jnp.log(1.0 + jnp.exp(-jnp.abs(dtr)))
        a = dt * a_neg
        ainc = _cumsum_rows(a)
        aexc = ainc - a
        return dt, ainc, aexc

    def fwd(c, hf):
        t0 = pl.multiple_of(c * L, L)
        dt, ainc, aexc = chunk_terms(t0)
        both_x = _split3_dot(jnp.concatenate([dt, ainc], axis=0), expand)
        dt_x = both_x[0:L]
        ainc_x = both_x[L:2 * L]
        aexc_x = ainc_x - dt_x * a_neg_x
        ainc_t = ainc.T
        aexc_t = aexc.T
        xs = xc[pl.ds(t0, L), 0:GROUP_W]
        bm = xc[pl.ds(t0, L), GROUP_W:2 * GROUP_W]
        cm = xc[pl.ds(t0, L), 2 * GROUP_W:3 * GROUP_W]
        xf = xs * dt_x[:, 0:GROUP_W]
        xb = xs * dt_x[:, GROUP_W:2 * GROUP_W]
        xcat = jnp.concatenate([xf, xb], axis=0).astype(BF16)
        tot_f = ainc_x[L - 1:L, 0:GROUP_W]
        tot_b = ainc_x[L - 1:L, GROUP_W:2 * GROUP_W]
        y = xs * dskip_ref[...]
        cbs = []
        for g in range(SSM_GROUPS):
            gs = slice(g * SSM_STATE, (g + 1) * SSM_STATE)
            cbs.append(_dot_nt(cm[:, gs].astype(BF16), bm[:, gs].astype(BF16)))
        for h in range(N_HEADS):
            cb = cbs[h // (N_HEADS // SSM_GROUPS)]
            col_f = ainc[:, h:h + 1]
            row_f = ainc_t[h:h + 1, :]
            col_b = aexc[:, N_HEADS + h:N_HEADS + h + 1]
            row_b = aexc_t[N_HEADS + h:N_HEADS + h + 1, :]
            lf = jnp.exp(jnp.where(lower, col_f - row_f, NEG))
            ub = jnp.exp(jnp.where(upper, row_b - col_b, NEG))
            mcat = jnp.concatenate([cb * lf, cb * ub], axis=1).astype(BF16)
            y = y + jnp.where(hmasks[h], _dot(mcat, xcat), 0.0)
        wf = (jnp.exp(tot_f - ainc_x[:, 0:GROUP_W]) * xf).astype(BF16)
        wb = (jnp.exp(aexc_x[:, GROUP_W:2 * GROUP_W]) * xb).astype(BF16)
        ef = jnp.exp(ainc_x[:, 0:GROUP_W])
        cs_s[pl.ds(t0, L), :] = jnp.exp(tot_b - aexc_x[:, GROUP_W:2 * GROUP_W])
        db_s[pl.ds(pl.multiple_of(c * SUBLANES, SUBLANES), SUBLANES), :] = jnp.broadcast_to(
            jnp.exp(tot_b), (SUBLANES, GROUP_W))
        dec_f = jnp.exp(tot_f)
        hf_new = []
        yoff = []
        for g in range(SSM_GROUPS):
            gs = slice(g * SSM_STATE, (g + 1) * SSM_STATE)
            bt = bm[:, gs].T.astype(BF16)
            yoff.append(_dot(cm[:, gs].astype(BF16), hf[g].astype(BF16)))
            hf_new.append(dec_f[:, gs] * hf[g] + _dot(bt, wf[:, gs]))
            sb_s[pl.ds(t0, L), gs] = _dot(bt, wb[:, gs])
        y = y + jnp.concatenate(yoff, axis=1) * ef
        y_s[pl.ds(t0, L), :] = y
        return tuple(hf_new)

    h0 = tuple(jnp.zeros((SSM_STATE, SSM_STATE), F32) for _ in range(SSM_GROUPS))
    lax.fori_loop(0, N_CHUNK, fwd, h0, unroll=4)

    def bwd(i, hb):
        c = N_CHUNK - 1 - i
        t0 = pl.multiple_of(c * L, L)
        cm = xc[pl.ds(t0, L), 2 * GROUP_W:3 * GROUP_W]
        dec_b = db_s[pl.ds(pl.multiple_of(c * SUBLANES, SUBLANES), 1), :]
        yoff = []
        hb_new = []
        for g in range(SSM_GROUPS):
            gs = slice(g * SSM_STATE, (g + 1) * SSM_STATE)
            yoff.append(_dot(cm[:, gs].astype(BF16), hb[g].astype(BF16)))
            hb_new.append(dec_b[:, gs] * hb[g] + sb_s[pl.ds(t0, L), gs])
        y_s[pl.ds(t0, L), :] = y_s[pl.ds(t0, L), :] + jnp.concatenate(yoff, axis=1) * cs_s[pl.ds(t0, L), :]
        return tuple(hb_new)

    lax.fori_loop(0, N_CHUNK, bwd, h0, unroll=4)

    def fin(i, _):
        r0 = pl.multiple_of(i * CONV_TILE, CONV_TILE)
        z = p_ref[0, pl.ds(r0, CONV_TILE), 0:GROUP_W].astype(F32)
        y = y_s[pl.ds(r0, CONV_TILE), :] * _silu(z)
        parts = []
        for g in range(SSM_GROUPS):
            yg = y[:, g * SSM_STATE:(g + 1) * SSM_STATE]
            parts.append(yg * lax.rsqrt(jnp.mean(yg * yg, axis=-1, keepdims=True) + EPS))
        o_ref[0, pl.ds(r0, CONV_TILE), :] = (jnp.concatenate(parts, axis=1) * nw_ref[...]).astype(BF16)
        return 0

    lax.fori_loop(0, SEQ // CONV_TILE, fin, 0)


def _head_expand_matrix():
    j = lax.broadcasted_iota(jnp.int32, (LANES, 2 * GROUP_W), 0)
    c = lax.broadcasted_iota(jnp.int32, (LANES, 2 * GROUP_W), 1)
    return jnp.tile((j == c // HEAD_DIM).astype(BF16), (3, 1))


def _mixer_c(pc, pdt, conv_w, conv_b, a_log, dt_bias, d_skip, norm_w):
    bsz = pc.shape[0]
    pad8 = lambda v: jnp.pad(v.reshape(1, 2 * N_HEADS), ((0, 0), (0, LANES - 2 * N_HEADS)))
    small = lambda shape: pl.BlockSpec(shape, lambda b: (0,) * len(shape))
    return pl.pallas_call(
        _mixer_c_kernel,
        grid=(bsz,),
        in_specs=[pl.BlockSpec((1, SEQ, 1024), lambda b: (b, 0, 0)),
                  pl.BlockSpec((1, SEQ, DT_PAD), lambda b: (b, 0, 0)),
                  small((SSM_CONV, SSM_XBC)), small((1, SSM_XBC)),
                  small((1, LANES)), small((1, 2 * GROUP_W)), small((1, LANES)),
                  small((1, GROUP_W)), small((1, GROUP_W)),
                  small((3 * LANES, 2 * GROUP_W))],
        out_specs=pl.BlockSpec((1, SEQ, GROUP_W), lambda b: (b, 0, 0)),
        out_shape=jax.ShapeDtypeStruct((bsz, SEQ, GROUP_W), BF16),
        scratch_shapes=[pltpu.VMEM((SEQ + 2 * CONV_HALO, SSM_XBC), F32),
                        pltpu.VMEM((SEQ, SSM_XBC), F32),
                        pltpu.VMEM((SEQ, GROUP_W), F32),
                        pltpu.VMEM((SEQ, GROUP_W), F32),
                        pltpu.VMEM((SEQ, GROUP_W), F32),
                        pltpu.VMEM((N_CHUNK * SUBLANES, GROUP_W), F32)],
        compiler_params=_params(1),
        name="mixer_c_ssd",
    )(pc, pdt, conv_w, conv_b.reshape(1, SSM_XBC), pad8(a_log),
      jnp.repeat(a_log.reshape(-1), HEAD_DIM).reshape(1, 2 * GROUP_W), pad8(dt_bias),
      jnp.repeat(d_skip, HEAD_DIM).reshape(1, GROUP_W), norm_w.reshape(1, GROUP_W),
      _head_expand_matrix())


NA_ROWS = SEQ // GRID_W
NA_DR = 2 * NA_WIN_H - 1
NA_DC = 2 * NA_WIN_W - 1
NA_PAIRS = NA_DR - 1


def _na_bias_kernel(rpb_ref, o_ref):
    lh = pl.program_id(0)
    cq = lax.broadcasted_iota(jnp.int32, (GRID_W, LANES), 0)
    lane = lax.broadcasted_iota(jnp.int32, (GRID_W, LANES), 1)
    second = lane >= GRID_W
    ck = jnp.where(second, lane - GRID_W, lane)
    cs = jnp.clip(cq - NA_WIN_W // 2, 0, GRID_W - NA_WIN_W)
    inside = (ck >= cs) & (ck < cs + NA_WIN_W)
    dc = ck - cq + NA_WIN_W - 1
    base = lh * (NA_DR * NA_DC)
    rows = []
    for dr in range(NA_DR):
        acc = jnp.zeros((GRID_W, LANES), F32)
        for j in range(NA_DC):
            acc = jnp.where(dc == j, rpb_ref[base + dr * NA_DC + j], acc)
        rows.append(acc)
    for p in range(NA_PAIRS):
        o_ref[0, p] = jnp.where(inside, jnp.where(second, rows[p + 1], rows[p]), NEG)


def _na_bias(na_rpb):
    n = DEPTH * N_HEADS
    return pl.pallas_call(
        _na_bias_kernel,
        grid=(n,),
        in_specs=[pl.BlockSpec(memory_space=pltpu.SMEM)],
        out_specs=pl.BlockSpec((1, NA_PAIRS, GRID_W, LANES), lambda i: (i, 0, 0, 0)),
        out_shape=jax.ShapeDtypeStruct((n, NA_PAIRS, GRID_W, LANES), F32),
        compiler_params=_params(1),
        name="na_bias_table",
    )(na_rpb.reshape(-1))


NA_UNROLL = 16


def _mixer_d_kernel(q_ref, k_ref, v_ref, g_ref, bias_ref, o_ref, vm_ref):
    nkeys = NA_WIN_H * GRID_W
    _mask_values(v_ref, vm_ref)

    def score_row(r):
        rs = jnp.clip(r - NA_WIN_H // 2, 0, NA_ROWS - NA_WIN_H)
        k0 = pl.multiple_of(rs * GRID_W, GRID_W)
        q = q_ref[0, pl.ds(pl.multiple_of(r * GRID_W, GRID_W), GRID_W), :]
        qs = jnp.concatenate([jnp.where(_head_mask(h), q, jnp.zeros_like(q)) for h in range(N_HEADS)], axis=0)
        return _dot_nt(qs, k_ref[0, pl.ds(k0, nkeys), :])

    def prob_row(r, s_all):
        delta = r - jnp.clip(r - NA_WIN_H // 2, 0, NA_ROWS - NA_WIN_H)
        ps, invs = [], []
        for h in range(N_HEADS):
            bias = jnp.concatenate(
                [bias_ref[h, 2 * i - delta + NA_WIN_H - 1] for i in range(NA_WIN_H // 2)], axis=-1)
            e = s_all[h * GRID_W:(h + 1) * GRID_W] + bias
            m = jnp.max(e, axis=-1, keepdims=True)
            p = jnp.exp(e - m)
            invs.append(1.0 / jnp.sum(p, axis=-1, keepdims=True))
            ps.append(p.astype(BF16))
        inv = jnp.broadcast_to(invs[N_HEADS - 1], (GRID_W, GROUP_W))
        for h in range(N_HEADS - 2, -1, -1):
            inv = jnp.where(_head_mask(h), invs[h], inv)
        return jnp.concatenate(ps, axis=-1), inv

    def value_row(r, pcat, inv):
        rs = jnp.clip(r - NA_WIN_H // 2, 0, NA_ROWS - NA_WIN_H)
        k0 = pl.multiple_of(rs * GRID_W, GRID_W)
        q0 = pl.multiple_of(r * GRID_W, GRID_W)
        vcat = jnp.concatenate([vm_ref[pl.ds(pl.multiple_of(h * SEQ + k0, GRID_W), nkeys), :]
                                for h in range(N_HEADS)], axis=0)
        acc = _dot(pcat, vcat)
        g = g_ref[0, pl.ds(q0, GRID_W), :].astype(F32)
        o_ref[0, pl.ds(q0, GRID_W), :] = (acc * inv * _silu(g)).astype(BF16)

    def rstep(i, _):
        rows = [i * NA_UNROLL + u for u in range(NA_UNROLL)]
        s_next = score_row(rows[0])
        for u, r in enumerate(rows):
            s_cur = s_next
            if u + 1 < NA_UNROLL:
                s_next = score_row(rows[u + 1])
            value_row(r, *prob_row(r, s_cur))
        return 0

    lax.fori_loop(0, NA_ROWS // NA_UNROLL, rstep, 0)


def _mixer_d(pd, bias, l):
    bsz = pd.shape[0]
    return pl.pallas_call(
        _mixer_d_kernel,
        grid=(bsz,),
        in_specs=_attn_specs() + [pl.BlockSpec((N_HEADS, NA_PAIRS, GRID_W, LANES), lambda b: (l, 0, 0, 0))],
        out_specs=pl.BlockSpec((1, SEQ, GROUP_W), lambda b: (b, 0, 0)),
        out_shape=jax.ShapeDtypeStruct((bsz, SEQ, GROUP_W), BF16),
        scratch_shapes=[pltpu.VMEM((N_HEADS * SEQ, GROUP_W), BF16)],
        compiler_params=_params(1),
        name="mixer_d_neighbourhood",
    )(pd, pd, pd, pd, bias)


def kernel(x, c, norm_w, ada_w, ada_b, w_in, diff_lambda, diff_norm_w, conv_w, conv_b, ssm_a_log,
           ssm_dt_bias, ssm_d, ssm_norm_w, na_rpb, w_out, final_norm_w):
    bsz = x.shape[0]
    n = 2 * N_HEADS
    slopes = [2.0 ** (-8.0 * i / n) for i in range(1, n + 1)]
    slopes_a, slopes_b = tuple(slopes[0::2]), tuple(slopes[1::2])
    mod4 = _adaln(c, ada_w, ada_b).reshape(DEPTH, bsz, 1, 3 * D_MODEL)
    na_bias = _na_bias(na_rpb)
    w_main, w_dt = _prep_w_in(w_in)
    norm_w3 = norm_w.reshape(DEPTH, 1, D_MODEL)
    final_w = final_norm_w.reshape(1, D_MODEL)
    pa, pb, pc, pd, pdt = _inproj(x, mod4, norm_w3, w_main, w_dt, 0)
    for l in range(DEPTH):
        lam_init = 0.8 - 0.6 * math.exp(-0.3 * l)
        ya = _mixer_a(pa, slopes_a)
        yb = _mixer_b(pb, diff_lambda[l], jnp.tile(diff_norm_w[l], N_HEADS).reshape(1, GROUP_W),
                      slopes_b, lam_init)
        yc = _mixer_c(pc, pdt, conv_w[l], conv_b[l], ssm_a_log[l], ssm_dt_bias[l], ssm_d[l], ssm_norm_w[l])
        yd = _mixer_d(pd, na_bias, l)
        if l + 1 < DEPTH:
            x, pa, pb, pc, pd, pdt = _midproj((ya, yb, yc, yd), w_out, x, mod4, norm_w3, w_main, w_dt, l)
        else:
            x = _outproj((ya, yb, yc, yd), w_out, x, mod4, final_w, l)
    return x
```
